```python
import math
import jax, jax.numpy as jnp
from jax import lax
import numpy as np


D_MODEL = 1024
BATCH = 8
SEQ = 2048
DEPTH = 2
DEC_BATCH = 32
DEC_SEQ = 2048
PAST_LEN = 128

HEAD_DIM = 64
ROPE_THETA = 10000.0
NEG_INF = -1e30
RMS_EPS = 1e-6

MLA_HEADS = 8
MLA_Q_RANK = 384
MLA_KV_RANK = 256
MLA_NOPE = 64
MLA_ROPE = 32
MLA_V = 64
MLA_Q_BLOCK = 128

SWA_HEADS = 8
SWA_KV_HEADS = 2
SWA_GROUP = SWA_HEADS // SWA_KV_HEADS
SWA_HALF_WINDOW = 128
SWA_BLOCK = 128

DIL_HEADS = 8
DIL_BRANCHES = ((128, 1), (512, 4), (2048, 16))

NA_HEADS = 8
GRID_W = 64
NA_MAX_KH = 8
NA_KW = 16
NA_COL_BLOCK = 16

AB_IN = MLA_Q_RANK + MLA_KV_RANK + MLA_ROPE + (SWA_HEADS + 2 * SWA_KV_HEADS) * HEAD_DIM
AB_OUT = MLA_HEADS * MLA_V + SWA_HEADS * HEAD_DIM
CD_IN = 3 * (DIL_HEADS + NA_HEADS) * HEAD_DIM
CD_OUT = (DIL_HEADS + NA_HEADS) * HEAD_DIM

N_EXPERTS = 16
EC_CAPACITY_FACTOR = 2
D_EXPERT = 1024

N_EVEN = (DEPTH + 1) // 2
N_ODD = DEPTH // 2

kernel_name = 'hybrid_bidir_mla_swa_dilated_natten_ec_moe'


def rms_norm(x, g):
    xf = x.astype(jnp.float32)
    y = xf * lax.rsqrt(jnp.mean(xf * xf, axis=-1, keepdims=True) + RMS_EPS)
    return (y * g.astype(jnp.float32)).astype(x.dtype)


def rope_tables(n, dim):
    inv = 1.0 / (ROPE_THETA ** (jnp.arange(0, dim, 2, dtype=jnp.float32) / dim))
    ang = jnp.arange(n, dtype=jnp.float32)[:, None] * inv[None, :]
    return jnp.cos(ang), jnp.sin(ang)


def apply_rope(x, cos, sin):
    x1, x2 = jnp.split(x, 2, axis=-1)
    c = cos[None, :, None, :].astype(x.dtype)
    s = sin[None, :, None, :].astype(x.dtype)
    return jnp.concatenate([x1 * c - x2 * s, x1 * s + x2 * c], axis=-1)


def banded_attention(q, k, v, half_window, blk):
    n, L, G, R, dh = q.shape
    nb = -(-L // blk)
    pad = nb * blk - L
    qb = jnp.pad(q, ((0, 0), (0, pad), (0, 0), (0, 0), (0, 0))).reshape(n, nb, blk, G, R, dh)

    def windows(t):
        tp = jnp.pad(t, ((0, 0), (blk, pad + blk), (0, 0), (0, 0))).reshape(n, nb + 2, blk, G, dh)
        return jnp.concatenate([tp[:, :-2], tp[:, 1:-1], tp[:, 2:]], axis=2)

    kw, vw = windows(k), windows(v)
    qpos = jnp.arange(nb * blk).reshape(nb, blk)
    kpos = jnp.arange(-blk, (nb + 1) * blk).reshape(nb + 2, blk)
    kpos = jnp.concatenate([kpos[:-2], kpos[1:-1], kpos[2:]], axis=1)
    valid = ((jnp.abs(qpos[:, :, None] - kpos[:, None, :]) <= half_window)
             & (kpos[:, None, :] >= 0) & (kpos[:, None, :] < L))
    s = jnp.einsum('nbqgrd,nbkgd->nbgrqk', qb, kw).astype(jnp.float32) * (dh ** -0.5)
    s = jnp.where(valid[None, :, None, None], s, NEG_INF)
    m = jnp.max(s, axis=-1, keepdims=True)
    p = jnp.exp(s - m)
    den = jnp.sum(p, axis=-1, keepdims=True)
    o = jnp.einsum('nbgrqk,nbkgd->nbqgrd', (p / den).astype(v.dtype), vw)
    lse = jnp.transpose((m + jnp.log(den))[..., 0], (0, 1, 4, 2, 3))
    o = o.reshape(n, nb * blk, G, R, dh)[:, :L]
    lse = lse.reshape(n, nb * blk, G, R)[:, :L]
    return o, lse


def mla_attention(q_nope, q_rope, k_nope, k_rope, v):
    B, S, H, _ = q_nope.shape
    nb = S // MLA_Q_BLOCK
    scale = (MLA_NOPE + MLA_ROPE) ** -0.5
    qn = q_nope.reshape(B, nb, MLA_Q_BLOCK, H, MLA_NOPE).transpose(1, 0, 2, 3, 4)
    qr = q_rope.reshape(B, nb, MLA_Q_BLOCK, H, MLA_ROPE).transpose(1, 0, 2, 3, 4)

    def block(args):
        qn_b, qr_b = args
        s = (jnp.einsum('bqhd,bkhd->bhqk', qn_b, k_nope)
             + jnp.einsum('bqhd,bkd->bhqk', qr_b, k_rope)).astype(jnp.float32) * scale
        p = jax.nn.softmax(s, axis=-1)
        return jnp.einsum('bhqk,bkhd->bqhd', p.astype(v.dtype), v)

    out = lax.map(block, (qn, qr))
    return out.transpose(1, 0, 2, 3, 4).reshape(B, S, H, MLA_V)


def dilated_attention(q, k, v):
    B, S, H, dh = q.shape
    outs, lses = [], []
    for window, dil in DIL_BRANCHES:
        L = S // dil
        half = window // (2 * dil)

        def to_stream(t):
            return t.reshape(B, L, dil, H, dh).transpose(0, 2, 1, 3, 4).reshape(B * dil, L, H, dh)

        o, lse = banded_attention(to_stream(q)[:, :, :, None, :], to_stream(k), to_stream(v), half, half)
        outs.append(o[:, :, :, 0].reshape(B, dil, L, H, dh).transpose(0, 2, 1, 3, 4).reshape(B, S, H, dh))
        lses.append(lse[..., 0].reshape(B, dil, L, H).transpose(0, 2, 1, 3).reshape(B, S, H))
    w = jax.nn.softmax(jnp.stack(lses), axis=0)
    return jnp.einsum('nbsh,nbshd->bshd', w.astype(q.dtype), jnp.stack(outs))


def neighborhood_attention(q, k, v, rpb):
    B, S, H, dh = q.shape
    rows = S // GRID_W
    kh = min(NA_MAX_KH, rows)
    scale = dh ** -0.5
    qg = q.reshape(B, rows, GRID_W, H, dh).transpose(1, 0, 2, 3, 4)
    kg = k.reshape(B, rows, GRID_W, H, dh)
    vg = v.reshape(B, rows, GRID_W, H, dh)
    ncb = GRID_W // NA_COL_BLOCK
    span = NA_COL_BLOCK + NA_KW
    col_q = jnp.arange(GRID_W).reshape(ncb, NA_COL_BLOCK)
    kb_start = jnp.clip(jnp.arange(ncb) * NA_COL_BLOCK - NA_KW // 2, 0, GRID_W - span)
    col_k = kb_start[:, None] + jnp.arange(span)[None, :]
    win_c0 = jnp.clip(col_q - NA_KW // 2, 0, GRID_W - NA_KW)
    col_valid = ((col_k[:, None, :] >= win_c0[..., None])
                 & (col_k[:, None, :] < win_c0[..., None] + NA_KW))
    col_idx = jnp.clip(col_k[:, None, :] - col_q[..., None] + NA_KW - 1, 0, 2 * NA_KW - 2)
    row_start = jnp.clip(jnp.arange(rows) - kh // 2, 0, rows - kh)

    def row_block(args):
        r, r0, q_row = args
        k_rows = lax.dynamic_slice_in_dim(kg, r0, kh, axis=1)
        v_rows = lax.dynamic_slice_in_dim(vg, r0, kh, axis=1)
        k_cb = k_rows[:, :, col_k]
        v_cb = v_rows[:, :, col_k]
        q_cb = q_row.reshape(B, ncb, NA_COL_BLOCK, H, dh)
        s = jnp.einsum('bcqhd,bjcshd->bhcqjs', q_cb, k_cb).astype(jnp.float32) * scale
        row_idx = r0 + jnp.arange(kh) - r + NA_MAX_KH - 1
        bias = rpb[:, row_idx[None, None, :, None], col_idx[:, :, None, :]]
        s = s + bias[None].astype(jnp.float32)
        s = jnp.where(col_valid[None, None, :, :, None, :], s, NEG_INF)
        p = jax.nn.softmax(s, axis=(-2, -1))
        o = jnp.einsum('bhcqjs,bjcshd->bcqhd', p.astype(v.dtype), v_cb)
        return o.reshape(B, GRID_W, H, dh)

    out = lax.map(row_block, (jnp.arange(rows), row_start, qg))
    return out.transpose(1, 0, 2, 3, 4).reshape(B, S, H, dh)


def mixer_mla_swa(h, w_in, q_norm, w_uq, kv_norm, w_ukv, sink, w_out):
    B, S, _ = h.shape
    proj = h @ w_in
    cuts = np.cumsum([MLA_Q_RANK, MLA_KV_RANK, MLA_ROPE, SWA_HEADS * HEAD_DIM, SWA_KV_HEADS * HEAD_DIM])
    c_q, c_kv, k_r, q_b, k_b, v_b = jnp.split(proj, [int(c) for c in cuts], axis=-1)
    q = (rms_norm(c_q, q_norm) @ w_uq).reshape(B, S, MLA_HEADS, MLA_NOPE + MLA_ROPE)
    kv = (rms_norm(c_kv, kv_norm) @ w_ukv).reshape(B, S, MLA_HEADS, MLA_NOPE + MLA_V)
    cos_r, sin_r = rope_tables(S, MLA_ROPE)
    q_rope = apply_rope(q[..., MLA_NOPE:], cos_r, sin_r)
    k_rope = apply_rope(k_r[:, :, None, :], cos_r, sin_r)[:, :, 0]
    o_a = mla_attention(q[..., :MLA_NOPE], q_rope, kv[..., :MLA_NOPE], k_rope, kv[..., MLA_NOPE:])
    cos, sin = rope_tables(S, HEAD_DIM)
    qb = apply_rope(q_b.reshape(B, S, SWA_HEADS, HEAD_DIM), cos, sin).reshape(B, S, SWA_KV_HEADS, SWA_GROUP, HEAD_DIM)
    kb = apply_rope(k_b.reshape(B, S, SWA_KV_HEADS, HEAD_DIM), cos, sin)
    vb = v_b.reshape(B, S, SWA_KV_HEADS, HEAD_DIM)
    o_b, lse = banded_attention(qb, kb, vb, SWA_HALF_WINDOW, SWA_BLOCK)
    keep = jax.nn.sigmoid(lse - sink.reshape(SWA_KV_HEADS, SWA_GROUP).astype(jnp.float32))
    o_b = o_b * keep[..., None].astype(o_b.dtype)
    o = jnp.concatenate([o_a.reshape(B, S, MLA_HEADS * MLA_V), o_b.reshape(B, S, SWA_HEADS * HEAD_DIM)], axis=-1)
    return o @ w_out


def mixer_dilated_natten(h, w_in, rpb, w_out):
    B, S, _ = h.shape
    proj = h @ w_in
    q_c, k_c, v_c, q_d, k_d, v_d = jnp.split(proj, 6, axis=-1)
    cos, sin = rope_tables(S, HEAD_DIM)
    shp = (B, S, DIL_HEADS, HEAD_DIM)
    o_c = dilated_attention(apply_rope(q_c.reshape(shp), cos, sin), apply_rope(k_c.reshape(shp), cos, sin), v_c.reshape(shp))
    shp_d = (B, S, NA_HEADS, HEAD_DIM)
    o_d = neighborhood_attention(q_d.reshape(shp_d), k_d.reshape(shp_d), v_d.reshape(shp_d), rpb)
    o = jnp.concatenate([o_c.reshape(B, S, DIL_HEADS * HEAD_DIM), o_d.reshape(B, S, NA_HEADS * HEAD_DIM)], axis=-1)
    return o @ w_out


def expert_choice_ffn(h, w_router, w_gate, w_up, w_down):
    shp = h.shape
    xt = h.reshape(-1, shp[-1])
    T = xt.shape[0]
    cap = EC_CAPACITY_FACTOR * T // N_EXPERTS
    aff = jax.nn.softmax((xt @ w_router).astype(jnp.float32), axis=-1)
    gate, idx = lax.top_k(aff.T, cap)
    xe = xt[idx]
    g = jnp.einsum('ecd,edf->ecf', xe, w_gate)
    u = jnp.einsum('ecd,edf->ecf', xe, w_up)
    ye = jnp.einsum('ecf,efd->ecd', jax.nn.silu(g) * u, w_down) * gate[..., None].astype(xt.dtype)
    y = jnp.zeros_like(xt).at[idx.reshape(-1)].add(ye.reshape(-1, shp[-1]))
    return y.reshape(shp)


def trunk(x, norm_mix, norm_ffn, norm_final, w_in_ab, mla_q_norm, mla_w_uq, mla_kv_norm, mla_w_ukv,
          swa_sink, w_out_ab, w_in_cd, na_rpb, w_out_cd, w_router, w_gate, w_up, w_down):
    for l in range(DEPTH):
        i = l // 2
        h = rms_norm(x, norm_mix[l])
        if l % 2 == 0:
            x = x + mixer_mla_swa(h, w_in_ab[i], mla_q_norm[i], mla_w_uq[i], mla_kv_norm[i], mla_w_ukv[i],
                                  swa_sink[i], w_out_ab[i])
        else:
            x = x + mixer_dilated_natten(h, w_in_cd[i], na_rpb[i], w_out_cd[i])
        h = rms_norm(x, norm_ffn[l])
        x = x + expert_choice_ffn(h, w_router[l], w_gate[l], w_up[l], w_down[l])
    return rms_norm(x, norm_final)


def setup_inputs(seed: int = 0) -> dict:
    key = jax.random.key(seed)
    ks = jax.random.split(key, 20)

    def nrm(k, shape, scale):
        return jax.random.normal(k, shape, jnp.float32) * scale

    return {
        'x_prompt': nrm(ks[0], (BATCH, SEQ, D_MODEL), 1.0),
        'x_sample': nrm(ks[1], (DEC_BATCH, DEC_SEQ, D_MODEL), 1.0),
        'norm_mix': 1.0 + nrm(ks[2], (DEPTH, D_MODEL), 0.02),
        'norm_ffn': 1.0 + nrm(ks[3], (DEPTH, D_MODEL), 0.02),
        'norm_final': 1.0 + nrm(ks[4], (D_MODEL,), 0.02),
        'w_in_ab': nrm(ks[5], (N_EVEN, D_MODEL, AB_IN), D_MODEL ** -0.5),
        'mla_q_norm': 1.0 + nrm(ks[6], (N_EVEN, MLA_Q_RANK), 0.02),
        'mla_w_uq': nrm(ks[7], (N_EVEN, MLA_Q_RANK, MLA_HEADS * (MLA_NOPE + MLA_ROPE)), MLA_Q_RANK ** -0.5),
        'mla_kv_norm': 1.0 + nrm(ks[8], (N_EVEN, MLA_KV_RANK), 0.02),
        'mla_w_ukv': nrm(ks[9], (N_EVEN, MLA_KV_RANK, MLA_HEADS * (MLA_NOPE + MLA_V)), MLA_KV_RANK ** -0.5),
        'swa_sink': nrm(ks[10], (N_EVEN, SWA_HEADS), 1.0),
        'w_out_ab': nrm(ks[11], (N_EVEN, AB_OUT, D_MODEL), AB_OUT ** -0.5),
        'w_in_cd': nrm(ks[12], (N_ODD, D_MODEL, CD_IN), D_MODEL ** -0.5),
        'na_rpb': nrm(ks[13], (N_ODD, NA_HEADS, 2 * NA_MAX_KH - 1, 2 * NA_KW - 1), 0.1),
        'w_out_cd': nrm(ks[14], (N_ODD, CD_OUT, D_MODEL), CD_OUT ** -0.5),
        'w_router': nrm(ks[15], (DEPTH, D_MODEL, N_EXPERTS), D_MODEL ** -0.5),
        'w_gate': nrm(ks[16], (DEPTH, N_EXPERTS, D_MODEL, D_EXPERT), D_MODEL ** -0.5),
        'w_up': nrm(ks[17], (DEPTH, N_EXPERTS, D_MODEL, D_EXPERT), D_MODEL ** -0.5),
        'w_down': nrm(ks[18], (DEPTH, N_EXPERTS, D_EXPERT, D_MODEL), D_EXPERT ** -0.5),
    }


def reference(x_prompt, x_sample, norm_mix, norm_ffn, norm_final, w_in_ab, mla_q_norm, mla_w_uq, mla_kv_norm,
              mla_w_ukv, swa_sink, w_out_ab, w_in_cd, na_rpb, w_out_cd, w_router, w_gate, w_up, w_down):
    y_prompt = trunk(x_prompt, norm_mix, norm_ffn, norm_final, w_in_ab, mla_q_norm, mla_w_uq, mla_kv_norm,
                     mla_w_ukv, swa_sink, w_out_ab, w_in_cd, na_rpb, w_out_cd, w_router, w_gate, w_up, w_down)
    y_sample = trunk(x_sample, norm_mix, norm_ffn, norm_final, w_in_ab, mla_q_norm, mla_w_uq, mla_kv_norm,
                     mla_w_ukv, swa_sink, w_out_ab, w_in_cd, na_rpb, w_out_cd, w_router, w_gate, w_up, w_down)
    return (y_prompt, y_sample)
```

```python
import functools

import jax
import jax.numpy as jnp
import numpy as np
from jax import lax
from jax.experimental import pallas as pl
from jax.experimental.pallas import tpu as pltpu

D_MODEL = 1024
SEQ = 2048
HEAD_DIM = 64
ROPE_THETA = 10000.0
NEG_INF = -1e30
RMS_EPS = 1e-6

MLA_HEADS = 8
MLA_Q_RANK = 384
MLA_KV_RANK = 256
MLA_NOPE = 64
MLA_ROPE = 32
MLA_V = 64

SWA_HEADS = 8
SWA_KV_HEADS = 2
SWA_GROUP = SWA_HEADS // SWA_KV_HEADS
SWA_HALF_WINDOW = 128

DIL_HEADS = 8
DIL_BRANCHES = ((128, 1), (512, 4), (2048, 16))

NA_HEADS = 8
GRID_W = 64
NA_KH = 8
NA_KW = 16

N_EXPERTS = 16
EC_CAPACITY_FACTOR = 2

LANES = 128
VMEM_LIMIT = 56 * 1024 * 1024

BF16 = jnp.bfloat16
F32 = jnp.float32


def _cparams(*sem):
    return pltpu.CompilerParams(dimension_semantics=sem, vmem_limit_bytes=VMEM_LIMIT)


def _dot(a, b):
    return jnp.dot(a, b, preferred_element_type=F32)


def _dot_nt(a, b):
    return lax.dot_general(a, b, (((1,), (1,)), ((), ())), preferred_element_type=F32)


def _rms(x, g):
    return x * lax.rsqrt(jnp.mean(x * x, axis=-1, keepdims=True) + RMS_EPS) * g


def _rot_cols(w, n_heads, dim):
    k = w.shape[0]
    w = w.reshape(k, n_heads, dim)
    return jnp.concatenate([-w[..., dim // 2:], w[..., :dim // 2]], axis=-1).reshape(k, n_heads * dim)


def _rope_tables():
    def tables(dim):
        inv = 1.0 / (ROPE_THETA ** (jnp.arange(0, dim, 2, dtype=F32) / dim))
        ang = jnp.arange(SEQ, dtype=F32)[:, None] * inv[None, :]
        c, s = jnp.cos(ang), jnp.sin(ang)
        return jnp.concatenate([c, c], axis=-1), jnp.concatenate([s, s], axis=-1)

    c32, s32 = tables(MLA_ROPE)
    one = jnp.ones((SEQ, MLA_NOPE), F32)
    zero = jnp.zeros((SEQ, MLA_NOPE), F32)
    z32 = jnp.zeros((SEQ, LANES - MLA_NOPE - MLA_ROPE), F32)
    cos_a = jnp.concatenate([one, c32, z32], axis=-1)
    sin_a = jnp.concatenate([zero, s32, z32], axis=-1)
    c64, s64 = tables(HEAD_DIM)
    cos_b = jnp.concatenate([c64, c64], axis=-1)
    sin_b = jnp.concatenate([s64, s64], axis=-1)
    return cos_a, sin_a, cos_b, sin_b


def _pad_heads(w, n_heads, dim):
    k = w.shape[0]
    w = w.reshape(k, n_heads, dim)
    return jnp.pad(w, ((0, 0), (0, 0), (0, LANES - dim))).reshape(k, n_heads * LANES)


def _prep_ab(w_in, w_uq, w_ukv):
    cq = w_in[:, :MLA_Q_RANK]
    o = MLA_Q_RANK
    ckv = w_in[:, o:o + MLA_KV_RANK]
    o += MLA_KV_RANK
    kr = w_in[:, o:o + MLA_ROPE]
    o += MLA_ROPE
    qb = w_in[:, o:o + SWA_HEADS * HEAD_DIM]
    o += SWA_HEADS * HEAD_DIM
    kb = w_in[:, o:o + SWA_KV_HEADS * HEAD_DIM]
    o += SWA_KV_HEADS * HEAD_DIM
    vb = w_in[:, o:]
    kr_rot = _rot_cols(kr, 1, MLA_ROPE)
    lead = jnp.zeros((D_MODEL, MLA_NOPE), F32)
    tail = jnp.zeros((D_MODEL, LANES - MLA_NOPE - MLA_ROPE), F32)
    kr_pad = jnp.concatenate([lead, kr, tail], axis=1)
    kr_rot_pad = jnp.concatenate([lead, kr_rot, tail], axis=1)
    w_big = jnp.concatenate([cq, ckv, qb, _rot_cols(qb, SWA_HEADS, HEAD_DIM), kb,
                             _rot_cols(kb, SWA_KV_HEADS, HEAD_DIM), vb, kr_pad, kr_rot_pad], axis=1)
    dq = MLA_NOPE + MLA_ROPE
    wq = w_uq.reshape(MLA_Q_RANK, MLA_HEADS, dq)
    rope = wq[..., MLA_NOPE:]
    rope_rot = jnp.concatenate([-rope[..., MLA_ROPE // 2:], rope[..., :MLA_ROPE // 2]], axis=-1)
    wq_rot = jnp.concatenate([jnp.zeros_like(wq[..., :MLA_NOPE]), rope_rot], axis=-1)
    wq_pad = _pad_heads(wq.reshape(MLA_Q_RANK, -1), MLA_HEADS, dq)
    wq_rot_pad = _pad_heads(wq_rot.reshape(MLA_Q_RANK, -1), MLA_HEADS, dq)
    w_q = jnp.concatenate([wq_pad, wq_rot_pad], axis=1)
    wkv = w_ukv.reshape(MLA_KV_RANK, MLA_HEADS, MLA_NOPE + MLA_V)
    wk_pad = _pad_heads(wkv[..., :MLA_NOPE].reshape(MLA_KV_RANK, -1), MLA_HEADS, MLA_NOPE)
    wv = wkv[..., MLA_NOPE:].reshape(MLA_KV_RANK, MLA_HEADS * MLA_V)
    w_kv = jnp.concatenate([wk_pad, wv], axis=1)
    return w_big.astype(BF16), w_q.astype(BF16), w_kv.astype(BF16)


def _prep_cd(w_in):
    n = DIL_HEADS * HEAD_DIM
    q_c, k_c, v_c, q_d, k_d, v_d = [w_in[:, i * n:(i + 1) * n] for i in range(6)]
    w_big = jnp.concatenate([q_c, _rot_cols(q_c, DIL_HEADS, HEAD_DIM), k_c, _rot_cols(k_c, DIL_HEADS, HEAD_DIM),
                             v_c, q_d, k_d, v_d], axis=1)
    return w_big.astype(BF16)


def _dilation_counts():
    d = np.arange(SEQ)[:, None] - np.arange(SEQ)[None, :]
    c = np.zeros((SEQ, SEQ), np.float32)
    for window, dil in DIL_BRANCHES:
        c += ((d % dil == 0) & (np.abs(d) <= window // 2)).astype(np.float32)
    return jnp.asarray(c, dtype=BF16)


def _pre_ab_kernel(x_ref, g_ref, wbig_ref, qn_ref, wq_ref, kvn_ref, wkv_ref, ca_ref, sa_ref, cb_ref, sb_ref,
                   qm_ref, km_ref, vm_ref, qs_ref, ks_ref, vs_ref):
    h = _rms(x_ref[...], g_ref[...]).astype(BF16)
    proj = _dot(h, wbig_ref[...])
    ca, sa, cb, sb = ca_ref[...], sa_ref[...], cb_ref[...], sb_ref[...]
    o = 0
    c_q = proj[:, o:o + MLA_Q_RANK]
    o += MLA_Q_RANK
    c_kv = proj[:, o:o + MLA_KV_RANK]
    o += MLA_KV_RANK
    nq = SWA_HEADS * HEAD_DIM
    q_b, q_b_rot = proj[:, o:o + nq], proj[:, o + nq:o + 2 * nq]
    o += 2 * nq
    k_b, k_b_rot, v_b = proj[:, o:o + LANES], proj[:, o + LANES:o + 2 * LANES], proj[:, o + 2 * LANES:o + 3 * LANES]
    o += 3 * LANES
    k_r, k_r_rot = proj[:, o:o + LANES], proj[:, o + LANES:o + 2 * LANES]

    qs_ref[...] = jnp.concatenate(
        [q_b[:, i * LANES:(i + 1) * LANES] * cb + q_b_rot[:, i * LANES:(i + 1) * LANES] * sb
         for i in range(nq // LANES)], axis=1).astype(BF16)
    ks_ref[...] = (k_b * cb + k_b_rot * sb).astype(BF16)
    vs_ref[...] = v_b.astype(BF16)

    q_all = _dot(_rms(c_q, qn_ref[...]).astype(BF16), wq_ref[...])
    nh = MLA_HEADS * LANES
    qm_ref[...] = jnp.concatenate(
        [q_all[:, i * LANES:(i + 1) * LANES] * ca + q_all[:, nh + i * LANES:nh + (i + 1) * LANES] * sa
         for i in range(MLA_HEADS)], axis=1).astype(BF16)
    kv = _dot(_rms(c_kv, kvn_ref[...]).astype(BF16), wkv_ref[...])
    k_rope = k_r * ca + k_r_rot * sa
    km_ref[...] = jnp.concatenate(
        [kv[:, i * LANES:(i + 1) * LANES] + k_rope for i in range(MLA_HEADS)], axis=1).astype(BF16)
    vm_ref[...] = kv[:, nh:].astype(BF16)


def _pre_ab(x, g, w_big, qn, w_q, kvn, w_kv, tabs, tm=512):
    t = x.shape[0]
    nsb = SEQ // tm
    full = lambda a: pl.BlockSpec(a.shape, lambda i: (0,) * a.ndim)
    tab = pl.BlockSpec((tm, LANES), lambda i: (i % nsb, 0))
    row = lambda n: pl.BlockSpec((tm, n), lambda i: (i, 0))
    outs = [(MLA_HEADS * LANES, BF16), (MLA_HEADS * LANES, BF16), (MLA_HEADS * MLA_V, BF16),
            (SWA_HEADS * HEAD_DIM, BF16), (LANES, BF16), (LANES, BF16)]
    return pl.pallas_call(
        _pre_ab_kernel,
        grid=(t // tm,),
        in_specs=[row(D_MODEL), full(g), full(w_big), full(qn), full(w_q), full(kvn), full(w_kv), tab, tab, tab, tab],
        out_specs=[row(n) for n, _ in outs],
        out_shape=[jax.ShapeDtypeStruct((t, n), d) for n, d in outs],
        compiler_params=_cparams("parallel"),
        name="pre_ab",
    )(x, g, w_big, qn, w_q, kvn, w_kv, *tabs)


def _pre_cd_kernel(x_ref, g_ref, wbig_ref, cb_ref, sb_ref, qc_ref, kc_ref, vc_ref, qd_ref, kd_ref, vd_ref):
    h = _rms(x_ref[...], g_ref[...]).astype(BF16)
    cb, sb = cb_ref[...], sb_ref[...]
    n = DIL_HEADS * HEAD_DIM
    w = wbig_ref

    def roped(o):
        a = _dot(h, w[:, o:o + n])
        b = _dot(h, w[:, o + n:o + 2 * n])
        return jnp.concatenate([a[:, i * LANES:(i + 1) * LANES] * cb + b[:, i * LANES:(i + 1) * LANES] * sb
                                for i in range(n // LANES)], axis=1).astype(BF16)

    qc_ref[...] = roped(0)
    kc_ref[...] = roped(2 * n)
    vc_ref[...] = _dot(h, w[:, 4 * n:5 * n]).astype(BF16)
    qd_ref[...] = _dot(h, w[:, 5 * n:6 * n]).astype(BF16)
    kd_ref[...] = _dot(h, w[:, 6 * n:7 * n]).astype(BF16)
    vd_ref[...] = _dot(h, w[:, 7 * n:8 * n]).astype(BF16)


def _pre_cd(x, g, w_big, tabs, tm=512):
    t = x.shape[0]
    nsb = SEQ // tm
    n = DIL_HEADS * HEAD_DIM
    full = lambda a: pl.BlockSpec(a.shape, lambda i: (0,) * a.ndim)
    tab = pl.BlockSpec((tm, LANES), lambda i: (i % nsb, 0))
    row = lambda m: pl.BlockSpec((tm, m), lambda i: (i, 0))
    return pl.pallas_call(
        _pre_cd_kernel,
        grid=(t // tm,),
        in_specs=[row(D_MODEL), full(g), full(w_big), tab, tab],
        out_specs=[row(n)] * 6,
        out_shape=[jax.ShapeDtypeStruct((t, n), BF16)] * 6,
        compiler_params=_cparams("parallel"),
        name="pre_cd",
    )(x, g, w_big, tabs[2], tabs[3])


def _dense_attn_kernel(*refs, n_heads, dq, dv, scale, weighted):
    if weighted:
        q_ref, k_ref, v_ref, c_ref, o_ref = refs
        cnt = c_ref[...].astype(F32)
        visible = cnt > 0.0
    else:
        q_ref, k_ref, v_ref, o_ref = refs
    outs = []
    for h in range(n_heads):
        q = q_ref[0, :, h * dq:(h + 1) * dq]
        k = k_ref[0, :, h * dq:(h + 1) * dq]
        v = v_ref[0, :, h * dv:(h + 1) * dv]
        s = _dot_nt(q, k) * scale
        if weighted:
            s = jnp.where(visible, s, NEG_INF)
        m = jnp.max(s, axis=-1, keepdims=True)
        p = jnp.exp(s - m)
        if weighted:
            p = p * cnt
        l = jnp.sum(p, axis=-1, keepdims=True)
        outs.append(_dot(p.astype(BF16), v) / l)
    o_ref[0] = jnp.concatenate(outs, axis=1).astype(o_ref.dtype)


def _dense_attn(q, k, v, counts, *, n_heads, dq, dv, scale, tq=256):
    b = q.shape[0]
    weighted = counts is not None
    in_specs = [pl.BlockSpec((1, tq, n_heads * dq), lambda i, j: (i, j, 0)),
                pl.BlockSpec((1, SEQ, n_heads * dq), lambda i, j: (i, 0, 0)),
                pl.BlockSpec((1, SEQ, n_heads * dv), lambda i, j: (i, 0, 0))]
    args = [q, k, v]
    if weighted:
        in_specs.append(pl.BlockSpec((tq, SEQ), lambda i, j: (j, 0)))
        args.append(counts)
    return pl.pallas_call(
        functools.partial(_dense_attn_kernel, n_heads=n_heads, dq=dq, dv=dv, scale=scale, weighted=weighted),
        grid=(b, SEQ // tq),
        in_specs=in_specs,
        out_specs=pl.BlockSpec((1, tq, n_heads * dv), lambda i, j: (i, j, 0)),
        out_shape=jax.ShapeDtypeStruct((b, SEQ, n_heads * dv), BF16),
        compiler_params=_cparams("parallel", "parallel"),
        name="dilated_attn" if weighted else "mla_attn",
    )(*args)


def _swa_kernel(q_ref, kp_ref, kc_ref, kn_ref, vp_ref, vc_ref, vn_ref, sink_ref, o_ref, *, blk):
    i = pl.program_id(1)
    qpos = i * blk + lax.broadcasted_iota(jnp.int32, (blk, 3 * blk), 0)
    kpos = (i - 1) * blk + lax.broadcasted_iota(jnp.int32, (blk, 3 * blk), 1)
    valid = (jnp.abs(qpos - kpos) <= SWA_HALF_WINDOW) & (kpos >= 0) & (kpos < SEQ)
    scale = HEAD_DIM ** -0.5
    outs = []
    for g in range(SWA_KV_HEADS):
        sl = slice(g * HEAD_DIM, (g + 1) * HEAD_DIM)
        k = jnp.concatenate([kp_ref[0, :, sl], kc_ref[0, :, sl], kn_ref[0, :, sl]], axis=0)
        v = jnp.concatenate([vp_ref[0, :, sl], vc_ref[0, :, sl], vn_ref[0, :, sl]], axis=0)
        for r in range(SWA_GROUP):
            hs = slice((g * SWA_GROUP + r) * HEAD_DIM, (g * SWA_GROUP + r + 1) * HEAD_DIM)
            s = jnp.where(valid, _dot_nt(q_ref[0, :, hs], k) * scale, NEG_INF)
            m = jnp.max(s, axis=-1, keepdims=True)
            p = jnp.exp(s - m)
            l = jnp.sum(p, axis=-1, keepdims=True)
            o = _dot(p.astype(BF16), v) / l
            lse = m + jnp.log(l)
            keep = jax.nn.sigmoid(lse - sink_ref[:, hs])
            outs.append(o * keep)
    o_ref[0] = jnp.concatenate(outs, axis=1).astype(o_ref.dtype)


def _swa_attn(q, k, v, sink_lanes, blk=128):
    b = q.shape[0]
    nb = SEQ // blk
    kv = lambda f: pl.BlockSpec((1, blk, LANES), f)
    prev = lambda i, j: (i, jnp.maximum(j - 1, 0), 0)
    cur = lambda i, j: (i, j, 0)
    nxt = lambda i, j: (i, jnp.minimum(j + 1, nb - 1), 0)
    return pl.pallas_call(
        functools.partial(_swa_kernel, blk=blk),
        grid=(b, nb),
        in_specs=[pl.BlockSpec((1, blk, SWA_HEADS * HEAD_DIM), cur), kv(prev), kv(cur), kv(nxt),
                  kv(prev), kv(cur), kv(nxt), pl.BlockSpec(sink_lanes.shape, lambda i, j: (0, 0))],
        out_specs=pl.BlockSpec((1, blk, SWA_HEADS * HEAD_DIM), cur),
        out_shape=jax.ShapeDtypeStruct((b, SEQ, SWA_HEADS * HEAD_DIM), BF16),
        compiler_params=_cparams("parallel", "parallel"),
        name="swa_attn",
    )(q, k, k, k, v, v, v, sink_lanes)


NA_SPAN = NA_KH * GRID_W
NA_RPB_ROWS = 2 * NA_KH - 1
NA_RPB_COLS = 2 * NA_KW - 1


def _na_bias_kernel(rpb_ref, o_ref):
    h = pl.program_id(0)
    delta = pl.program_id(1)
    c = lax.broadcasted_iota(jnp.int32, (GRID_W, NA_SPAN), 0)
    col = lax.broadcasted_iota(jnp.int32, (GRID_W, NA_SPAN), 1)
    j = col >> 6
    cp = col & (GRID_W - 1)
    c0 = jnp.clip(c - NA_KW // 2, 0, GRID_W - NA_KW)
    d = cp - c + (NA_KW - 1)
    acc = jnp.full((GRID_W, NA_SPAN), NEG_INF, F32)
    for jj in range(NA_KH):
        base = (h * NA_RPB_ROWS + (jj - delta + NA_KH - 1)) * NA_RPB_COLS
        for kk in range(NA_RPB_COLS):
            acc = jnp.where((j == jj) & (d == kk), rpb_ref[base + kk], acc)
    valid = (cp >= c0) & (cp < c0 + NA_KW)
    o_ref[0, 0] = jnp.where(valid, acc, NEG_INF)


def _na_bias(rpb):
    return pl.pallas_call(
        _na_bias_kernel,
        grid=(NA_HEADS, NA_KH),
        in_specs=[pl.BlockSpec(memory_space=pltpu.SMEM)],
        out_specs=pl.BlockSpec((1, 1, GRID_W, NA_SPAN), lambda h, d: (h, d, 0, 0)),
        out_shape=jax.ShapeDtypeStruct((NA_HEADS, NA_KH, GRID_W, NA_SPAN), F32),
        compiler_params=_cparams("parallel", "parallel"),
        name="na_bias",
    )(rpb.reshape(-1))


def _na_kernel(q_ref, k_ref, v_ref, bias_ref, o_ref):
    r = pl.program_id(1)
    rows = SEQ // GRID_W
    r0 = jnp.clip(r - NA_KH // 2, 0, rows - NA_KH)
    delta = r - r0
    start = pl.multiple_of(r0 * GRID_W, GRID_W)
    scale = HEAD_DIM ** -0.5
    outs = []
    for h in range(NA_HEADS):
        hs = slice(h * HEAD_DIM, (h + 1) * HEAD_DIM)
        k = k_ref[0, pl.ds(start, NA_SPAN), hs]
        v = v_ref[0, pl.ds(start, NA_SPAN), hs]
        s = _dot_nt(q_ref[0, :, hs], k) * scale + bias_ref[h, delta]
        m = jnp.max(s, axis=-1, keepdims=True)
        p = jnp.exp(s - m)
        l = jnp.sum(p, axis=-1, keepdims=True)
        outs.append(_dot(p.astype(BF16), v) / l)
    o_ref[0] = jnp.concatenate(outs, axis=1).astype(o_ref.dtype)


def _na_attn(q, k, v, bias):
    b = q.shape[0]
    n = NA_HEADS * HEAD_DIM
    return pl.pallas_call(
        _na_kernel,
        grid=(b, SEQ // GRID_W),
        in_specs=[pl.BlockSpec((1, GRID_W, n), lambda i, r: (i, r, 0)),
                  pl.BlockSpec((1, SEQ, n), lambda i, r: (i, 0, 0)),
                  pl.BlockSpec((1, SEQ, n), lambda i, r: (i, 0, 0)),
                  pl.BlockSpec(bias.shape, lambda i, r: (0, 0, 0, 0))],
        out_specs=pl.BlockSpec((1, GRID_W, n), lambda i, r: (i, r, 0)),
        out_shape=jax.ShapeDtypeStruct((b, SEQ, n), BF16),
        compiler_params=_cparams("parallel", "parallel"),
        name="na_attn",
    )(q, k, v, bias)


def _post_kernel(oa_ref, ob_ref, x_ref, w1_ref, w2_ref, g_ref, wr_ref, xo_ref, h_ref, aff_ref):
    x = x_ref[...] + _dot(oa_ref[...], w1_ref[...]) + _dot(ob_ref[...], w2_ref[...])
    xo_ref[...] = x
    h = _rms(x, g_ref[...])
    h_ref[...] = h
    logits = _dot_nt(wr_ref[...], h.astype(BF16))
    e = jnp.exp(logits - jnp.max(logits, axis=0, keepdims=True))
    aff_ref[...] = e / jnp.sum(e, axis=0, keepdims=True)


def _post(oa, ob, x, w_out, g, w_router, tm=512):
    t = x.shape[0]
    half = oa.shape[1]
    w1, w2 = w_out[:half].astype(BF16), w_out[half:].astype(BF16)
    wr = w_router.T.astype(BF16)
    full = lambda a: pl.BlockSpec(a.shape, lambda i: (0,) * a.ndim)
    row = lambda n: pl.BlockSpec((tm, n), lambda i: (i, 0))
    return pl.pallas_call(
        _post_kernel,
        grid=(t // tm,),
        in_specs=[row(half), row(half), row(D_MODEL), full(w1), full(w2), full(g), full(wr)],
        out_specs=[row(D_MODEL), row(D_MODEL), pl.BlockSpec((N_EXPERTS, tm), lambda i: (0, i))],
        out_shape=[jax.ShapeDtypeStruct((t, D_MODEL), F32), jax.ShapeDtypeStruct((t, D_MODEL), F32),
                   jax.ShapeDtypeStruct((N_EXPERTS, t), F32)],
        compiler_params=_cparams("parallel"),
        name="post_attn",
    )(oa, ob, x, w1, w2, g, wr)


def _select_kernel(aff_ref, idx_ref, gate_ref, incl_s, a_hi, a_mid, a_lo, start_s, end_s, cnt_s, *, cap, tc):
    s_id = pl.program_id(1)
    rows = aff_ref.shape[1]

    @pl.when(s_id == 0)
    def _():
        aff = aff_ref[0]
        bits = pltpu.bitcast(aff, jnp.int32)

        def search(i, prefix):
            cand = prefix | jnp.left_shift(jnp.int32(1), 30 - i)
            n = jnp.sum((bits >= cand).astype(F32), keepdims=True)
            return jnp.where(n >= cap, cand, prefix)

        thr = lax.fori_loop(0, 31, search, jnp.zeros((1, 1), jnp.int32))
        gt = bits > thr
        eq = bits == thr
        need = cap - jnp.sum(gt.astype(F32), keepdims=True)
        li = lax.broadcasted_iota(jnp.int32, (LANES, LANES), 0)
        lj = lax.broadcasted_iota(jnp.int32, (LANES, LANES), 1)
        tri = (li <= lj).astype(BF16)
        ones = jnp.ones((LANES, LANES), BF16)
        ri = lax.broadcasted_iota(jnp.int32, (rows, rows), 0)
        rj = lax.broadcasted_iota(jnp.int32, (rows, rows), 1)
        below = (rj < ri).astype(BF16)
        above = (ri < rj).astype(BF16)

        eqb = eq.astype(BF16)
        eq_rank = _dot(below, _dot(eqb, ones).astype(BF16)) + _dot(eqb, tri) - eqb.astype(F32)
        sel = (gt | (eq & (eq_rank < need))).astype(BF16)
        incl_s[...] = _dot(sel, tri).astype(BF16)
        cnt = _dot_nt(jnp.ones((8, LANES), BF16), sel)
        start = _dot(cnt.astype(BF16), above)
        cnt_s[...] = cnt
        start_s[...] = start
        end_s[...] = start + cnt
        hi = aff.astype(BF16)
        r1 = aff - hi.astype(F32)
        mid = r1.astype(BF16)
        a_hi[...] = hi
        a_mid[...] = mid
        a_lo[...] = (r1 - mid.astype(F32)).astype(BF16)

    c = (s_id * tc + lax.broadcasted_iota(jnp.int32, (tc, rows), 0)).astype(F32)
    start, end, cnt = start_s[0:1, :], end_s[0:1, :], cnt_s[0:1, :]
    before = c >= end
    row_id = jnp.sum(before.astype(F32), axis=1, keepdims=True)
    row_start = jnp.sum(jnp.where(before, cnt, 0.0), axis=1, keepdims=True)
    onehot = ((c >= start) & (c < end)).astype(BF16)
    within = _dot(onehot, incl_s[...])
    k = c[:, 0:1] - row_start
    lane = jnp.sum((within <= k).astype(F32), axis=1, keepdims=True)
    idx_ref[0] = (row_id * LANES + lane).astype(jnp.int32)
    vals = _dot(onehot, a_hi[...]) + _dot(onehot, a_mid[...]) + _dot(onehot, a_lo[...])
    lane_iota = lax.broadcasted_iota(jnp.int32, (tc, LANES), 1).astype(F32)
    gate = jnp.sum(jnp.where(lane_iota == lane, vals, 0.0), axis=1, keepdims=True)
    gate_ref[0] = jnp.broadcast_to(gate, (tc, LANES))


def _select(aff_t, cap, tc=256):
    t = aff_t.shape[1]
    rows = t // LANES
    aff3 = aff_t.reshape(N_EXPERTS, rows, LANES)
    idx, gate = pl.pallas_call(
        functools.partial(_select_kernel, cap=cap, tc=tc),
        grid=(N_EXPERTS, cap // tc),
        in_specs=[pl.BlockSpec((1, rows, LANES), lambda e, s: (e, 0, 0))],
        out_specs=[pl.BlockSpec((1, tc, 1), lambda e, s: (e, s, 0)),
                   pl.BlockSpec((1, tc, LANES), lambda e, s: (e, s, 0))],
        out_shape=[jax.ShapeDtypeStruct((N_EXPERTS, cap, 1), jnp.int32),
                   jax.ShapeDtypeStruct((N_EXPERTS, cap, LANES), F32)],
        scratch_shapes=[pltpu.VMEM((rows, LANES), BF16)] * 4 + [pltpu.VMEM((8, rows), F32)] * 3,
        compiler_params=_cparams("parallel", "arbitrary"),
        name="ec_select",
    )(aff3)
    return idx.reshape(N_EXPERTS * cap), gate.reshape(N_EXPERTS * cap, LANES)


def _ffn_kernel(idx_ref, gate_ref, wg_ref, wu_ref, wd_ref, h_hbm, x_hbm, xo_hbm, hbuf, xbuf, sem_h, sem_x, sem_o,
                *, tc, n_tiles):
    del x_hbm
    base = (pl.program_id(0) * n_tiles + pl.program_id(1)) * tc

    def gather(i, carry):
        t = idx_ref[base + i]
        pltpu.make_async_copy(h_hbm.at[pl.ds(t, 1)], hbuf.at[pl.ds(i, 1)], sem_h).start()
        pltpu.make_async_copy(xo_hbm.at[pl.ds(t, 1)], xbuf.at[pl.ds(i, 1)], sem_x).start()
        return carry

    lax.fori_loop(0, tc, gather, 0)

    def gather_wait(i, carry):
        pltpu.make_async_copy(h_hbm.at[pl.ds(0, 1)], hbuf.at[pl.ds(i, 1)], sem_h).wait()
        pltpu.make_async_copy(xo_hbm.at[pl.ds(0, 1)], xbuf.at[pl.ds(i, 1)], sem_x).wait()
        return carry

    lax.fori_loop(0, tc, gather_wait, 0)

    h = hbuf[...].astype(BF16)
    g = _dot(h, wg_ref[0])
    u = _dot(h, wu_ref[0])
    act = (g * jax.nn.sigmoid(g) * u).astype(BF16)
    y = _dot(act, wd_ref[0]) * gate_ref[:, 0:1]
    xbuf[...] = xbuf[...] + y

    def scatter(i, carry):
        t = idx_ref[base + i]
        pltpu.make_async_copy(xbuf.at[pl.ds(i, 1)], xo_hbm.at[pl.ds(t, 1)], sem_o).start()
        return carry

    lax.fori_loop(0, tc, scatter, 0)

    def scatter_wait(i, carry):
        pltpu.make_async_copy(xbuf.at[pl.ds(i, 1)], xo_hbm.at[pl.ds(0, 1)], sem_o).wait()
        return carry

    lax.fori_loop(0, tc, scatter_wait, 0)


def _ffn(idx, gate, h, x, w_gate, w_up, w_down, cap, tc=256):
    n_tiles = cap // tc
    t = x.shape[0]
    wspec = pl.BlockSpec((1, D_MODEL, D_MODEL), lambda e, s, idx_ref: (e, 0, 0))
    any_spec = pl.BlockSpec(memory_space=pl.ANY)
    grid_spec = pltpu.PrefetchScalarGridSpec(
        num_scalar_prefetch=1,
        grid=(N_EXPERTS, n_tiles),
        in_specs=[pl.BlockSpec((tc, LANES), lambda e, s, idx_ref: (e * n_tiles + s, 0)),
                  wspec, wspec, wspec, any_spec, any_spec],
        out_specs=any_spec,
        scratch_shapes=[pltpu.VMEM((tc, D_MODEL), F32), pltpu.VMEM((tc, D_MODEL), F32),
                        pltpu.SemaphoreType.DMA, pltpu.SemaphoreType.DMA, pltpu.SemaphoreType.DMA],
    )
    return pl.pallas_call(
        functools.partial(_ffn_kernel, tc=tc, n_tiles=n_tiles),
        grid_spec=grid_spec,
        out_shape=jax.ShapeDtypeStruct((t, D_MODEL), F32),
        input_output_aliases={6: 0},
        compiler_params=_cparams("arbitrary", "arbitrary"),
        name="ec_ffn",
    )(idx, gate, w_gate, w_up, w_down, h, x)


def _final_kernel(x_ref, g_ref, o_ref):
    o_ref[...] = _rms(x_ref[...], g_ref[...])


def _final_norm(x, g, tm=1024):
    t = x.shape[0]
    return pl.pallas_call(
        _final_kernel,
        grid=(t // tm,),
        in_specs=[pl.BlockSpec((tm, D_MODEL), lambda i: (i, 0)), pl.BlockSpec(g.shape, lambda i: (0, 0))],
        out_specs=pl.BlockSpec((tm, D_MODEL), lambda i: (i, 0)),
        out_shape=jax.ShapeDtypeStruct((t, D_MODEL), F32),
        compiler_params=_cparams("parallel"),
        name="final_norm",
    )(x, g)


def _moe(x, h, aff_t, w_gate, w_up, w_down):
    t = x.shape[0]
    cap = EC_CAPACITY_FACTOR * t // N_EXPERTS
    idx, gate = _select(aff_t, cap)
    return _ffn(idx, gate, h, x, w_gate, w_up, w_down, cap)


def _trunk(x3, p):
    b = x3.shape[0]
    t = b * SEQ
    x = x3.reshape(t, D_MODEL)
    seq = lambda a: a.reshape(b, SEQ, a.shape[-1])
    flat = lambda a: a.reshape(t, a.shape[-1])

    qm, km, vm, qs, ks, vs = _pre_ab(x, p["g_mix0"], p["w_big_ab"], p["q_norm"], p["w_q"], p["kv_norm"], p["w_kv"],
                                     p["tabs"])
    o_a = _dense_attn(seq(qm), seq(km), seq(vm), None, n_heads=MLA_HEADS, dq=LANES, dv=MLA_V,
                      scale=(MLA_NOPE + MLA_ROPE) ** -0.5)
    o_b = _swa_attn(seq(qs), seq(ks), seq(vs), p["sink"])
    x, h, aff_t = _post(flat(o_a), flat(o_b), x, p["w_out_ab"], p["g_ffn0"], p["w_router0"])
    x = _moe(x, h, aff_t, p["w_gate0"], p["w_up0"], p["w_down0"])

    qc, kc, vc, qd, kd, vd = _pre_cd(x, p["g_mix1"], p["w_big_cd"], p["tabs"])
    o_c = _dense_attn(seq(qc), seq(kc), seq(vc), p["counts"], n_heads=DIL_HEADS, dq=HEAD_DIM, dv=HEAD_DIM,
                      scale=HEAD_DIM ** -0.5)
    o_d = _na_attn(seq(qd), seq(kd), seq(vd), p["na_bias"])
    x, h, aff_t = _post(flat(o_c), flat(o_d), x, p["w_out_cd"], p["g_ffn1"], p["w_router1"])
    x = _moe(x, h, aff_t, p["w_gate1"], p["w_up1"], p["w_down1"])
    return _final_norm(x, p["g_final"]).reshape(b, SEQ, D_MODEL)


def kernel(x_prompt, x_sample, norm_mix, norm_ffn, norm_final, w_in_ab, mla_q_norm, mla_w_uq, mla_kv_norm,
           mla_w_ukv, swa_sink, w_out_ab, w_in_cd, na_rpb, w_out_cd, w_router, w_gate, w_up, w_down):
    w_big_ab, w_q, w_kv = _prep_ab(w_in_ab[0], mla_w_uq[0], mla_w_ukv[0])
    p = {
        "tabs": _rope_tables(),
        "g_mix0": norm_mix[0:1], "g_mix1": norm_mix[1:2], "g_ffn0": norm_ffn[0:1], "g_ffn1": norm_ffn[1:2],
        "g_final": norm_final.reshape(1, D_MODEL),
        "w_big_ab": w_big_ab, "w_q": w_q, "w_kv": w_kv,
        "q_norm": mla_q_norm[0:1], "kv_norm": mla_kv_norm[0:1],
        "sink": jnp.repeat(swa_sink[0], HEAD_DIM).reshape(1, SWA_HEADS * HEAD_DIM),
        "w_out_ab": w_out_ab[0], "w_big_cd": _prep_cd(w_in_cd[0]), "w_out_cd": w_out_cd[0],
        "counts": _dilation_counts(), "na_bias": _na_bias(na_rpb[0]),
        "w_router0": w_router[0], "w_router1": w_router[1],
    }
    for l in range(2):
        p[f"w_gate{l}"] = w_gate[l].astype(BF16)
        p[f"w_up{l}"] = w_up[l].astype(BF16)
        p[f"w_down{l}"] = w_down[l].astype(BF16)
    return _trunk(x_prompt, p), _trunk(x_sample, p)
```

```python
import functools

import jax
import jax.numpy as jnp
import numpy as np
from jax import lax
from jax.experimental import pallas as pl
from jax.experimental.pallas import tpu as pltpu

D_MODEL = 1024
SEQ = 2048
HEAD_DIM = 64
ROPE_THETA = 10000.0
NEG_INF = -1e30
RMS_EPS = 1e-6

MLA_HEADS = 8
MLA_Q_RANK = 384
MLA_KV_RANK = 256
MLA_NOPE = 64
MLA_ROPE = 32
MLA_V = 64

SWA_HEADS = 8
SWA_KV_HEADS = 2
SWA_GROUP = SWA_HEADS // SWA_KV_HEADS
SWA_HALF_WINDOW = 128

DIL_HEADS = 8
DIL_BRANCHES = ((128, 1), (512, 4), (2048, 16))

NA_HEADS = 8
GRID_W = 64
NA_KH = 8
NA_KW = 16

N_EXPERTS = 16
EC_CAPACITY_FACTOR = 2

LANES = 128
VMEM_LIMIT = 56 * 1024 * 1024

BF16 = jnp.bfloat16
F32 = jnp.float32


def _cparams(*sem):
    return pltpu.CompilerParams(dimension_semantics=sem, vmem_limit_bytes=VMEM_LIMIT)


def _dot(a, b):
    return jnp.dot(a, b, preferred_element_type=F32)


def _dot_nt(a, b):
    return lax.dot_general(a, b, (((1,), (1,)), ((), ())), preferred_element_type=F32)


def _rms(x, g):
    return x * lax.rsqrt(jnp.mean(x * x, axis=-1, keepdims=True) + RMS_EPS) * g


def _rot_cols(w, n_heads, dim):
    k = w.shape[0]
    w = w.reshape(k, n_heads, dim)
    return jnp.concatenate([-w[..., dim // 2:], w[..., :dim // 2]], axis=-1).reshape(k, n_heads * dim)


def _rope_tables():
    def tables(dim):
        inv = 1.0 / (ROPE_THETA ** (jnp.arange(0, dim, 2, dtype=F32) / dim))
        ang = jnp.arange(SEQ, dtype=F32)[:, None] * inv[None, :]
        c, s = jnp.cos(ang), jnp.sin(ang)
        return jnp.concatenate([c, c], axis=-1), jnp.concatenate([s, s], axis=-1)

    c32, s32 = tables(MLA_ROPE)
    one = jnp.ones((SEQ, MLA_NOPE), F32)
    zero = jnp.zeros((SEQ, MLA_NOPE), F32)
    z32 = jnp.zeros((SEQ, LANES - MLA_NOPE - MLA_ROPE), F32)
    cos_a = jnp.concatenate([one, c32, z32], axis=-1)
    sin_a = jnp.concatenate([zero, s32, z32], axis=-1)
    c64, s64 = tables(HEAD_DIM)
    cos_b = jnp.concatenate([c64, c64], axis=-1)
    sin_b = jnp.concatenate([s64, s64], axis=-1)
    return cos_a, sin_a, cos_b, sin_b


def _pad_heads(w, n_heads, dim):
    k = w.shape[0]
    w = w.reshape(k, n_heads, dim)
    return jnp.pad(w, ((0, 0), (0, 0), (0, LANES - dim))).reshape(k, n_heads * LANES)


def _prep_ab(w_in, w_uq, w_ukv):
    cq = w_in[:, :MLA_Q_RANK]
    o = MLA_Q_RANK
    ckv = w_in[:, o:o + MLA_KV_RANK]
    o += MLA_KV_RANK
    kr = w_in[:, o:o + MLA_ROPE]
    o += MLA_ROPE
    qb = w_in[:, o:o + SWA_HEADS * HEAD_DIM]
    o += SWA_HEADS * HEAD_DIM
    kb = w_in[:, o:o + SWA_KV_HEADS * HEAD_DIM]
    o += SWA_KV_HEADS * HEAD_DIM
    vb = w_in[:, o:]
    kr_rot = _rot_cols(kr, 1, MLA_ROPE)
    lead = jnp.zeros((D_MODEL, MLA_NOPE), F32)
    tail = jnp.zeros((D_MODEL, LANES - MLA_NOPE - MLA_ROPE), F32)
    kr_pad = jnp.concatenate([lead, kr, tail], axis=1)
    kr_rot_pad = jnp.concatenate([lead, kr_rot, tail], axis=1)
    w_big = jnp.concatenate([cq, ckv, qb, _rot_cols(qb, SWA_HEADS, HEAD_DIM), kb,
                             _rot_cols(kb, SWA_KV_HEADS, HEAD_DIM), vb, kr_pad, kr_rot_pad], axis=1)
    dq = MLA_NOPE + MLA_ROPE
    wq = w_uq.reshape(MLA_Q_RANK, MLA_HEADS, dq)
    rope = wq[..., MLA_NOPE:]
    rope_rot = jnp.concatenate([-rope[..., MLA_ROPE // 2:], rope[..., :MLA_ROPE // 2]], axis=-1)
    wq_rot = jnp.concatenate([jnp.zeros_like(wq[..., :MLA_NOPE]), rope_rot], axis=-1)
    wq_pad = _pad_heads(wq.reshape(MLA_Q_RANK, -1), MLA_HEADS, dq)
    wq_rot_pad = _pad_heads(wq_rot.reshape(MLA_Q_RANK, -1), MLA_HEADS, dq)
    w_q = jnp.concatenate([wq_pad, wq_rot_pad], axis=1)
    wkv = w_ukv.reshape(MLA_KV_RANK, MLA_HEADS, MLA_NOPE + MLA_V)
    wk_pad = _pad_heads(wkv[..., :MLA_NOPE].reshape(MLA_KV_RANK, -1), MLA_HEADS, MLA_NOPE)
    wv = wkv[..., MLA_NOPE:].reshape(MLA_KV_RANK, MLA_HEADS * MLA_V)
    w_kv = jnp.concatenate([wk_pad, wv], axis=1)
    return w_big.astype(BF16), w_q.astype(BF16), w_kv.astype(BF16)


def _prep_cd(w_in):
    n = DIL_HEADS * HEAD_DIM
    q_c, k_c, v_c, q_d, k_d, v_d = [w_in[:, i * n:(i + 1) * n] for i in range(6)]
    w_big = jnp.concatenate([q_c, _rot_cols(q_c, DIL_HEADS, HEAD_DIM), k_c, _rot_cols(k_c, DIL_HEADS, HEAD_DIM),
                             v_c, q_d, k_d, v_d], axis=1)
    return w_big.astype(BF16)


def _dilation_counts():
    d = np.arange(SEQ)[:, None] - np.arange(SEQ)[None, :]
    c = np.zeros((SEQ, SEQ), np.float32)
    for window, dil in DIL_BRANCHES:
        c += ((d % dil == 0) & (np.abs(d) <= window // 2)).astype(np.float32)
    return jnp.asarray(c, dtype=BF16)


def _pre_ab_kernel(x_ref, g_ref, wbig_ref, qn_ref, wq_ref, kvn_ref, wkv_ref, ca_ref, sa_ref, cb_ref, sb_ref,
                   qm_ref, km_ref, vm_ref, qs_ref, ks_ref, vs_ref):
    h = _rms(x_ref[...], g_ref[...]).astype(BF16)
    proj = _dot(h, wbig_ref[...])
    ca, sa, cb, sb = ca_ref[...], sa_ref[...], cb_ref[...], sb_ref[...]
    o = 0
    c_q = proj[:, o:o + MLA_Q_RANK]
    o += MLA_Q_RANK
    c_kv = proj[:, o:o + MLA_KV_RANK]
    o += MLA_KV_RANK
    nq = SWA_HEADS * HEAD_DIM
    q_b, q_b_rot = proj[:, o:o + nq], proj[:, o + nq:o + 2 * nq]
    o += 2 * nq
    k_b, k_b_rot, v_b = proj[:, o:o + LANES], proj[:, o + LANES:o + 2 * LANES], proj[:, o + 2 * LANES:o + 3 * LANES]
    o += 3 * LANES
    k_r, k_r_rot = proj[:, o:o + LANES], proj[:, o + LANES:o + 2 * LANES]

    qs_ref[...] = jnp.concatenate(
        [q_b[:, i * LANES:(i + 1) * LANES] * cb + q_b_rot[:, i * LANES:(i + 1) * LANES] * sb
         for i in range(nq // LANES)], axis=1).astype(BF16)
    ks_ref[...] = (k_b * cb + k_b_rot * sb).astype(BF16)
    vs_ref[...] = v_b.astype(BF16)

    q_all = _dot(_rms(c_q, qn_ref[...]).astype(BF16), wq_ref[...])
    nh = MLA_HEADS * LANES
    qm_ref[...] = jnp.concatenate(
        [q_all[:, i * LANES:(i + 1) * LANES] * ca + q_all[:, nh + i * LANES:nh + (i + 1) * LANES] * sa
         for i in range(MLA_HEADS)], axis=1).astype(BF16)
    kv = _dot(_rms(c_kv, kvn_ref[...]).astype(BF16), wkv_ref[...])
    k_rope = k_r * ca + k_r_rot * sa
    km_ref[...] = jnp.concatenate(
        [kv[:, i * LANES:(i + 1) * LANES] + k_rope for i in range(MLA_HEADS)], axis=1).astype(BF16)
    vm_ref[...] = kv[:, nh:].astype(BF16)


def _pre_ab(x, g, w_big, qn, w_q, kvn, w_kv, tabs, tm=512):
    t = x.shape[0]
    nsb = SEQ // tm
    full = lambda a: pl.BlockSpec(a.shape, lambda i: (0,) * a.ndim)
    tab = pl.BlockSpec((tm, LANES), lambda i: (i % nsb, 0))
    row = lambda n: pl.BlockSpec((tm, n), lambda i: (i, 0))
    outs = [(MLA_HEADS * LANES, BF16), (MLA_HEADS * LANES, BF16), (MLA_HEADS * MLA_V, BF16),
            (SWA_HEADS * HEAD_DIM, BF16), (LANES, BF16), (LANES, BF16)]
    return pl.pallas_call(
        _pre_ab_kernel,
        grid=(t // tm,),
        in_specs=[row(D_MODEL), full(g), full(w_big), full(qn), full(w_q), full(kvn), full(w_kv), tab, tab, tab, tab],
        out_specs=[row(n) for n, _ in outs],
        out_shape=[jax.ShapeDtypeStruct((t, n), d) for n, d in outs],
        compiler_params=_cparams("parallel"),
        name="pre_ab",
    )(x, g, w_big, qn, w_q, kvn, w_kv, *tabs)


def _pre_cd_kernel(x_ref, g_ref, wbig_ref, cb_ref, sb_ref, qc_ref, kc_ref, vc_ref, qd_ref, kd_ref, vd_ref):
    h = _rms(x_ref[...], g_ref[...]).astype(BF16)
    cb, sb = cb_ref[...], sb_ref[...]
    n = DIL_HEADS * HEAD_DIM
    w = wbig_ref

    def roped(o):
        a = _dot(h, w[:, o:o + n])
        b = _dot(h, w[:, o + n:o + 2 * n])
        return jnp.concatenate([a[:, i * LANES:(i + 1) * LANES] * cb + b[:, i * LANES:(i + 1) * LANES] * sb
                                for i in range(n // LANES)], axis=1).astype(BF16)

    qc_ref[...] = roped(0)
    kc_ref[...] = roped(2 * n)
    vc_ref[...] = _dot(h, w[:, 4 * n:5 * n]).astype(BF16)
    qd_ref[...] = _dot(h, w[:, 5 * n:6 * n]).astype(BF16)
    kd_ref[...] = _dot(h, w[:, 6 * n:7 * n]).astype(BF16)
    vd_ref[...] = _dot(h, w[:, 7 * n:8 * n]).astype(BF16)


def _pre_cd(x, g, w_big, tabs, tm=512):
    t = x.shape[0]
    nsb = SEQ // tm
    n = DIL_HEADS * HEAD_DIM
    full = lambda a: pl.BlockSpec(a.shape, lambda i: (0,) * a.ndim)
    tab = pl.BlockSpec((tm, LANES), lambda i: (i % nsb, 0))
    row = lambda m: pl.BlockSpec((tm, m), lambda i: (i, 0))
    return pl.pallas_call(
        _pre_cd_kernel,
        grid=(t // tm,),
        in_specs=[row(D_MODEL), full(g), full(w_big), tab, tab],
        out_specs=[row(n)] * 6,
        out_shape=[jax.ShapeDtypeStruct((t, n), BF16)] * 6,
        compiler_params=_cparams("parallel"),
        name="pre_cd",
    )(x, g, w_big, tabs[2], tabs[3])


def _dense_attn_kernel(*refs, n_heads, dq, dv, scale, weighted):
    if weighted:
        q_ref, k_ref, v_ref, c_ref, o_ref = refs
        cnt = c_ref[...].astype(F32)
        visible = cnt > 0.0
    else:
        q_ref, k_ref, v_ref, o_ref = refs
    outs = []
    for h in range(n_heads):
        q = q_ref[0, :, h * dq:(h + 1) * dq]
        k = k_ref[0, :, h * dq:(h + 1) * dq]
        v = v_ref[0, :, h * dv:(h + 1) * dv]
        s = _dot_nt(q, k) * scale
        if weighted:
            s = jnp.where(visible, s, NEG_INF)
        m = jnp.max(s, axis=-1, keepdims=True)
        p = jnp.exp(s - m)
        if weighted:
            p = p * cnt
        l = jnp.sum(p, axis=-1, keepdims=True)
        outs.append(_dot(p.astype(BF16), v) / l)
    o_ref[0] = jnp.concatenate(outs, axis=1).astype(o_ref.dtype)


def _dense_attn(q, k, v, counts, *, n_heads, dq, dv, scale, tq=256):
    b = q.shape[0]
    weighted = counts is not None
    in_specs = [pl.BlockSpec((1, tq, n_heads * dq), lambda i, j: (i, j, 0)),
                pl.BlockSpec((1, SEQ, n_heads * dq), lambda i, j: (i, 0, 0)),
                pl.BlockSpec((1, SEQ, n_heads * dv), lambda i, j: (i, 0, 0))]
    args = [q, k, v]
    if weighted:
        in_specs.append(pl.BlockSpec((tq, SEQ), lambda i, j: (j, 0)))
        args.append(counts)
    return pl.pallas_call(
        functools.partial(_dense_attn_kernel, n_heads=n_heads, dq=dq, dv=dv, scale=scale, weighted=weighted),
        grid=(b, SEQ // tq),
        in_specs=in_specs,
        out_specs=pl.BlockSpec((1, tq, n_heads * dv), lambda i, j: (i, j, 0)),
        out_shape=jax.ShapeDtypeStruct((b, SEQ, n_heads * dv), BF16),
        compiler_params=_cparams("parallel", "parallel"),
        name="dilated_attn" if weighted else "mla_attn",
    )(*args)


def _swa_kernel(q_ref, kp_ref, kc_ref, kn_ref, vp_ref, vc_ref, vn_ref, sink_ref, o_ref, *, blk):
    i = pl.program_id(1)
    qpos = i * blk + lax.broadcasted_iota(jnp.int32, (blk, 3 * blk), 0)
    kpos = (i - 1) * blk + lax.broadcasted_iota(jnp.int32, (blk, 3 * blk), 1)
    valid = (jnp.abs(qpos - kpos) <= SWA_HALF_WINDOW) & (kpos >= 0) & (kpos < SEQ)
    scale = HEAD_DIM ** -0.5
    outs = []
    for g in range(SWA_KV_HEADS):
        sl = slice(g * HEAD_DIM, (g + 1) * HEAD_DIM)
        k = jnp.concatenate([kp_ref[0, :, sl], kc_ref[0, :, sl], kn_ref[0, :, sl]], axis=0)
        v = jnp.concatenate([vp_ref[0, :, sl], vc_ref[0, :, sl], vn_ref[0, :, sl]], axis=0)
        for r in range(SWA_GROUP):
            hs = slice((g * SWA_GROUP + r) * HEAD_DIM, (g * SWA_GROUP + r + 1) * HEAD_DIM)
            s = jnp.where(valid, _dot_nt(q_ref[0, :, hs], k) * scale, NEG_INF)
            m = jnp.max(s, axis=-1, keepdims=True)
            p = jnp.exp(s - m)
            l = jnp.sum(p, axis=-1, keepdims=True)
            o = _dot(p.astype(BF16), v) / l
            lse = m + jnp.log(l)
            keep = jax.nn.sigmoid(lse - sink_ref[:, hs])
            outs.append(o * keep)
    o_ref[0] = jnp.concatenate(outs, axis=1).astype(o_ref.dtype)


def _swa_attn(q, k, v, sink_lanes, blk=128):
    b = q.shape[0]
    nb = SEQ // blk
    kv = lambda f: pl.BlockSpec((1, blk, LANES), f)
    prev = lambda i, j: (i, jnp.maximum(j - 1, 0), 0)
    cur = lambda i, j: (i, j, 0)
    nxt = lambda i, j: (i, jnp.minimum(j + 1, nb - 1), 0)
    return pl.pallas_call(
        functools.partial(_swa_kernel, blk=blk),
        grid=(b, nb),
        in_specs=[pl.BlockSpec((1, blk, SWA_HEADS * HEAD_DIM), cur), kv(prev), kv(cur), kv(nxt),
                  kv(prev), kv(cur), kv(nxt), pl.BlockSpec(sink_lanes.shape, lambda i, j: (0, 0))],
        out_specs=pl.BlockSpec((1, blk, SWA_HEADS * HEAD_DIM), cur),
        out_shape=jax.ShapeDtypeStruct((b, SEQ, SWA_HEADS * HEAD_DIM), BF16),
        compiler_params=_cparams("parallel", "parallel"),
        name="swa_attn",
    )(q, k, k, k, v, v, v, sink_lanes)


NA_SPAN = NA_KH * GRID_W
NA_RPB_ROWS = 2 * NA_KH - 1
NA_RPB_COLS = 2 * NA_KW - 1


def _na_bias_kernel(rpb_ref, o_ref):
    h = pl.program_id(0)
    delta = pl.program_id(1)
    c = lax.broadcasted_iota(jnp.int32, (GRID_W, NA_SPAN), 0)
    col = lax.broadcasted_iota(jnp.int32, (GRID_W, NA_SPAN), 1)
    j = col >> 6
    cp = col & (GRID_W - 1)
    c0 = jnp.clip(c - NA_KW // 2, 0, GRID_W - NA_KW)
    d = cp - c + (NA_KW - 1)
    acc = jnp.full((GRID_W, NA_SPAN), NEG_INF, F32)
    for jj in range(NA_KH):
        base = (h * NA_RPB_ROWS + (jj - delta + NA_KH - 1)) * NA_RPB_COLS
        for kk in range(NA_RPB_COLS):
            acc = jnp.where((j == jj) & (d == kk), rpb_ref[base + kk], acc)
    valid = (cp >= c0) & (cp < c0 + NA_KW)
    o_ref[0, 0] = jnp.where(valid, acc, NEG_INF)


def _na_bias(rpb):
    return pl.pallas_call(
        _na_bias_kernel,
        grid=(NA_HEADS, NA_KH),
        in_specs=[pl.BlockSpec(memory_space=pltpu.SMEM)],
        out_specs=pl.BlockSpec((1, 1, GRID_W, NA_SPAN), lambda h, d: (h, d, 0, 0)),
        out_shape=jax.ShapeDtypeStruct((NA_HEADS, NA_KH, GRID_W, NA_SPAN), F32),
        compiler_params=_cparams("parallel", "parallel"),
        name="na_bias",
    )(rpb.reshape(-1))


def _na_kernel(q_ref, k_ref, v_ref, bias_ref, o_ref):
    r = pl.program_id(1)
    rows = SEQ // GRID_W
    r0 = jnp.clip(r - NA_KH // 2, 0, rows - NA_KH)
    delta = r - r0
    start = pl.multiple_of(r0 * GRID_W, GRID_W)
    scale = HEAD_DIM ** -0.5
    outs = []
    for h in range(NA_HEADS):
        hs = slice(h * HEAD_DIM, (h + 1) * HEAD_DIM)
        k = k_ref[0, pl.ds(start, NA_SPAN), hs]
        v = v_ref[0, pl.ds(start, NA_SPAN), hs]
        s = _dot_nt(q_ref[0, :, hs], k) * scale + bias_ref[h, delta]
        m = jnp.max(s, axis=-1, keepdims=True)
        p = jnp.exp(s - m)
        l = jnp.sum(p, axis=-1, keepdims=True)
        outs.append(_dot(p.astype(BF16), v) / l)
    o_ref[0] = jnp.concatenate(outs, axis=1).astype(o_ref.dtype)


def _na_attn(q, k, v, bias):
    b = q.shape[0]
    n = NA_HEADS * HEAD_DIM
    return pl.pallas_call(
        _na_kernel,
        grid=(b, SEQ // GRID_W),
        in_specs=[pl.BlockSpec((1, GRID_W, n), lambda i, r: (i, r, 0)),
                  pl.BlockSpec((1, SEQ, n), lambda i, r: (i, 0, 0)),
                  pl.BlockSpec((1, SEQ, n), lambda i, r: (i, 0, 0)),
                  pl.BlockSpec(bias.shape, lambda i, r: (0, 0, 0, 0))],
        out_specs=pl.BlockSpec((1, GRID_W, n), lambda i, r: (i, r, 0)),
        out_shape=jax.ShapeDtypeStruct((b, SEQ, n), BF16),
        compiler_params=_cparams("parallel", "parallel"),
        name="na_attn",
    )(q, k, v, bias)


def _post_kernel(oa_ref, ob_ref, x_ref, w1_ref, w2_ref, g_ref, wr_ref, xh_ref, aff_ref):
    x = x_ref[...] + _dot(oa_ref[...], w1_ref[...]) + _dot(ob_ref[...], w2_ref[...])
    h = _rms(x, g_ref[...])
    xh_ref[:, :D_MODEL] = x
    xh_ref[:, D_MODEL:] = h
    logits = _dot_nt(wr_ref[...], h.astype(BF16))
    e = jnp.exp(logits - jnp.max(logits, axis=0, keepdims=True))
    aff_ref[...] = e / jnp.sum(e, axis=0, keepdims=True)


def _post(oa, ob, x, w_out, g, w_router, tm=512):
    t = x.shape[0]
    half = oa.shape[1]
    w1, w2 = w_out[:half].astype(BF16), w_out[half:].astype(BF16)
    wr = w_router.T.astype(BF16)
    full = lambda a: pl.BlockSpec(a.shape, lambda i: (0,) * a.ndim)
    row = lambda n: pl.BlockSpec((tm, n), lambda i: (i, 0))
    return pl.pallas_call(
        _post_kernel,
        grid=(t // tm,),
        in_specs=[row(half), row(half), row(D_MODEL), full(w1), full(w2), full(g), full(wr)],
        out_specs=[row(2 * D_MODEL), pl.BlockSpec((N_EXPERTS, tm), lambda i: (0, i))],
        out_shape=[jax.ShapeDtypeStruct((t, 2 * D_MODEL), F32), jax.ShapeDtypeStruct((N_EXPERTS, t), F32)],
        compiler_params=_cparams("parallel"),
        name="post_attn",
    )(oa, ob, x, w1, w2, g, wr)


def _select_kernel(aff_ref, idx_ref, gate_ref, incl_s, a_hi, a_mid, a_lo, start_s, end_s, cnt_s, *, cap, tc):
    s_id = pl.program_id(1)
    rows = aff_ref.shape[1]

    @pl.when(s_id == 0)
    def _():
        aff = aff_ref[0]
        bits = pltpu.bitcast(aff, jnp.int32)

        def search(i, prefix):
            cand = prefix | jnp.left_shift(jnp.int32(1), 30 - i)
            n = jnp.sum((bits >= cand).astype(F32), keepdims=True)
            return jnp.where(n >= cap, cand, prefix)

        thr = lax.fori_loop(0, 31, search, jnp.zeros((1, 1), jnp.int32))
        gt = bits > thr
        eq = bits == thr
        need = cap - jnp.sum(gt.astype(F32), keepdims=True)
        li = lax.broadcasted_iota(jnp.int32, (LANES, LANES), 0)
        lj = lax.broadcasted_iota(jnp.int32, (LANES, LANES), 1)
        tri = (li <= lj).astype(BF16)
        ones = jnp.ones((LANES, LANES), BF16)
        ri = lax.broadcasted_iota(jnp.int32, (rows, rows), 0)
        rj = lax.broadcasted_iota(jnp.int32, (rows, rows), 1)
        below = (rj < ri).astype(BF16)
        above = (ri < rj).astype(BF16)

        eqb = eq.astype(BF16)
        eq_rank = _dot(below, _dot(eqb, ones).astype(BF16)) + _dot(eqb, tri) - eqb.astype(F32)
        sel = (gt | (eq & (eq_rank < need))).astype(BF16)
        incl_s[...] = _dot(sel, tri).astype(BF16)
        cnt = _dot_nt(jnp.ones((8, LANES), BF16), sel)
        start = _dot(cnt.astype(BF16), above)
        cnt_s[...] = cnt
        start_s[...] = start
        end_s[...] = start + cnt
        hi = aff.astype(BF16)
        r1 = aff - hi.astype(F32)
        mid = r1.astype(BF16)
        a_hi[...] = hi
        a_mid[...] = mid
        a_lo[...] = (r1 - mid.astype(F32)).astype(BF16)

    c = (s_id * tc + lax.broadcasted_iota(jnp.int32, (tc, rows), 0)).astype(F32)
    start, end, cnt = start_s[0:1, :], end_s[0:1, :], cnt_s[0:1, :]
    before = c >= end
    row_id = jnp.sum(before.astype(F32), axis=1, keepdims=True)
    row_start = jnp.sum(jnp.where(before, cnt, 0.0), axis=1, keepdims=True)
    onehot = ((c >= start) & (c < end)).astype(BF16)
    within = _dot(onehot, incl_s[...])
    k = c[:, 0:1] - row_start
    lane = jnp.sum((within <= k).astype(F32), axis=1, keepdims=True)
    idx_ref[0] = (row_id * LANES + lane).astype(jnp.int32)
    vals = _dot(onehot, a_hi[...]) + _dot(onehot, a_mid[...]) + _dot(onehot, a_lo[...])
    lane_iota = lax.broadcasted_iota(jnp.int32, (tc, LANES), 1).astype(F32)
    gate = jnp.sum(jnp.where(lane_iota == lane, vals, 0.0), axis=1, keepdims=True)
    gate_ref[0] = jnp.broadcast_to(gate, (tc, LANES))


def _select(aff_t, cap, tc=256):
    t = aff_t.shape[1]
    rows = t // LANES
    aff3 = aff_t.reshape(N_EXPERTS, rows, LANES)
    idx, gate = pl.pallas_call(
        functools.partial(_select_kernel, cap=cap, tc=tc),
        grid=(N_EXPERTS, cap // tc),
        in_specs=[pl.BlockSpec((1, rows, LANES), lambda e, s: (e, 0, 0))],
        out_specs=[pl.BlockSpec((1, tc, 1), lambda e, s: (e, s, 0)),
                   pl.BlockSpec((1, tc, LANES), lambda e, s: (e, s, 0))],
        out_shape=[jax.ShapeDtypeStruct((N_EXPERTS, cap, 1), jnp.int32),
                   jax.ShapeDtypeStruct((N_EXPERTS, cap, LANES), F32)],
        scratch_shapes=[pltpu.VMEM((rows, LANES), BF16)] * 4 + [pltpu.VMEM((8, rows), F32)] * 3,
        compiler_params=_cparams("parallel", "arbitrary"),
        name="ec_select",
    )(aff3)
    return idx.reshape(N_EXPERTS * cap), gate.reshape(N_EXPERTS * cap, LANES)


N_FFN_BUFS = 3


def _ffn_kernel(idx_ref, gate_ref, wg_ref, wu_ref, wd_ref, xh_in, xh_hbm, buf, sem_g, sem_o, *, tc, n_tiles):
    del xh_in
    e, s = pl.program_id(0), pl.program_id(1)
    slot = s % N_FFN_BUFS

    def start_gather(tile, dst_slot):
        base = (e * n_tiles + tile) * tc

        def body(i, carry):
            t = idx_ref[base + i]
            pltpu.make_async_copy(xh_hbm.at[pl.ds(t, 1)], buf.at[dst_slot, pl.ds(i, 1)], sem_g.at[dst_slot]).start()
            return carry

        lax.fori_loop(0, tc, body, 0, unroll=8)

    def wait_gather(dst_slot):
        pltpu.make_async_copy(xh_hbm.at[pl.ds(0, tc)], buf.at[dst_slot], sem_g.at[dst_slot]).wait()

    def start_scatter(tile, src_slot):
        base = (e * n_tiles + tile) * tc

        def body(i, carry):
            t = idx_ref[base + i]
            pltpu.make_async_copy(buf.at[src_slot, pl.ds(i, 1), pl.ds(0, D_MODEL)],
                                  xh_hbm.at[pl.ds(t, 1), pl.ds(0, D_MODEL)], sem_o.at[src_slot]).start()
            return carry

        lax.fori_loop(0, tc, body, 0, unroll=8)

    def wait_scatter(src_slot):
        pltpu.make_async_copy(buf.at[src_slot, :, pl.ds(0, D_MODEL)],
                              xh_hbm.at[pl.ds(0, tc), pl.ds(0, D_MODEL)], sem_o.at[src_slot]).wait()

    @pl.when(s == 0)
    def _():
        @pl.when(e > 0)
        def _():
            for k in range(N_FFN_BUFS):
                wait_scatter(k)

        start_gather(0, 0)

    wait_gather(slot)

    @pl.when(s + 1 < n_tiles)
    def _():
        nxt = (s + 1) % N_FFN_BUFS

        @pl.when(s + 1 >= N_FFN_BUFS)
        def _():
            wait_scatter(nxt)

        start_gather(s + 1, nxt)

    h = buf[slot, :, D_MODEL:].astype(BF16)
    g = _dot(h, wg_ref[0])
    u = _dot(h, wu_ref[0])
    act = (g * jax.nn.sigmoid(g) * u).astype(BF16)
    y = _dot(act, wd_ref[0]) * gate_ref[:, 0:1]
    buf[slot, :, :D_MODEL] = buf[slot, :, :D_MODEL] + y
    start_scatter(s, slot)

    @pl.when((e == N_EXPERTS - 1) & (s == n_tiles - 1))
    def _():
        for k in range(N_FFN_BUFS):
            wait_scatter(k)


def _ffn(idx, gate, xh, w_gate, w_up, w_down, cap, tc=256):
    n_tiles = cap // tc
    assert n_tiles >= N_FFN_BUFS
    wspec = pl.BlockSpec((1, D_MODEL, D_MODEL), lambda e, s, idx_ref: (e, 0, 0))
    any_spec = pl.BlockSpec(memory_space=pl.ANY)
    grid_spec = pltpu.PrefetchScalarGridSpec(
        num_scalar_prefetch=1,
        grid=(N_EXPERTS, n_tiles),
        in_specs=[pl.BlockSpec((tc, LANES), lambda e, s, idx_ref: (e * n_tiles + s, 0)),
                  wspec, wspec, wspec, any_spec],
        out_specs=any_spec,
        scratch_shapes=[pltpu.VMEM((N_FFN_BUFS, tc, 2 * D_MODEL), F32),
                        pltpu.SemaphoreType.DMA((N_FFN_BUFS,)), pltpu.SemaphoreType.DMA((N_FFN_BUFS,))],
    )
    return pl.pallas_call(
        functools.partial(_ffn_kernel, tc=tc, n_tiles=n_tiles),
        grid_spec=grid_spec,
        out_shape=jax.ShapeDtypeStruct(xh.shape, F32),
        input_output_aliases={5: 0},
        compiler_params=_cparams("arbitrary", "arbitrary"),
        name="ec_ffn",
    )(idx, gate, w_gate, w_up, w_down, xh)


def _final_kernel(x_ref, g_ref, o_ref):
    o_ref[...] = _rms(x_ref[...], g_ref[...])


def _final_norm(x, g, tm=1024):
    t = x.shape[0]
    return pl.pallas_call(
        _final_kernel,
        grid=(t // tm,),
        in_specs=[pl.BlockSpec((tm, D_MODEL), lambda i: (i, 0)), pl.BlockSpec(g.shape, lambda i: (0, 0))],
        out_specs=pl.BlockSpec((tm, D_MODEL), lambda i: (i, 0)),
        out_shape=jax.ShapeDtypeStruct((t, D_MODEL), F32),
        compiler_params=_cparams("parallel"),
        name="final_norm",
    )(x, g)


def _moe(xh, aff_t, w_gate, w_up, w_down):
    cap = EC_CAPACITY_FACTOR * xh.shape[0] // N_EXPERTS
    idx, gate = _select(aff_t, cap)
    return _ffn(idx, gate, xh, w_gate, w_up, w_down, cap)


def _trunk(x3, p):
    b = x3.shape[0]
    t = b * SEQ
    x = x3.reshape(t, D_MODEL)
    seq = lambda a: a.reshape(b, SEQ, a.shape[-1])
    flat = lambda a: a.reshape(t, a.shape[-1])

    qm, km, vm, qs, ks, vs = _pre_ab(x, p["g_mix0"], p["w_big_ab"], p["q_norm"], p["w_q"], p["kv_norm"], p["w_kv"],
                                     p["tabs"])
    o_a = _dense_attn(seq(qm), seq(km), seq(vm), None, n_heads=MLA_HEADS, dq=LANES, dv=MLA_V,
                      scale=(MLA_NOPE + MLA_ROPE) ** -0.5)
    o_b = _swa_attn(seq(qs), seq(ks), seq(vs), p["sink"])
    xh, aff_t = _post(flat(o_a), flat(o_b), x, p["w_out_ab"], p["g_ffn0"], p["w_router0"])
    x = _moe(xh, aff_t, p["w_gate0"], p["w_up0"], p["w_down0"])

    qc, kc, vc, qd, kd, vd = _pre_cd(x, p["g_mix1"], p["w_big_cd"], p["tabs"])
    o_c = _dense_attn(seq(qc), seq(kc), seq(vc), p["counts"], n_heads=DIL_HEADS, dq=HEAD_DIM, dv=HEAD_DIM,
                      scale=HEAD_DIM ** -0.5)
    o_d = _na_attn(seq(qd), seq(kd), seq(vd), p["na_bias"])
    xh, aff_t = _post(flat(o_c), flat(o_d), x, p["w_out_cd"], p["g_ffn1"], p["w_router1"])
    x = _moe(xh, aff_t, p["w_gate1"], p["w_up1"], p["w_down1"])
    return _final_norm(x, p["g_final"]).reshape(b, SEQ, D_MODEL)


def kernel(x_prompt, x_sample, norm_mix, norm_ffn, norm_final, w_in_ab, mla_q_norm, mla_w_uq, mla_kv_norm,
           mla_w_ukv, swa_sink, w_out_ab, w_in_cd, na_rpb, w_out_cd, w_router, w_gate, w_up, w_down):
    w_big_ab, w_q, w_kv = _prep_ab(w_in_ab[0], mla_w_uq[0], mla_w_ukv[0])
    p = {
        "tabs": _rope_tables(),
        "g_mix0": norm_mix[0:1], "g_mix1": norm_mix[1:2], "g_ffn0": norm_ffn[0:1], "g_ffn1": norm_ffn[1:2],
        "g_final": norm_final.reshape(1, D_MODEL),
        "w_big_ab": w_big_ab, "w_q": w_q, "w_kv": w_kv,
        "q_norm": mla_q_norm[0:1], "kv_norm": mla_kv_norm[0:1],
        "sink": jnp.repeat(swa_sink[0], HEAD_DIM).reshape(1, SWA_HEADS * HEAD_DIM),
        "w_out_ab": w_out_ab[0], "w_big_cd": _prep_cd(w_in_cd[0]), "w_out_cd": w_out_cd[0],
        "counts": _dilation_counts(), "na_bias": _na_bias(na_rpb[0]),
        "w_router0": w_router[0], "w_router1": w_router[1],
    }
    for l in range(2):
        p[f"w_gate{l}"] = w_gate[l].astype(BF16)
        p[f"w_up{l}"] = w_up[l].astype(BF16)
        p[f"w_down{l}"] = w_down[l].astype(BF16)
    return _trunk(x_prompt, p), _trunk(x_sample, p)
```

```python
import functools

import jax
import jax.numpy as jnp
import numpy as np
from jax import lax
from jax.experimental import pallas as pl
from jax.experimental.pallas import tpu as pltpu

D_MODEL = 1024
SEQ = 2048
HEAD_DIM = 64
ROPE_THETA = 10000.0
NEG_INF = -1e30
RMS_EPS = 1e-6

MLA_HEADS = 8
MLA_Q_RANK = 384
MLA_KV_RANK = 256
MLA_NOPE = 64
MLA_ROPE = 32
MLA_V = 64

SWA_HEADS = 8
SWA_KV_HEADS = 2
SWA_GROUP = SWA_HEADS // SWA_KV_HEADS
SWA_HALF_WINDOW = 128
SWA_BLOCK = 128

DIL_HEADS = 8
DIL_BRANCHES = ((128, 1), (512, 4), (2048, 16))

NA_HEADS = 8
GRID_W = 64
NA_KH = 8
NA_KW = 16

N_EXPERTS = 16
EC_CAPACITY_FACTOR = 2

SWA_HEAD_ORDER = tuple(g * SWA_GROUP + r for r in range(SWA_GROUP) for g in range(SWA_KV_HEADS))
LOG2E = 1.4426950408889634
LN2 = 0.6931471805599453

LANES = 128
VMEM_LIMIT = 56 * 1024 * 1024

BF16 = jnp.bfloat16
F32 = jnp.float32


def _cparams(*sem):
    return pltpu.CompilerParams(dimension_semantics=sem, vmem_limit_bytes=VMEM_LIMIT)


def _dot(a, b):
    return jnp.dot(a, b, preferred_element_type=F32)


def _dot_nt(a, b):
    return lax.dot_general(a, b, (((1,), (1,)), ((), ())), preferred_element_type=F32)


def _rms(x, g):
    return x * lax.rsqrt(jnp.mean(x * x, axis=-1, keepdims=True) + RMS_EPS) * g


def _rot_cols(w, n_heads, dim):
    k = w.shape[0]
    w = w.reshape(k, n_heads, dim)
    return jnp.concatenate([-w[..., dim // 2:], w[..., :dim // 2]], axis=-1).reshape(k, n_heads * dim)


def _rope_tables():
    def tables(dim):
        inv = 1.0 / (ROPE_THETA ** (jnp.arange(0, dim, 2, dtype=F32) / dim))
        ang = jnp.arange(SEQ, dtype=F32)[:, None] * inv[None, :]
        c, s = jnp.cos(ang), jnp.sin(ang)
        return jnp.concatenate([c, c], axis=-1), jnp.concatenate([s, s], axis=-1)

    c32, s32 = tables(MLA_ROPE)
    one = jnp.ones((SEQ, MLA_NOPE), F32)
    zero = jnp.zeros((SEQ, MLA_NOPE), F32)
    z32 = jnp.zeros((SEQ, LANES - MLA_NOPE - MLA_ROPE), F32)
    cos_a = jnp.concatenate([one, c32, z32], axis=-1)
    sin_a = jnp.concatenate([zero, s32, z32], axis=-1)
    c64, s64 = tables(HEAD_DIM)
    cos_b = jnp.concatenate([c64, c64], axis=-1)
    sin_b = jnp.concatenate([s64, s64], axis=-1)
    return cos_a, sin_a, cos_b, sin_b


def _pad_heads(w, n_heads, dim):
    k = w.shape[0]
    w = w.reshape(k, n_heads, dim)
    return jnp.pad(w, ((0, 0), (0, 0), (0, LANES - dim))).reshape(k, n_heads * LANES)


def _prep_ab(w_in, w_uq, w_ukv):
    cq = w_in[:, :MLA_Q_RANK]
    o = MLA_Q_RANK
    ckv = w_in[:, o:o + MLA_KV_RANK]
    o += MLA_KV_RANK
    kr = w_in[:, o:o + MLA_ROPE]
    o += MLA_ROPE
    qb = w_in[:, o:o + SWA_HEADS * HEAD_DIM]
    qb = qb.reshape(D_MODEL, SWA_HEADS, HEAD_DIM)[:, list(SWA_HEAD_ORDER)].reshape(D_MODEL, SWA_HEADS * HEAD_DIM)
    o += SWA_HEADS * HEAD_DIM
    kb = w_in[:, o:o + SWA_KV_HEADS * HEAD_DIM]
    o += SWA_KV_HEADS * HEAD_DIM
    vb = w_in[:, o:]
    kr_rot = _rot_cols(kr, 1, MLA_ROPE)
    lead = jnp.zeros((D_MODEL, MLA_NOPE), F32)
    tail = jnp.zeros((D_MODEL, LANES - MLA_NOPE - MLA_ROPE), F32)
    kr_pad = jnp.concatenate([lead, kr, tail], axis=1)
    kr_rot_pad = jnp.concatenate([lead, kr_rot, tail], axis=1)
    w_big = jnp.concatenate([cq, ckv, qb, _rot_cols(qb, SWA_HEADS, HEAD_DIM), kb,
                             _rot_cols(kb, SWA_KV_HEADS, HEAD_DIM), vb, kr_pad, kr_rot_pad], axis=1)
    dq = MLA_NOPE + MLA_ROPE
    wq = w_uq.reshape(MLA_Q_RANK, MLA_HEADS, dq)
    rope = wq[..., MLA_NOPE:]
    rope_rot = jnp.concatenate([-rope[..., MLA_ROPE // 2:], rope[..., :MLA_ROPE // 2]], axis=-1)
    wq_rot = jnp.concatenate([jnp.zeros_like(wq[..., :MLA_NOPE]), rope_rot], axis=-1)
    wq_pad = _pad_heads(wq.reshape(MLA_Q_RANK, -1), MLA_HEADS, dq)
    wq_rot_pad = _pad_heads(wq_rot.reshape(MLA_Q_RANK, -1), MLA_HEADS, dq)
    w_q = jnp.concatenate([wq_pad, wq_rot_pad], axis=1)
    wkv = w_ukv.reshape(MLA_KV_RANK, MLA_HEADS, MLA_NOPE + MLA_V)
    wk_pad = _pad_heads(wkv[..., :MLA_NOPE].reshape(MLA_KV_RANK, -1), MLA_HEADS, MLA_NOPE)
    wv = wkv[..., MLA_NOPE:].reshape(MLA_KV_RANK, MLA_HEADS * MLA_V)
    w_kv = jnp.concatenate([wk_pad, wv], axis=1)
    return w_big.astype(BF16), w_q.astype(BF16), w_kv.astype(BF16)


def _permute_swa_rows(w_out):
    n = MLA_HEADS * MLA_V
    wb = w_out[n:].reshape(SWA_HEADS, HEAD_DIM, D_MODEL)[jnp.array(SWA_HEAD_ORDER)]
    return jnp.concatenate([w_out[:n], wb.reshape(SWA_HEADS * HEAD_DIM, D_MODEL)], axis=0)


def _prep_cd(w_in):
    n = DIL_HEADS * HEAD_DIM
    q_c, k_c, v_c, q_d, k_d, v_d = [w_in[:, i * n:(i + 1) * n] for i in range(6)]
    w_big = jnp.concatenate([q_c, _rot_cols(q_c, DIL_HEADS, HEAD_DIM), k_c, _rot_cols(k_c, DIL_HEADS, HEAD_DIM),
                             v_c, q_d, k_d, v_d], axis=1)
    return w_big.astype(BF16)


def _dilation_bias():
    d = np.arange(SEQ)[:, None] - np.arange(SEQ)[None, :]
    c = np.zeros((SEQ, SEQ), np.float64)
    for window, dil in DIL_BRANCHES:
        c += (d % dil == 0) & (np.abs(d) <= window // 2)
    return jnp.asarray(np.where(c > 0, np.log2(np.maximum(c, 1.0)), NEG_INF), dtype=F32)


def _pre_ab_kernel(x_ref, g_ref, wbig_ref, qn_ref, wq_ref, kvn_ref, wkv_ref, ca_ref, sa_ref, cb_ref, sb_ref,
                   qm_ref, km_ref, vm_ref, qs_ref, ks_ref, vs_ref):
    h = _rms(x_ref[...], g_ref[...]).astype(BF16)
    proj = _dot(h, wbig_ref[...])
    ca, sa, cb, sb = ca_ref[...], sa_ref[...], cb_ref[...], sb_ref[...]
    o = 0
    c_q = proj[:, o:o + MLA_Q_RANK]
    o += MLA_Q_RANK
    c_kv = proj[:, o:o + MLA_KV_RANK]
    o += MLA_KV_RANK
    nq = SWA_HEADS * HEAD_DIM
    q_b, q_b_rot = proj[:, o:o + nq], proj[:, o + nq:o + 2 * nq]
    o += 2 * nq
    k_b, k_b_rot, v_b = proj[:, o:o + LANES], proj[:, o + LANES:o + 2 * LANES], proj[:, o + 2 * LANES:o + 3 * LANES]
    o += 3 * LANES
    k_r, k_r_rot = proj[:, o:o + LANES], proj[:, o + LANES:o + 2 * LANES]

    c_swa = HEAD_DIM ** -0.5 * LOG2E
    qs_ref[...] = jnp.concatenate(
        [(q_b[:, i * LANES:(i + 1) * LANES] * cb + q_b_rot[:, i * LANES:(i + 1) * LANES] * sb) * c_swa
         for i in range(nq // LANES)], axis=1).astype(BF16)
    ks_ref[...] = (k_b * cb + k_b_rot * sb).astype(BF16)
    vs_ref[...] = v_b.astype(BF16)

    q_all = _dot(_rms(c_q, qn_ref[...]).astype(BF16), wq_ref[...])
    nh = MLA_HEADS * LANES
    c_mla = (MLA_NOPE + MLA_ROPE) ** -0.5 * LOG2E
    qm_ref[...] = jnp.concatenate(
        [(q_all[:, i * LANES:(i + 1) * LANES] * ca + q_all[:, nh + i * LANES:nh + (i + 1) * LANES] * sa) * c_mla
         for i in range(MLA_HEADS)], axis=1).astype(BF16)
    kv = _dot(_rms(c_kv, kvn_ref[...]).astype(BF16), wkv_ref[...])
    k_rope = k_r * ca + k_r_rot * sa
    km_ref[...] = jnp.concatenate(
        [kv[:, i * LANES:(i + 1) * LANES] + k_rope for i in range(MLA_HEADS)], axis=1).astype(BF16)
    vm_ref[...] = kv[:, nh:].astype(BF16)


def _pre_ab(x, g, w_big, qn, w_q, kvn, w_kv, tabs, tm=512):
    t = x.shape[0]
    nsb = SEQ // tm
    full = lambda a: pl.BlockSpec(a.shape, lambda i: (0,) * a.ndim)
    tab = pl.BlockSpec((tm, LANES), lambda i: (i % nsb, 0))
    row = lambda n: pl.BlockSpec((tm, n), lambda i: (i, 0))
    outs = [(MLA_HEADS * LANES, BF16), (MLA_HEADS * LANES, BF16), (MLA_HEADS * MLA_V, BF16),
            (SWA_HEADS * HEAD_DIM, BF16), (LANES, BF16), (LANES, BF16)]
    return pl.pallas_call(
        _pre_ab_kernel,
        grid=(t // tm,),
        in_specs=[row(D_MODEL), full(g), full(w_big), full(qn), full(w_q), full(kvn), full(w_kv), tab, tab, tab, tab],
        out_specs=[row(n) for n, _ in outs],
        out_shape=[jax.ShapeDtypeStruct((t, n), d) for n, d in outs],
        compiler_params=_cparams("parallel"),
        name="pre_ab",
    )(x, g, w_big, qn, w_q, kvn, w_kv, *tabs)


def _pre_cd_kernel(x_ref, g_ref, wbig_ref, cb_ref, sb_ref, qc_ref, kc_ref, vc_ref, qd_ref, kd_ref, vd_ref):
    h = _rms(x_ref[...], g_ref[...]).astype(BF16)
    cb, sb = cb_ref[...], sb_ref[...]
    n = DIL_HEADS * HEAD_DIM
    w = wbig_ref

    c_att = HEAD_DIM ** -0.5 * LOG2E

    def roped(o, c):
        a = _dot(h, w[:, o:o + n])
        b = _dot(h, w[:, o + n:o + 2 * n])
        return jnp.concatenate([(a[:, i * LANES:(i + 1) * LANES] * cb + b[:, i * LANES:(i + 1) * LANES] * sb) * c
                                for i in range(n // LANES)], axis=1).astype(BF16)

    qc_ref[...] = roped(0, c_att)
    kc_ref[...] = roped(2 * n, 1.0)
    vc_ref[...] = _dot(h, w[:, 4 * n:5 * n]).astype(BF16)
    qd_ref[...] = (_dot(h, w[:, 5 * n:6 * n]) * c_att).astype(BF16)
    kd_ref[...] = _dot(h, w[:, 6 * n:7 * n]).astype(BF16)
    vd_ref[...] = _dot(h, w[:, 7 * n:8 * n]).astype(BF16)


def _pre_cd(x, g, w_big, tabs, tm=512):
    t = x.shape[0]
    nsb = SEQ // tm
    n = DIL_HEADS * HEAD_DIM
    full = lambda a: pl.BlockSpec(a.shape, lambda i: (0,) * a.ndim)
    tab = pl.BlockSpec((tm, LANES), lambda i: (i % nsb, 0))
    row = lambda m: pl.BlockSpec((tm, m), lambda i: (i, 0))
    return pl.pallas_call(
        _pre_cd_kernel,
        grid=(t // tm,),
        in_specs=[row(D_MODEL), full(g), full(w_big), tab, tab],
        out_specs=[row(n)] * 6,
        out_shape=[jax.ShapeDtypeStruct((t, n), BF16)] * 6,
        compiler_params=_cparams("parallel"),
        name="pre_cd",
    )(x, g, w_big, tabs[2], tabs[3])


def _low_half():
    return lax.broadcasted_iota(jnp.int32, (1, LANES), 1) < LANES // 2


def _split_heads(x2, low):
    zero = jnp.zeros_like(x2)
    return jnp.where(low, x2, zero), jnp.where(low, zero, x2)


def _softmax_pv(s, v2):
    m = jnp.max(s, axis=-1, keepdims=True)
    e = jnp.exp2(s - m)
    l = jnp.sum(e, axis=-1, keepdims=True)
    return _dot(e.astype(BF16), v2) / l


def _dense_attn_kernel(*refs, n_pairs, q_per_head, biased):
    if biased:
        q_ref, k_ref, v_ref, b_ref, o_ref = refs
        bias = b_ref[...]
    else:
        q_ref, k_ref, v_ref, o_ref = refs
    low = _low_half()
    outs = []
    for p in range(n_pairs):
        v2 = v_ref[0, :, p * LANES:(p + 1) * LANES]
        if q_per_head:
            qs = [q_ref[0, :, (2 * p + a) * LANES:(2 * p + a + 1) * LANES] for a in range(2)]
            ks = [k_ref[0, :, (2 * p + a) * LANES:(2 * p + a + 1) * LANES] for a in range(2)]
        else:
            qs = _split_heads(q_ref[0, :, p * LANES:(p + 1) * LANES], low)
            ks = [k_ref[0, :, p * LANES:(p + 1) * LANES]] * 2
        halves = []
        for q, k in zip(qs, ks):
            s = _dot_nt(q, k)
            if biased:
                s = s + bias
            halves.append(_softmax_pv(s, v2))
        outs.append(jnp.where(low, halves[0], halves[1]))
    o_ref[0] = jnp.concatenate(outs, axis=1).astype(o_ref.dtype)


def _dense_attn(q, k, v, bias, *, q_per_head, tq=256):
    b = q.shape[0]
    nq, nv = q.shape[-1], v.shape[-1]
    biased = bias is not None
    in_specs = [pl.BlockSpec((1, tq, nq), lambda i, j: (i, j, 0)),
                pl.BlockSpec((1, SEQ, nq), lambda i, j: (i, 0, 0)),
                pl.BlockSpec((1, SEQ, nv), lambda i, j: (i, 0, 0))]
    args = [q, k, v]
    if biased:
        in_specs.append(pl.BlockSpec((tq, SEQ), lambda i, j: (j, 0)))
        args.append(bias)
    return pl.pallas_call(
        functools.partial(_dense_attn_kernel, n_pairs=nv // LANES, q_per_head=q_per_head, biased=biased),
        grid=(b, SEQ // tq),
        in_specs=in_specs,
        out_specs=pl.BlockSpec((1, tq, nv), lambda i, j: (i, j, 0)),
        out_shape=jax.ShapeDtypeStruct((b, SEQ, nv), BF16),
        compiler_params=_cparams("parallel", "parallel"),
        name="dilated_attn" if biased else "mla_attn",
    )(*args)


def _swa_kernel(q_ref, kp_ref, kc_ref, kn_ref, vp_ref, vc_ref, vn_ref, sink_ref, o_ref, *, blk):
    i = pl.program_id(1)
    low = _low_half()
    stacked = []
    for r in range(SWA_GROUP):
        stacked += _split_heads(q_ref[0, :, r * LANES:(r + 1) * LANES], low)
    q = jnp.concatenate(stacked, axis=0)
    k = jnp.concatenate([kp_ref[0], kc_ref[0], kn_ref[0]], axis=0)
    v = jnp.concatenate([vp_ref[0], vc_ref[0], vn_ref[0]], axis=0)
    rows = SWA_HEADS * blk
    qpos = i * blk + lax.broadcasted_iota(jnp.int32, (blk, 3 * blk), 0)
    kpos = (i - 1) * blk + lax.broadcasted_iota(jnp.int32, (blk, 3 * blk), 1)
    valid = (jnp.abs(qpos - kpos) <= SWA_HALF_WINDOW) & (kpos >= 0) & (kpos < SEQ)
    mask = jnp.where(valid, 0.0, NEG_INF)
    s = (_dot_nt(q, k).reshape(SWA_HEADS, blk, 3 * blk) + mask[None]).reshape(rows, 3 * blk)
    m = jnp.max(s, axis=-1, keepdims=True)
    e = jnp.exp2(s - m)
    l = jnp.sum(e, axis=-1, keepdims=True)
    lse = m * LN2 + jnp.log(l)
    keep = jax.nn.sigmoid(lse - sink_ref[...])
    o = _dot(e.astype(BF16), v) * (keep / l)
    o_ref[0] = jnp.concatenate(
        [jnp.where(low, o[2 * r * blk:(2 * r + 1) * blk], o[(2 * r + 1) * blk:(2 * r + 2) * blk])
         for r in range(SWA_GROUP)], axis=1).astype(o_ref.dtype)


def _swa_attn(q, k, v, sink_rows, blk=SWA_BLOCK):
    b = q.shape[0]
    nb = SEQ // blk
    kv = lambda f: pl.BlockSpec((1, blk, LANES), f)
    prev = lambda i, j: (i, jnp.maximum(j - 1, 0), 0)
    cur = lambda i, j: (i, j, 0)
    nxt = lambda i, j: (i, jnp.minimum(j + 1, nb - 1), 0)
    return pl.pallas_call(
        functools.partial(_swa_kernel, blk=blk),
        grid=(b, nb),
        in_specs=[pl.BlockSpec((1, blk, SWA_HEADS * HEAD_DIM), cur), kv(prev), kv(cur), kv(nxt),
                  kv(prev), kv(cur), kv(nxt), pl.BlockSpec(sink_rows.shape, lambda i, j: (0, 0))],
        out_specs=pl.BlockSpec((1, blk, SWA_HEADS * HEAD_DIM), cur),
        out_shape=jax.ShapeDtypeStruct((b, SEQ, SWA_HEADS * HEAD_DIM), BF16),
        compiler_params=_cparams("parallel", "parallel"),
        name="swa_attn",
    )(q, k, k, k, v, v, v, sink_rows)


NA_Q_ROWS = 4
NA_Q = NA_Q_ROWS * GRID_W
NA_K_ROWS = 12
NA_K = NA_K_ROWS * GRID_W
NA_GROUPS = SEQ // NA_Q
NA_RPB_ROWS = 2 * NA_KH - 1
NA_RPB_COLS = 2 * NA_KW - 1
NA_WIN0 = (lambda a: 0, lambda a: a, lambda a: NA_Q_ROWS)
NA_RHO0 = (NA_KH - 1, NA_KH - 1 - NA_Q_ROWS, NA_KH - 1 - 2 * NA_Q_ROWS)


def _na_bias_kernel(rpb_ref, o_ref):
    pair = pl.program_id(0)
    c = lax.broadcasted_iota(jnp.int32, (GRID_W, LANES), 0)
    lane = lax.broadcasted_iota(jnp.int32, (GRID_W, LANES), 1)
    odd = lane >= GRID_W
    cp = lane & (GRID_W - 1)
    c0 = jnp.clip(c - NA_KW // 2, 0, GRID_W - NA_KW)
    col_ok = (cp >= c0) & (cp < c0 + NA_KW)
    d = cp - c + (NA_KW - 1)
    neg = jnp.full((GRID_W, LANES), NEG_INF, F32)

    for a in range(2):
        head = 2 * pair + a
        cache = {}

        def tile(rho_even, ok_even, ok_odd):
            key = (rho_even, ok_even, ok_odd)
            if key not in cache:
                acc = neg
                for kk in range(NA_RPB_COLS):
                    lo = rpb_ref[(head * NA_RPB_ROWS + rho_even) * NA_RPB_COLS + kk] if ok_even else NEG_INF
                    hi = rpb_ref[(head * NA_RPB_ROWS + rho_even + 1) * NA_RPB_COLS + kk] if ok_odd else NEG_INF
                    acc = jnp.where(d == kk, jnp.where(odd, hi, lo), acc)
                cache[key] = jnp.where(col_ok, acc * LOG2E, NEG_INF)
            return cache[key]

        for pat in range(3):
            for qa in range(NA_Q_ROWS):
                w0 = NA_WIN0[pat](qa)
                tiles = []
                for j in range(0, NA_K_ROWS, 2):
                    ok_even = w0 <= j < w0 + NA_KH
                    ok_odd = w0 <= j + 1 < w0 + NA_KH
                    rho = j - qa + NA_RHO0[pat]
                    tiles.append(tile(rho, ok_even, ok_odd) if (ok_even or ok_odd) else neg)
                o_ref[pat, 0, a, qa * GRID_W:(qa + 1) * GRID_W, :] = jnp.concatenate(tiles, axis=1)


def _na_bias(rpb):
    return pl.pallas_call(
        _na_bias_kernel,
        grid=(NA_HEADS // 2,),
        in_specs=[pl.BlockSpec(memory_space=pltpu.SMEM)],
        out_specs=pl.BlockSpec((3, 1, 2, NA_Q, NA_K), lambda p: (0, p, 0, 0, 0)),
        out_shape=jax.ShapeDtypeStruct((3, NA_HEADS // 2, 2, NA_Q, NA_K), F32),
        compiler_params=_cparams("parallel"),
        name="na_bias",
    )(rpb.reshape(-1))


def _na_kernel(q_ref, k0_ref, k1_ref, k2_ref, v0_ref, v1_ref, v2_ref, bias_ref, o_ref):
    low = _low_half()
    k = jnp.concatenate([k0_ref[0], k1_ref[0], k2_ref[0]], axis=0)
    v = jnp.concatenate([v0_ref[0], v1_ref[0], v2_ref[0]], axis=0)
    halves = [_softmax_pv(_dot_nt(q, k) + bias_ref[0, 0, a], v)
              for a, q in enumerate(_split_heads(q_ref[0], low))]
    o_ref[0] = jnp.where(low, halves[0], halves[1]).astype(o_ref.dtype)


def _na_attn(q, k, v, bias):
    b = q.shape[0]
    first_block = lambda g: jnp.clip(g - 1, 0, NA_GROUPS - NA_K // NA_Q)
    pattern = lambda g: jnp.where(g == 0, 0, jnp.where(g == NA_GROUPS - 1, 2, 1))
    cur = pl.BlockSpec((1, NA_Q, LANES), lambda p, g, i: (i, g, p))
    kv = lambda n: pl.BlockSpec((1, NA_Q, LANES), lambda p, g, i: (i, first_block(g) + n, p))
    return pl.pallas_call(
        _na_kernel,
        grid=(NA_HEADS // 2, NA_GROUPS, b),
        in_specs=[cur, kv(0), kv(1), kv(2), kv(0), kv(1), kv(2),
                  pl.BlockSpec((1, 1, 2, NA_Q, NA_K), lambda p, g, i: (pattern(g), p, 0, 0, 0))],
        out_specs=cur,
        out_shape=jax.ShapeDtypeStruct((b, SEQ, NA_HEADS * HEAD_DIM), BF16),
        compiler_params=_cparams("parallel", "parallel", "parallel"),
        name="na_attn",
    )(q, k, k, k, v, v, v, bias)


def _post_kernel(oa_ref, ob_ref, x_ref, w1_ref, w2_ref, g_ref, wr_ref, xh_ref, aff_ref):
    x = x_ref[...] + _dot(oa_ref[...], w1_ref[...]) + _dot(ob_ref[...], w2_ref[...])
    h = _rms(x, g_ref[...])
    xh_ref[:, :D_MODEL] = x
    xh_ref[:, D_MODEL:] = h
    logits = _dot_nt(wr_ref[...], h.astype(BF16))
    e = jnp.exp(logits - jnp.max(logits, axis=0, keepdims=True))
    aff_ref[...] = e / jnp.sum(e, axis=0, keepdims=True)


def _post(oa, ob, x, w_out, g, w_router, tm=512):
    t = x.shape[0]
    half = oa.shape[1]
    w1, w2 = w_out[:half].astype(BF16), w_out[half:].astype(BF16)
    wr = w_router.T.astype(BF16)
    full = lambda a: pl.BlockSpec(a.shape, lambda i: (0,) * a.ndim)
    row = lambda n: pl.BlockSpec((tm, n), lambda i: (i, 0))
    return pl.pallas_call(
        _post_kernel,
        grid=(t // tm,),
        in_specs=[row(half), row(half), row(D_MODEL), full(w1), full(w2), full(g), full(wr)],
        out_specs=[row(2 * D_MODEL), pl.BlockSpec((N_EXPERTS, tm), lambda i: (0, i))],
        out_shape=[jax.ShapeDtypeStruct((t, 2 * D_MODEL), F32), jax.ShapeDtypeStruct((N_EXPERTS, t), F32)],
        compiler_params=_cparams("parallel"),
        name="post_attn",
    )(oa, ob, x, w1, w2, g, wr)


def _select_kernel(aff_ref, idx_ref, gate_ref, incl_s, a_hi, a_mid, a_lo, start_s, end_s, cnt_s, *, cap, tc):
    s_id = pl.program_id(1)
    rows = aff_ref.shape[1]

    @pl.when(s_id == 0)
    def _():
        aff = aff_ref[0]
        bits = pltpu.bitcast(aff, jnp.int32)

        def search(i, prefix):
            cand = prefix | jnp.left_shift(jnp.int32(1), 30 - i)
            n = jnp.sum((bits >= cand).astype(F32), keepdims=True)
            return jnp.where(n >= cap, cand, prefix)

        thr = lax.fori_loop(0, 31, search, jnp.zeros((1, 1), jnp.int32))
        gt = bits > thr
        eq = bits == thr
        need = cap - jnp.sum(gt.astype(F32), keepdims=True)
        li = lax.broadcasted_iota(jnp.int32, (LANES, LANES), 0)
        lj = lax.broadcasted_iota(jnp.int32, (LANES, LANES), 1)
        tri = (li <= lj).astype(BF16)
        ones = jnp.ones((LANES, LANES), BF16)
        ri = lax.broadcasted_iota(jnp.int32, (rows, rows), 0)
        rj = lax.broadcasted_iota(jnp.int32, (rows, rows), 1)
        below = (rj < ri).astype(BF16)
        above = (ri < rj).astype(BF16)

        eqb = eq.astype(BF16)
        eq_rank = _dot(below, _dot(eqb, ones).astype(BF16)) + _dot(eqb, tri) - eqb.astype(F32)
        sel = (gt | (eq & (eq_rank < need))).astype(BF16)
        incl_s[...] = _dot(sel, tri).astype(BF16)
        cnt = _dot_nt(jnp.ones((8, LANES), BF16), sel)
        start = _dot(cnt.astype(BF16), above)
        cnt_s[...] = cnt
        start_s[...] = start
        end_s[...] = start + cnt
        hi = aff.astype(BF16)
        r1 = aff - hi.astype(F32)
        mid = r1.astype(BF16)
        a_hi[...] = hi
        a_mid[...] = mid
        a_lo[...] = (r1 - mid.astype(F32)).astype(BF16)

    c = (s_id * tc + lax.broadcasted_iota(jnp.int32, (tc, rows), 0)).astype(F32)
    start, end, cnt = start_s[0:1, :], end_s[0:1, :], cnt_s[0:1, :]
    before = c >= end
    row_id = jnp.sum(before.astype(F32), axis=1, keepdims=True)
    row_start = jnp.sum(jnp.where(before, cnt, 0.0), axis=1, keepdims=True)
    onehot = ((c >= start) & (c < end)).astype(BF16)
    within = _dot(onehot, incl_s[...])
    k = c[:, 0:1] - row_start
    lane = jnp.sum((within <= k).astype(F32), axis=1, keepdims=True)
    idx_ref[0] = (row_id * LANES + lane).astype(jnp.int32)
    vals = _dot(onehot, a_hi[...]) + _dot(onehot, a_mid[...]) + _dot(onehot, a_lo[...])
    lane_iota = lax.broadcasted_iota(jnp.int32, (tc, LANES), 1).astype(F32)
    gate = jnp.sum(jnp.where(lane_iota == lane, vals, 0.0), axis=1, keepdims=True)
    gate_ref[0] = jnp.broadcast_to(gate, (tc, LANES))


def _select(aff_t, cap, tc=256):
    t = aff_t.shape[1]
    rows = t // LANES
    aff3 = aff_t.reshape(N_EXPERTS, rows, LANES)
    idx, gate = pl.pallas_call(
        functools.partial(_select_kernel, cap=cap, tc=tc),
        grid=(N_EXPERTS, cap // tc),
        in_specs=[pl.BlockSpec((1, rows, LANES), lambda e, s: (e, 0, 0))],
        out_specs=[pl.BlockSpec((1, tc, 1), lambda e, s: (e, s, 0)),
                   pl.BlockSpec((1, tc, LANES), lambda e, s: (e, s, 0))],
        out_shape=[jax.ShapeDtypeStruct((N_EXPERTS, cap, 1), jnp.int32),
                   jax.ShapeDtypeStruct((N_EXPERTS, cap, LANES), F32)],
        scratch_shapes=[pltpu.VMEM((rows, LANES), BF16)] * 4 + [pltpu.VMEM((8, rows), F32)] * 3,
        compiler_params=_cparams("parallel", "arbitrary"),
        name="ec_select",
    )(aff3)
    return idx.reshape(N_EXPERTS * cap), gate.reshape(N_EXPERTS * cap, LANES)


FFN_SUB = 4


def _ffn_kernel(idx_ref, gate_ref, wg_ref, wu_ref, wd_ref, xh_in, xh_hbm, *scratch, tc, n_steps):
    del xh_in
    bufs, (sem_g, sem_o) = scratch[:FFN_SUB], scratch[FFN_SUB:]
    e, s = pl.program_id(0), pl.program_id(1)
    step_base = (e * n_steps + s) * (FFN_SUB * tc)

    def start_gather(base, j):
        for i in range(tc):
            t = idx_ref[base + i]
            pltpu.make_async_copy(xh_hbm.at[pl.ds(t, 1)], bufs[j].at[pl.ds(i, 1)], sem_g.at[j]).start()

    def wait_gather(j):
        pltpu.make_async_copy(xh_hbm.at[pl.ds(0, tc)], bufs[j], sem_g.at[j]).wait()

    def start_scatter(base, j):
        for i in range(tc):
            t = idx_ref[base + i]
            pltpu.make_async_copy(bufs[j].at[pl.ds(i, 1), pl.ds(0, D_MODEL)],
                                  xh_hbm.at[pl.ds(t, 1), pl.ds(0, D_MODEL)], sem_o.at[j]).start()

    def wait_scatter(j):
        pltpu.make_async_copy(bufs[j].at[:, pl.ds(0, D_MODEL)],
                              xh_hbm.at[pl.ds(0, tc), pl.ds(0, D_MODEL)], sem_o.at[j]).wait()

    @pl.when(s == 0)
    def _():
        @pl.when(e > 0)
        def _():
            for j in range(FFN_SUB):
                wait_scatter(j)

        start_gather(step_base, 0)

    wait_gather(0)
    for j in range(FFN_SUB):
        if j + 1 < FFN_SUB:
            @pl.when(s > 0)
            def _():
                wait_scatter(j + 1)

            start_gather(step_base + (j + 1) * tc, j + 1)
        else:
            @pl.when(s + 1 < n_steps)
            def _():
                wait_scatter(0)
                start_gather(step_base + FFN_SUB * tc, 0)
        if j > 0:
            start_scatter(step_base + (j - 1) * tc, j - 1)

        h = bufs[j][:, D_MODEL:].astype(BF16)
        g = _dot(h, wg_ref[0])
        u = _dot(h, wu_ref[0])
        act = (g * jax.nn.sigmoid(g) * u).astype(BF16)
        y = _dot(act, wd_ref[0]) * gate_ref[j * tc:(j + 1) * tc, 0:1]
        bufs[j][:, :D_MODEL] = bufs[j][:, :D_MODEL] + y
        if j + 1 < FFN_SUB:
            wait_gather(j + 1)
    start_scatter(step_base + (FFN_SUB - 1) * tc, FFN_SUB - 1)

    @pl.when((e == N_EXPERTS - 1) & (s == n_steps - 1))
    def _():
        for j in range(FFN_SUB):
            wait_scatter(j)


def _ffn(idx, gate, xh, w_gate, w_up, w_down, cap, tc=256):
    n_steps = cap // (FFN_SUB * tc)
    assert n_steps * FFN_SUB * tc == cap
    wspec = pl.BlockSpec((1, D_MODEL, D_MODEL), lambda e, s, idx_ref: (e, 0, 0))
    any_spec = pl.BlockSpec(memory_space=pl.ANY)
    grid_spec = pltpu.PrefetchScalarGridSpec(
        num_scalar_prefetch=1,
        grid=(N_EXPERTS, n_steps),
        in_specs=[pl.BlockSpec((FFN_SUB * tc, LANES), lambda e, s, idx_ref: (e * n_steps + s, 0)),
                  wspec, wspec, wspec, any_spec],
        out_specs=any_spec,
        scratch_shapes=[pltpu.VMEM((tc, 2 * D_MODEL), F32)] * FFN_SUB
        + [pltpu.SemaphoreType.DMA((FFN_SUB,)), pltpu.SemaphoreType.DMA((FFN_SUB,))],
    )
    return pl.pallas_call(
        functools.partial(_ffn_kernel, tc=tc, n_steps=n_steps),
        grid_spec=grid_spec,
        out_shape=jax.ShapeDtypeStruct(xh.shape, F32),
        input_output_aliases={5: 0},
        compiler_params=_cparams("arbitrary", "arbitrary"),
        name="ec_ffn",
    )(idx, gate, w_gate, w_up, w_down, xh)


def _final_kernel(x_ref, g_ref, o_ref):
    o_ref[...] = _rms(x_ref[...], g_ref[...])


def _final_norm(x, g, tm=1024):
    t = x.shape[0]
    return pl.pallas_call(
        _final_kernel,
        grid=(t // tm,),
        in_specs=[pl.BlockSpec((tm, D_MODEL), lambda i: (i, 0)), pl.BlockSpec(g.shape, lambda i: (0, 0))],
        out_specs=pl.BlockSpec((tm, D_MODEL), lambda i: (i, 0)),
        out_shape=jax.ShapeDtypeStruct((t, D_MODEL), F32),
        compiler_params=_cparams("parallel"),
        name="final_norm",
    )(x, g)


def _moe(xh, aff_t, w_gate, w_up, w_down):
    cap = EC_CAPACITY_FACTOR * xh.shape[0] // N_EXPERTS
    idx, gate = _select(aff_t, cap)
    return _ffn(idx, gate, xh, w_gate, w_up, w_down, cap)


def _trunk(x3, p):
    b = x3.shape[0]
    t = b * SEQ
    x = x3.reshape(t, D_MODEL)
    seq = lambda a: a.reshape(b, SEQ, a.shape[-1])
    flat = lambda a: a.reshape(t, a.shape[-1])

    qm, km, vm, qs, ks, vs = _pre_ab(x, p["g_mix0"], p["w_big_ab"], p["q_norm"], p["w_q"], p["kv_norm"], p["w_kv"],
                                     p["tabs"])
    o_a = _dense_attn(seq(qm), seq(km), seq(vm), None, q_per_head=True)
    o_b = _swa_attn(seq(qs), seq(ks), seq(vs), p["sink"])
    xh, aff_t = _post(flat(o_a), flat(o_b), x, p["w_out_ab"], p["g_ffn0"], p["w_router0"])
    x = _moe(xh, aff_t, p["w_gate0"], p["w_up0"], p["w_down0"])

    qc, kc, vc, qd, kd, vd = _pre_cd(x, p["g_mix1"], p["w_big_cd"], p["tabs"])
    o_c = _dense_attn(seq(qc), seq(kc), seq(vc), p["dil_bias"], q_per_head=False)
    o_d = _na_attn(seq(qd), seq(kd), seq(vd), p["na_bias"])
    xh, aff_t = _post(flat(o_c), flat(o_d), x, p["w_out_cd"], p["g_ffn1"], p["w_router1"])
    x = _moe(xh, aff_t, p["w_gate1"], p["w_up1"], p["w_down1"])
    return _final_norm(x, p["g_final"]).reshape(b, SEQ, D_MODEL)


def kernel(x_prompt, x_sample, norm_mix, norm_ffn, norm_final, w_in_ab, mla_q_norm, mla_w_uq, mla_kv_norm,
           mla_w_ukv, swa_sink, w_out_ab, w_in_cd, na_rpb, w_out_cd, w_router, w_gate, w_up, w_down):
    w_big_ab, w_q, w_kv = _prep_ab(w_in_ab[0], mla_w_uq[0], mla_w_ukv[0])
    p = {
        "tabs": _rope_tables(),
        "g_mix0": norm_mix[0:1], "g_mix1": norm_mix[1:2], "g_ffn0": norm_ffn[0:1], "g_ffn1": norm_ffn[1:2],
        "g_final": norm_final.reshape(1, D_MODEL),
        "w_big_ab": w_big_ab, "w_q": w_q, "w_kv": w_kv,
        "q_norm": mla_q_norm[0:1], "kv_norm": mla_kv_norm[0:1],
        "sink": jnp.repeat(swa_sink[0][jnp.array(SWA_HEAD_ORDER)], SWA_BLOCK).reshape(SWA_HEADS * SWA_BLOCK, 1),
        "w_out_ab": _permute_swa_rows(w_out_ab[0]), "w_big_cd": _prep_cd(w_in_cd[0]), "w_out_cd": w_out_cd[0],
        "dil_bias": _dilation_bias(), "na_bias": _na_bias(na_rpb[0]),
        "w_router0": w_router[0], "w_router1": w_router[1],
    }
    for l in range(2):
        p[f"w_gate{l}"] = w_gate[l].astype(BF16)
        p[f"w_up{l}"] = w_up[l].astype(BF16)
        p[f"w_down{l}"] = w_down[l].astype(BF16)
    return _trunk(x_prompt, p), _trunk(x_sample, p)
```

```python
import functools

import jax
import jax.numpy as jnp
from jax import lax
from jax.experimental import pallas as pl
from jax.experimental.pallas import tpu as pltpu

D_MODEL = 1024
SEQ = 2048
HEAD_DIM = 64
ROPE_THETA = 10000.0
NEG_INF = -1e30
RMS_EPS = 1e-6

MLA_HEADS = 8
MLA_Q_RANK = 384
MLA_KV_RANK = 256
MLA_NOPE = 64
MLA_ROPE = 32
MLA_V = 64

SWA_HEADS = 8
SWA_KV_HEADS = 2
SWA_GROUP = SWA_HEADS // SWA_KV_HEADS
SWA_HALF_WINDOW = 128
SWA_BLOCK = 128

DIL_HEADS = 8
DIL_BRANCHES = ((128, 1), (512, 4), (2048, 16))

NA_HEADS = 8
GRID_W = 64
NA_KH = 8
NA_KW = 16

N_EXPERTS = 16
EC_CAPACITY_FACTOR = 2

SWA_HEAD_ORDER = tuple(g * SWA_GROUP + r for r in range(SWA_GROUP) for g in range(SWA_KV_HEADS))
LOG2E = 1.4426950408889634
LN2 = 0.6931471805599453

LANES = 128
VMEM_LIMIT = 56 * 1024 * 1024

BF16 = jnp.bfloat16
F32 = jnp.float32


def _cparams(*sem):
    return pltpu.CompilerParams(dimension_semantics=sem, vmem_limit_bytes=VMEM_LIMIT)


def _dot(a, b):
    return jnp.dot(a, b, preferred_element_type=F32)


def _dot_nt(a, b):
    return lax.dot_general(a, b, (((1,), (1,)), ((), ())), preferred_element_type=F32)


def _rms(x, g):
    return x * lax.rsqrt(jnp.mean(x * x, axis=-1, keepdims=True) + RMS_EPS) * g


def _rot_cols(w, n_heads, dim):
    k = w.shape[0]
    w = w.reshape(k, n_heads, dim)
    return jnp.concatenate([-w[..., dim // 2:], w[..., :dim // 2]], axis=-1).reshape(k, n_heads * dim)


def _rope_tables():
    def tables(dim):
        inv = 1.0 / (ROPE_THETA ** (jnp.arange(0, dim, 2, dtype=F32) / dim))
        ang = jnp.arange(SEQ, dtype=F32)[:, None] * inv[None, :]
        c, s = jnp.cos(ang), jnp.sin(ang)
        return jnp.concatenate([c, c], axis=-1), jnp.concatenate([s, s], axis=-1)

    c32, s32 = tables(MLA_ROPE)
    one = jnp.ones((SEQ, MLA_NOPE), F32)
    zero = jnp.zeros((SEQ, MLA_NOPE), F32)
    z32 = jnp.zeros((SEQ, LANES - MLA_NOPE - MLA_ROPE), F32)
    cos_a = jnp.concatenate([one, c32, z32], axis=-1)
    sin_a = jnp.concatenate([zero, s32, z32], axis=-1)
    c64, s64 = tables(HEAD_DIM)
    cos_b = jnp.concatenate([c64, c64], axis=-1)
    sin_b = jnp.concatenate([s64, s64], axis=-1)
    return cos_a, sin_a, cos_b, sin_b


def _pad_heads(w, n_heads, dim):
    k = w.shape[0]
    w = w.reshape(k, n_heads, dim)
    return jnp.pad(w, ((0, 0), (0, 0), (0, LANES - dim))).reshape(k, n_heads * LANES)


def _prep_ab(w_in, w_uq, w_ukv):
    cq = w_in[:, :MLA_Q_RANK]
    o = MLA_Q_RANK
    ckv = w_in[:, o:o + MLA_KV_RANK]
    o += MLA_KV_RANK
    kr = w_in[:, o:o + MLA_ROPE]
    o += MLA_ROPE
    qb = w_in[:, o:o + SWA_HEADS * HEAD_DIM]
    qb = qb.reshape(D_MODEL, SWA_HEADS, HEAD_DIM)[:, list(SWA_HEAD_ORDER)].reshape(D_MODEL, SWA_HEADS * HEAD_DIM)
    o += SWA_HEADS * HEAD_DIM
    kb = w_in[:, o:o + SWA_KV_HEADS * HEAD_DIM]
    o += SWA_KV_HEADS * HEAD_DIM
    vb = w_in[:, o:]
    kr_rot = _rot_cols(kr, 1, MLA_ROPE)
    lead = jnp.zeros((D_MODEL, MLA_NOPE), F32)
    tail = jnp.zeros((D_MODEL, LANES - MLA_NOPE - MLA_ROPE), F32)
    kr_pad = jnp.concatenate([lead, kr, tail], axis=1)
    kr_rot_pad = jnp.concatenate([lead, kr_rot, tail], axis=1)
    w_big = jnp.concatenate([cq, ckv, qb, _rot_cols(qb, SWA_HEADS, HEAD_DIM), kb,
                             _rot_cols(kb, SWA_KV_HEADS, HEAD_DIM), vb, kr_pad, kr_rot_pad], axis=1)
    dq = MLA_NOPE + MLA_ROPE
    wq = w_uq.reshape(MLA_Q_RANK, MLA_HEADS, dq)
    rope = wq[..., MLA_NOPE:]
    rope_rot = jnp.concatenate([-rope[..., MLA_ROPE // 2:], rope[..., :MLA_ROPE // 2]], axis=-1)
    wq_rot = jnp.concatenate([jnp.zeros_like(wq[..., :MLA_NOPE]), rope_rot], axis=-1)
    wq_pad = _pad_heads(wq.reshape(MLA_Q_RANK, -1), MLA_HEADS, dq)
    wq_rot_pad = _pad_heads(wq_rot.reshape(MLA_Q_RANK, -1), MLA_HEADS, dq)
    w_q = jnp.concatenate([wq_pad, wq_rot_pad], axis=1)
    wkv = w_ukv.reshape(MLA_KV_RANK, MLA_HEADS, MLA_NOPE + MLA_V)
    wk_pad = _pad_heads(wkv[..., :MLA_NOPE].reshape(MLA_KV_RANK, -1), MLA_HEADS, MLA_NOPE)
    wv = wkv[..., MLA_NOPE:].reshape(MLA_KV_RANK, MLA_HEADS * MLA_V)
    w_kv = jnp.concatenate([wk_pad, wv], axis=1)
    return w_big.astype(BF16), w_q.astype(BF16), w_kv.astype(BF16)


def _permute_swa_rows(w_out):
    n = MLA_HEADS * MLA_V
    wb = w_out[n:].reshape(SWA_HEADS, HEAD_DIM, D_MODEL)[jnp.array(SWA_HEAD_ORDER)]
    return jnp.concatenate([w_out[:n], wb.reshape(SWA_HEADS * HEAD_DIM, D_MODEL)], axis=0)


def _prep_cd(w_in):
    n = DIL_HEADS * HEAD_DIM
    q_c, k_c, v_c, q_d, k_d, v_d = [w_in[:, i * n:(i + 1) * n] for i in range(6)]
    w_big = jnp.concatenate([q_c, _rot_cols(q_c, DIL_HEADS, HEAD_DIM), k_c, _rot_cols(k_c, DIL_HEADS, HEAD_DIM),
                             v_c, q_d, k_d, v_d], axis=1)
    return w_big.astype(BF16)


def _pre_ab_kernel(x_ref, g_ref, wbig_ref, qn_ref, wq_ref, kvn_ref, wkv_ref, ca_ref, sa_ref, cb_ref, sb_ref,
                   qm_ref, km_ref, vm_ref, qs_ref, ks_ref, vs_ref):
    h = _rms(x_ref[...], g_ref[...]).astype(BF16)
    proj = _dot(h, wbig_ref[...])
    ca, sa, cb, sb = ca_ref[...], sa_ref[...], cb_ref[...], sb_ref[...]
    o = 0
    c_q = proj[:, o:o + MLA_Q_RANK]
    o += MLA_Q_RANK
    c_kv = proj[:, o:o + MLA_KV_RANK]
    o += MLA_KV_RANK
    nq = SWA_HEADS * HEAD_DIM
    q_b, q_b_rot = proj[:, o:o + nq], proj[:, o + nq:o + 2 * nq]
    o += 2 * nq
    k_b, k_b_rot, v_b = proj[:, o:o + LANES], proj[:, o + LANES:o + 2 * LANES], proj[:, o + 2 * LANES:o + 3 * LANES]
    o += 3 * LANES
    k_r, k_r_rot = proj[:, o:o + LANES], proj[:, o + LANES:o + 2 * LANES]

    c_swa = HEAD_DIM ** -0.5 * LOG2E
    qs_ref[...] = jnp.concatenate(
        [(q_b[:, i * LANES:(i + 1) * LANES] * cb + q_b_rot[:, i * LANES:(i + 1) * LANES] * sb) * c_swa
         for i in range(nq // LANES)], axis=1).astype(BF16)
    ks_ref[...] = (k_b * cb + k_b_rot * sb).astype(BF16)
    vs_ref[...] = v_b.astype(BF16)

    q_all = _dot(_rms(c_q, qn_ref[...]).astype(BF16), wq_ref[...])
    nh = MLA_HEADS * LANES
    c_mla = (MLA_NOPE + MLA_ROPE) ** -0.5 * LOG2E
    qm_ref[...] = jnp.concatenate(
        [(q_all[:, i * LANES:(i + 1) * LANES] * ca + q_all[:, nh + i * LANES:nh + (i + 1) * LANES] * sa) * c_mla
         for i in range(MLA_HEADS)], axis=1).astype(BF16)
    kv = _dot(_rms(c_kv, kvn_ref[...]).astype(BF16), wkv_ref[...])
    k_rope = k_r * ca + k_r_rot * sa
    km_ref[...] = jnp.concatenate(
        [kv[:, i * LANES:(i + 1) * LANES] + k_rope for i in range(MLA_HEADS)], axis=1).astype(BF16)
    vm_ref[...] = kv[:, nh:].astype(BF16)


def _pre_ab(x, g, w_big, qn, w_q, kvn, w_kv, tabs, tm=512):
    t = x.shape[0]
    nsb = SEQ // tm
    full = lambda a: pl.BlockSpec(a.shape, lambda i: (0,) * a.ndim)
    tab = pl.BlockSpec((tm, LANES), lambda i: (i % nsb, 0))
    row = lambda n: pl.BlockSpec((tm, n), lambda i: (i, 0))
    outs = [(MLA_HEADS * LANES, BF16), (MLA_HEADS * LANES, BF16), (MLA_HEADS * MLA_V, BF16),
            (SWA_HEADS * HEAD_DIM, BF16), (LANES, BF16), (LANES, BF16)]
    return pl.pallas_call(
        _pre_ab_kernel,
        grid=(t // tm,),
        in_specs=[row(D_MODEL), full(g), full(w_big), full(qn), full(w_q), full(kvn), full(w_kv), tab, tab, tab, tab],
        out_specs=[row(n) for n, _ in outs],
        out_shape=[jax.ShapeDtypeStruct((t, n), d) for n, d in outs],
        compiler_params=_cparams("parallel"),
        name="pre_ab",
    )(x, g, w_big, qn, w_q, kvn, w_kv, *tabs)


def _pre_cd_kernel(x_ref, g_ref, wbig_ref, cb_ref, sb_ref, qc_ref, kc_ref, vc_ref, qd_ref, kd_ref, vd_ref):
    h = _rms(x_ref[...], g_ref[...]).astype(BF16)
    cb, sb = cb_ref[...], sb_ref[...]
    n = DIL_HEADS * HEAD_DIM
    w = wbig_ref

    c_att = HEAD_DIM ** -0.5 * LOG2E

    def roped(o, c):
        a = _dot(h, w[:, o:o + n])
        b = _dot(h, w[:, o + n:o + 2 * n])
        return jnp.concatenate([(a[:, i * LANES:(i + 1) * LANES] * cb + b[:, i * LANES:(i + 1) * LANES] * sb) * c
                                for i in range(n // LANES)], axis=1).astype(BF16)

    qc_ref[...] = roped(0, c_att)
    kc_ref[...] = roped(2 * n, 1.0)
    vc_ref[...] = _dot(h, w[:, 4 * n:5 * n]).astype(BF16)
    qd_ref[...] = (_dot(h, w[:, 5 * n:6 * n]) * c_att).astype(BF16)
    kd_ref[...] = _dot(h, w[:, 6 * n:7 * n]).astype(BF16)
    vd_ref[...] = _dot(h, w[:, 7 * n:8 * n]).astype(BF16)


def _pre_cd(x, g, w_big, tabs, tm=512):
    t = x.shape[0]
    nsb = SEQ // tm
    n = DIL_HEADS * HEAD_DIM
    full = lambda a: pl.BlockSpec(a.shape, lambda i: (0,) * a.ndim)
    tab = pl.BlockSpec((tm, LANES), lambda i: (i % nsb, 0))
    row = lambda m: pl.BlockSpec((tm, m), lambda i: (i, 0))
    return pl.pallas_call(
        _pre_cd_kernel,
        grid=(t // tm,),
        in_specs=[row(D_MODEL), full(g), full(w_big), tab, tab],
        out_specs=[row(n)] * 6,
        out_shape=[jax.ShapeDtypeStruct((t, n), BF16)] * 6,
        compiler_params=_cparams("parallel"),
        name="pre_cd",
    )(x, g, w_big, tabs[2], tabs[3])


def _low_half():
    return lax.broadcasted_iota(jnp.int32, (1, LANES), 1) < LANES // 2


def _split_heads(x2, low):
    zero = jnp.zeros_like(x2)
    return jnp.where(low, x2, zero), jnp.where(low, zero, x2)


def _softmax_pv(s, v2):
    m = jnp.max(s, axis=-1, keepdims=True)
    e = jnp.exp2(s - m)
    l = jnp.sum(e, axis=-1, keepdims=True)
    return _dot(e.astype(BF16), v2) / l


def _mla_kernel(q_ref, k_ref, v_ref, o_ref):
    low = _low_half()
    outs = []
    for p in range(MLA_HEADS // 2):
        v2 = v_ref[0, :, p * LANES:(p + 1) * LANES]
        halves = [_softmax_pv(_dot_nt(q_ref[0, :, h * LANES:(h + 1) * LANES], k_ref[0, :, h * LANES:(h + 1) * LANES]), v2)
                  for h in (2 * p, 2 * p + 1)]
        outs.append(jnp.where(low, halves[0], halves[1]))
    o_ref[0] = jnp.concatenate(outs, axis=1).astype(o_ref.dtype)


def _mla_attn(q, k, v, tq=512):
    b = q.shape[0]
    nq, nv = q.shape[-1], v.shape[-1]
    return pl.pallas_call(
        _mla_kernel,
        grid=(b, SEQ // tq),
        in_specs=[pl.BlockSpec((1, tq, nq), lambda i, j: (i, j, 0)),
                  pl.BlockSpec((1, SEQ, nq), lambda i, j: (i, 0, 0)),
                  pl.BlockSpec((1, SEQ, nv), lambda i, j: (i, 0, 0))],
        out_specs=pl.BlockSpec((1, tq, nv), lambda i, j: (i, j, 0)),
        out_shape=jax.ShapeDtypeStruct((b, SEQ, nv), BF16),
        compiler_params=_cparams("parallel", "parallel"),
        name="mla_attn",
    )(q, k, v)


def _swa_kernel(q_ref, kp_ref, kc_ref, kn_ref, vp_ref, vc_ref, vn_ref, sink_ref, o_ref, *, blk):
    i = pl.program_id(1)
    low = _low_half()
    stacked = []
    for r in range(SWA_GROUP):
        stacked += _split_heads(q_ref[0, :, r * LANES:(r + 1) * LANES], low)
    q = jnp.concatenate(stacked, axis=0)
    k = jnp.concatenate([kp_ref[0], kc_ref[0], kn_ref[0]], axis=0)
    v = jnp.concatenate([vp_ref[0], vc_ref[0], vn_ref[0]], axis=0)
    rows = SWA_HEADS * blk
    qpos = i * blk + lax.broadcasted_iota(jnp.int32, (blk, 3 * blk), 0)
    kpos = (i - 1) * blk + lax.broadcasted_iota(jnp.int32, (blk, 3 * blk), 1)
    valid = (jnp.abs(qpos - kpos) <= SWA_HALF_WINDOW) & (kpos >= 0) & (kpos < SEQ)
    mask = jnp.where(valid, 0.0, NEG_INF)
    s = (_dot_nt(q, k).reshape(SWA_HEADS, blk, 3 * blk) + mask[None]).reshape(rows, 3 * blk)
    m = jnp.max(s, axis=-1, keepdims=True)
    e = jnp.exp2(s - m)
    l = jnp.sum(e, axis=-1, keepdims=True)
    lse = m * LN2 + jnp.log(l)
    keep = jax.nn.sigmoid(lse - sink_ref[...])
    o = _dot(e.astype(BF16), v) * (keep / l)
    o_ref[0] = jnp.concatenate(
        [jnp.where(low, o[2 * r * blk:(2 * r + 1) * blk], o[(2 * r + 1) * blk:(2 * r + 2) * blk])
         for r in range(SWA_GROUP)], axis=1).astype(o_ref.dtype)


def _swa_attn(q, k, v, sink_rows, blk=SWA_BLOCK):
    b = q.shape[0]
    nb = SEQ // blk
    kv = lambda f: pl.BlockSpec((1, blk, LANES), f)
    prev = lambda i, j: (i, jnp.maximum(j - 1, 0), 0)
    cur = lambda i, j: (i, j, 0)
    nxt = lambda i, j: (i, jnp.minimum(j + 1, nb - 1), 0)
    return pl.pallas_call(
        functools.partial(_swa_kernel, blk=blk),
        grid=(b, nb),
        in_specs=[pl.BlockSpec((1, blk, SWA_HEADS * HEAD_DIM), cur), kv(prev), kv(cur), kv(nxt),
                  kv(prev), kv(cur), kv(nxt), pl.BlockSpec(sink_rows.shape, lambda i, j: (0, 0))],
        out_specs=pl.BlockSpec((1, blk, SWA_HEADS * HEAD_DIM), cur),
        out_shape=jax.ShapeDtypeStruct((b, SEQ, SWA_HEADS * HEAD_DIM), BF16),
        compiler_params=_cparams("parallel", "parallel"),
        name="swa_attn",
    )(q, k, k, k, v, v, v, sink_rows)


DIL_BLK = 128
DIL_WIN = 256


def _banded_blocks(q_blocks, k_wins, v_wins, offsets, half, low):
    n = len(q_blocks)
    win = k_wins[0].shape[0]
    q3 = jnp.stack([jnp.concatenate(_split_heads(q, low), axis=0) for q in q_blocks])
    k3, v3 = jnp.stack(k_wins), jnp.stack(v_wins)
    rel = (lax.broadcasted_iota(jnp.int32, (DIL_BLK, win), 0) - lax.broadcasted_iota(jnp.int32, (DIL_BLK, win), 1))
    masks = {off: jnp.where(jnp.abs(rel - off) <= half, 0.0, NEG_INF) for off in set(offsets)}
    mask3 = jnp.stack([masks[off] for off in offsets])
    s = jnp.einsum("bqd,bkd->bqk", q3, k3, preferred_element_type=F32)
    s = (s.reshape(n, 2, DIL_BLK, win) + mask3[:, None]).reshape(n, 2 * DIL_BLK, win)
    m = jnp.max(s, axis=-1, keepdims=True)
    e = jnp.exp2(s - m)
    l = jnp.sum(e, axis=-1, keepdims=True)
    o = jnp.einsum("bqk,bkd->bqd", e.astype(BF16), v3, preferred_element_type=F32)
    pick = lambda t: jnp.where(low, t[:, :DIL_BLK], t[:, DIL_BLK:])
    return pick(o), pick(jnp.broadcast_to(m, o.shape)), pick(jnp.broadcast_to(l, o.shape))


def _dilated_kernel(q_ref, k_ref, v_ref, o_ref, qf, kf, vf, *stats):
    low = _low_half()
    qf[...] = q_ref[0].astype(F32)
    kf[...] = k_ref[0].astype(F32)
    vf[...] = v_ref[0].astype(F32)
    for n, (window, dil) in enumerate(DIL_BRANCHES):
        length = SEQ // dil
        half = window // (2 * dil)
        win = min(DIL_WIN, length)
        per_stream = length // DIL_BLK
        starts = [min(max(a * DIL_BLK - half, 0), length - win) for a in range(per_stream)]
        q_blocks, k_wins, v_wins, offsets = [], [], [], []
        for r in range(dil):
            stream = lambda ref: ref[pl.ds(r, length, stride=dil), :].astype(BF16)
            qs, ks, vs = stream(qf), stream(kf), stream(vf)
            for a, w0 in enumerate(starts):
                q_blocks.append(qs[a * DIL_BLK:(a + 1) * DIL_BLK])
                k_wins.append(ks[w0:w0 + win])
                v_wins.append(vs[w0:w0 + win])
                offsets.append(w0 - a * DIL_BLK)
        tiles = _banded_blocks(q_blocks, k_wins, v_wins, offsets, half, low)
        for tile, dst in zip(tiles, stats[3 * n:3 * n + 3]):
            for r in range(dil):
                dst[pl.ds(r, length, stride=dil), :] = tile[r * per_stream:(r + 1) * per_stream].reshape(length, LANES)
    accs, ms, ls = stats[0::3], stats[1::3], stats[2::3]
    m_all = functools.reduce(jnp.maximum, [m[...] for m in ms])
    ws = [jnp.exp2(m[...] - m_all) for m in ms]
    num = sum(w * a[...] for w, a in zip(ws, accs))
    den = sum(w * l[...] for w, l in zip(ws, ls))
    o_ref[0] = (num / den).astype(o_ref.dtype)


def _dilated_attn(q, k, v):
    b = q.shape[0]
    spec = pl.BlockSpec((1, SEQ, LANES), lambda i, p: (i, 0, p))
    return pl.pallas_call(
        _dilated_kernel,
        grid=(b, DIL_HEADS // 2),
        in_specs=[spec, spec, spec],
        out_specs=spec,
        out_shape=jax.ShapeDtypeStruct((b, SEQ, DIL_HEADS * HEAD_DIM), BF16),
        scratch_shapes=[pltpu.VMEM((SEQ, LANES), F32)] * (3 + 3 * len(DIL_BRANCHES)),
        compiler_params=_cparams("parallel", "parallel"),
        name="dilated_attn",
    )(q, k, v)


NA_Q_ROWS = 4
NA_Q = NA_Q_ROWS * GRID_W
NA_K_ROWS = 12
NA_K = NA_K_ROWS * GRID_W
NA_GROUPS = SEQ // NA_Q
NA_RPB_ROWS = 2 * NA_KH - 1
NA_RPB_COLS = 2 * NA_KW - 1
NA_WIN0 = (lambda a: 0, lambda a: a, lambda a: NA_Q_ROWS)
NA_RHO0 = (NA_KH - 1, NA_KH - 1 - NA_Q_ROWS, NA_KH - 1 - 2 * NA_Q_ROWS)


def _na_bias_kernel(rpb_ref, o_ref):
    pair = pl.program_id(0)
    c = lax.broadcasted_iota(jnp.int32, (GRID_W, LANES), 0)
    lane = lax.broadcasted_iota(jnp.int32, (GRID_W, LANES), 1)
    odd = lane >= GRID_W
    cp = lane & (GRID_W - 1)
    c0 = jnp.clip(c - NA_KW // 2, 0, GRID_W - NA_KW)
    col_ok = (cp >= c0) & (cp < c0 + NA_KW)
    d = cp - c + (NA_KW - 1)
    neg = jnp.full((GRID_W, LANES), NEG_INF, F32)

    for a in range(2):
        head = 2 * pair + a
        cache = {}

        def tile(rho_even, ok_even, ok_odd):
            key = (rho_even, ok_even, ok_odd)
            if key not in cache:
                acc = neg
                for kk in range(NA_RPB_COLS):
                    lo = rpb_ref[(head * NA_RPB_ROWS + rho_even) * NA_RPB_COLS + kk] if ok_even else NEG_INF
                    hi = rpb_ref[(head * NA_RPB_ROWS + rho_even + 1) * NA_RPB_COLS + kk] if ok_odd else NEG_INF
                    acc = jnp.where(d == kk, jnp.where(odd, hi, lo), acc)
                cache[key] = jnp.where(col_ok, acc * LOG2E, NEG_INF)
            return cache[key]

        for pat in range(3):
            for qa in range(NA_Q_ROWS):
                w0 = NA_WIN0[pat](qa)
                tiles = []
                for j in range(0, NA_K_ROWS, 2):
                    ok_even = w0 <= j < w0 + NA_KH
                    ok_odd = w0 <= j + 1 < w0 + NA_KH
                    rho = j - qa + NA_RHO0[pat]
                    tiles.append(tile(rho, ok_even, ok_odd) if (ok_even or ok_odd) else neg)
                o_ref[pat, 0, a, qa * GRID_W:(qa + 1) * GRID_W, :] = jnp.concatenate(tiles, axis=1)


def _na_bias(rpb):
    return pl.pallas_call(
        _na_bias_kernel,
        grid=(NA_HEADS // 2,),
        in_specs=[pl.BlockSpec(memory_space=pltpu.SMEM)],
        out_specs=pl.BlockSpec((3, 1, 2, NA_Q, NA_K), lambda p: (0, p, 0, 0, 0)),
        out_shape=jax.ShapeDtypeStruct((3, NA_HEADS // 2, 2, NA_Q, NA_K), F32),
        compiler_params=_cparams("parallel"),
        name="na_bias",
    )(rpb.reshape(-1))


def _na_kernel(q_ref, k0_ref, k1_ref, k2_ref, v0_ref, v1_ref, v2_ref, bias_ref, o_ref):
    low = _low_half()
    k = jnp.concatenate([k0_ref[0], k1_ref[0], k2_ref[0]], axis=0)
    v = jnp.concatenate([v0_ref[0], v1_ref[0], v2_ref[0]], axis=0)
    halves = [_softmax_pv(_dot_nt(q, k) + bias_ref[0, 0, a], v)
              for a, q in enumerate(_split_heads(q_ref[0], low))]
    o_ref[0] = jnp.where(low, halves[0], halves[1]).astype(o_ref.dtype)


def _na_attn(q, k, v, bias):
    b = q.shape[0]
    first_block = lambda g: jnp.clip(g - 1, 0, NA_GROUPS - NA_K // NA_Q)
    pattern = lambda g: jnp.where(g == 0, 0, jnp.where(g == NA_GROUPS - 1, 2, 1))
    cur = pl.BlockSpec((1, NA_Q, LANES), lambda p, g, i: (i, g, p))
    kv = lambda n: pl.BlockSpec((1, NA_Q, LANES), lambda p, g, i: (i, first_block(g) + n, p))
    return pl.pallas_call(
        _na_kernel,
        grid=(NA_HEADS // 2, NA_GROUPS, b),
        in_specs=[cur, kv(0), kv(1), kv(2), kv(0), kv(1), kv(2),
                  pl.BlockSpec((1, 1, 2, NA_Q, NA_K), lambda p, g, i: (pattern(g), p, 0, 0, 0))],
        out_specs=cur,
        out_shape=jax.ShapeDtypeStruct((b, SEQ, NA_HEADS * HEAD_DIM), BF16),
        compiler_params=_cparams("parallel", "parallel", "parallel"),
        name="na_attn",
    )(q, k, k, k, v, v, v, bias)


def _post_kernel(oa_ref, ob_ref, x_ref, w1_ref, w2_ref, g_ref, wr_ref, xh_ref, aff_ref):
    x = x_ref[...] + _dot(oa_ref[...], w1_ref[...]) + _dot(ob_ref[...], w2_ref[...])
    h = _rms(x, g_ref[...])
    xh_ref[:, :D_MODEL] = x
    xh_ref[:, D_MODEL:] = h
    logits = _dot_nt(wr_ref[...], h.astype(BF16))
    e = jnp.exp(logits - jnp.max(logits, axis=0, keepdims=True))
    aff_ref[...] = e / jnp.sum(e, axis=0, keepdims=True)


def _post(oa, ob, x, w_out, g, w_router, tm=512):
    t = x.shape[0]
    half = oa.shape[1]
    w1, w2 = w_out[:half].astype(BF16), w_out[half:].astype(BF16)
    wr = w_router.T.astype(BF16)
    full = lambda a: pl.BlockSpec(a.shape, lambda i: (0,) * a.ndim)
    row = lambda n: pl.BlockSpec((tm, n), lambda i: (i, 0))
    return pl.pallas_call(
        _post_kernel,
        grid=(t // tm,),
        in_specs=[row(half), row(half), row(D_MODEL), full(w1), full(w2), full(g), full(wr)],
        out_specs=[row(2 * D_MODEL), pl.BlockSpec((N_EXPERTS, tm), lambda i: (0, i))],
        out_shape=[jax.ShapeDtypeStruct((t, 2 * D_MODEL), F32), jax.ShapeDtypeStruct((N_EXPERTS, t), F32)],
        compiler_params=_cparams("parallel"),
        name="post_attn",
    )(oa, ob, x, w1, w2, g, wr)


def _select_kernel(aff_ref, idx_ref, gate_ref, incl_s, a_hi, a_mid, a_lo, start_s, end_s, cnt_s, *, cap, tc):
    s_id = pl.program_id(1)
    rows = aff_ref.shape[1]

    @pl.when(s_id == 0)
    def _():
        aff = aff_ref[0]
        bits = pltpu.bitcast(aff, jnp.int32)

        def search(i, prefix):
            cand = prefix | jnp.left_shift(jnp.int32(1), 30 - i)
            n = jnp.sum((bits >= cand).astype(F32), keepdims=True)
            return jnp.where(n >= cap, cand, prefix)

        thr = lax.fori_loop(0, 31, search, jnp.zeros((1, 1), jnp.int32))
        gt = bits > thr
        eq = bits == thr
        need = cap - jnp.sum(gt.astype(F32), keepdims=True)
        li = lax.broadcasted_iota(jnp.int32, (LANES, LANES), 0)
        lj = lax.broadcasted_iota(jnp.int32, (LANES, LANES), 1)
        tri = (li <= lj).astype(BF16)
        ones = jnp.ones((LANES, LANES), BF16)
        ri = lax.broadcasted_iota(jnp.int32, (rows, rows), 0)
        rj = lax.broadcasted_iota(jnp.int32, (rows, rows), 1)
        below = (rj < ri).astype(BF16)
        above = (ri < rj).astype(BF16)

        eqb = eq.astype(BF16)
        eq_rank = _dot(below, _dot(eqb, ones).astype(BF16)) + _dot(eqb, tri) - eqb.astype(F32)
        sel = (gt | (eq & (eq_rank < need))).astype(BF16)
        incl_s[...] = _dot(sel, tri).astype(BF16)
        cnt = _dot_nt(jnp.ones((8, LANES), BF16), sel)
        start = _dot(cnt.astype(BF16), above)
        cnt_s[...] = cnt
        start_s[...] = start
        end_s[...] = start + cnt
        hi = aff.astype(BF16)
        r1 = aff - hi.astype(F32)
        mid = r1.astype(BF16)
        a_hi[...] = hi
        a_mid[...] = mid
        a_lo[...] = (r1 - mid.astype(F32)).astype(BF16)

    c = (s_id * tc + lax.broadcasted_iota(jnp.int32, (tc, rows), 0)).astype(F32)
    start, end, cnt = start_s[0:1, :], end_s[0:1, :], cnt_s[0:1, :]
    before = c >= end
    row_id = jnp.sum(before.astype(F32), axis=1, keepdims=True)
    row_start = jnp.sum(jnp.where(before, cnt, 0.0), axis=1, keepdims=True)
    onehot = ((c >= start) & (c < end)).astype(BF16)
    within = _dot(onehot, incl_s[...])
    k = c[:, 0:1] - row_start
    lane = jnp.sum((within <= k).astype(F32), axis=1, keepdims=True)
    idx_ref[0] = (row_id * LANES + lane).astype(jnp.int32)
    vals = _dot(onehot, a_hi[...]) + _dot(onehot, a_mid[...]) + _dot(onehot, a_lo[...])
    lane_iota = lax.broadcasted_iota(jnp.int32, (tc, LANES), 1).astype(F32)
    gate = jnp.sum(jnp.where(lane_iota == lane, vals, 0.0), axis=1, keepdims=True)
    gate_ref[0] = jnp.broadcast_to(gate, (tc, LANES))


def _select(aff_t, cap, tc=256):
    t = aff_t.shape[1]
    rows = t // LANES
    aff3 = aff_t.reshape(N_EXPERTS, rows, LANES)
    idx, gate = pl.pallas_call(
        functools.partial(_select_kernel, cap=cap, tc=tc),
        grid=(N_EXPERTS, cap // tc),
        in_specs=[pl.BlockSpec((1, rows, LANES), lambda e, s: (e, 0, 0))],
        out_specs=[pl.BlockSpec((1, tc, 1), lambda e, s: (e, s, 0)),
                   pl.BlockSpec((1, tc, LANES), lambda e, s: (e, s, 0))],
        out_shape=[jax.ShapeDtypeStruct((N_EXPERTS, cap, 1), jnp.int32),
                   jax.ShapeDtypeStruct((N_EXPERTS, cap, LANES), F32)],
        scratch_shapes=[pltpu.VMEM((rows, LANES), BF16)] * 4 + [pltpu.VMEM((8, rows), F32)] * 3,
        compiler_params=_cparams("parallel", "arbitrary"),
        name="ec_select",
    )(aff3)
    return idx.reshape(N_EXPERTS * cap), gate.reshape(N_EXPERTS * cap, LANES)


FFN_SUB = 4


def _ffn_kernel(idx_ref, gate_ref, wg_ref, wu_ref, wd_ref, xh_in, xh_hbm, *scratch, tc, n_steps):
    del xh_in
    bufs, (sem_g, sem_o) = scratch[:FFN_SUB], scratch[FFN_SUB:]
    e, s = pl.program_id(0), pl.program_id(1)
    step_base = (e * n_steps + s) * (FFN_SUB * tc)

    def start_gather(base, j):
        for i in range(tc):
            t = idx_ref[base + i]
            pltpu.make_async_copy(xh_hbm.at[pl.ds(t, 1)], bufs[j].at[pl.ds(i, 1)], sem_g.at[j]).start()

    def wait_gather(j):
        pltpu.make_async_copy(xh_hbm.at[pl.ds(0, tc)], bufs[j], sem_g.at[j]).wait()

    def start_scatter(base, j):
        for i in range(tc):
            t = idx_ref[base + i]
            pltpu.make_async_copy(bufs[j].at[pl.ds(i, 1), pl.ds(0, D_MODEL)],
                                  xh_hbm.at[pl.ds(t, 1), pl.ds(0, D_MODEL)], sem_o.at[j]).start()

    def wait_scatter(j):
        pltpu.make_async_copy(bufs[j].at[:, pl.ds(0, D_MODEL)],
                              xh_hbm.at[pl.ds(0, tc), pl.ds(0, D_MODEL)], sem_o.at[j]).wait()

    @pl.when(s == 0)
    def _():
        @pl.when(e > 0)
        def _():
            for j in range(FFN_SUB):
                wait_scatter(j)

        start_gather(step_base, 0)

    wait_gather(0)
    for j in range(FFN_SUB):
        if j + 1 < FFN_SUB:
            @pl.when(s > 0)
            def _():
                wait_scatter(j + 1)

            start_gather(step_base + (j + 1) * tc, j + 1)
        else:
            @pl.when(s + 1 < n_steps)
            def _():
                wait_scatter(0)
                start_gather(step_base + FFN_SUB * tc, 0)
        if j > 0:
            start_scatter(step_base + (j - 1) * tc, j - 1)

        h = bufs[j][:, D_MODEL:].astype(BF16)
        g = _dot(h, wg_ref[0])
        u = _dot(h, wu_ref[0])
        act = (g * jax.nn.sigmoid(g) * u).astype(BF16)
        y = _dot(act, wd_ref[0]) * gate_ref[j * tc:(j + 1) * tc, 0:1]
        bufs[j][:, :D_MODEL] = bufs[j][:, :D_MODEL] + y
        if j + 1 < FFN_SUB:
            wait_gather(j + 1)
    start_scatter(step_base + (FFN_SUB - 1) * tc, FFN_SUB - 1)

    @pl.when((e == N_EXPERTS - 1) & (s == n_steps - 1))
    def _():
        for j in range(FFN_SUB):
            wait_scatter(j)


def _ffn(idx, gate, xh, w_gate, w_up, w_down, cap, tc=256):
    n_steps = cap // (FFN_SUB * tc)
    assert n_steps * FFN_SUB * tc == cap
    wspec = pl.BlockSpec((1, D_MODEL, D_MODEL), lambda e, s, idx_ref: (e, 0, 0))
    any_spec = pl.BlockSpec(memory_space=pl.ANY)
    grid_spec = pltpu.PrefetchScalarGridSpec(
        num_scalar_prefetch=1,
        grid=(N_EXPERTS, n_steps),
        in_specs=[pl.BlockSpec((FFN_SUB * tc, LANES), lambda e, s, idx_ref: (e * n_steps + s, 0)),
                  wspec, wspec, wspec, any_spec],
        out_specs=any_spec,
        scratch_shapes=[pltpu.VMEM((tc, 2 * D_MODEL), F32)] * FFN_SUB
        + [pltpu.SemaphoreType.DMA((FFN_SUB,)), pltpu.SemaphoreType.DMA((FFN_SUB,))],
    )
    return pl.pallas_call(
        functools.partial(_ffn_kernel, tc=tc, n_steps=n_steps),
        grid_spec=grid_spec,
        out_shape=jax.ShapeDtypeStruct(xh.shape, F32),
        input_output_aliases={5: 0},
        compiler_params=_cparams("arbitrary", "arbitrary"),
        name="ec_ffn",
    )(idx, gate, w_gate, w_up, w_down, xh)


def _final_kernel(x_ref, g_ref, o_ref):
    o_ref[...] = _rms(x_ref[...], g_ref[...])


def _final_norm(x, g, tm=1024):
    t = x.shape[0]
    return pl.pallas_call(
        _final_kernel,
        grid=(t // tm,),
        in_specs=[pl.BlockSpec((tm, D_MODEL), lambda i: (i, 0)), pl.BlockSpec(g.shape, lambda i: (0, 0))],
        out_specs=pl.BlockSpec((tm, D_MODEL), lambda i: (i, 0)),
        out_shape=jax.ShapeDtypeStruct((t, D_MODEL), F32),
        compiler_params=_cparams("parallel"),
        name="final_norm",
    )(x, g)


def _moe(xh, aff_t, w_gate, w_up, w_down):
    cap = EC_CAPACITY_FACTOR * xh.shape[0] // N_EXPERTS
    idx, gate = _select(aff_t, cap)
    return _ffn(idx, gate, xh, w_gate, w_up, w_down, cap)


def _trunk(x3, p):
    b = x3.shape[0]
    t = b * SEQ
    x = x3.reshape(t, D_MODEL)
    seq = lambda a: a.reshape(b, SEQ, a.shape[-1])
    flat = lambda a: a.reshape(t, a.shape[-1])

    qm, km, vm, qs, ks, vs = _pre_ab(x, p["g_mix0"], p["w_big_ab"], p["q_norm"], p["w_q"], p["kv_norm"], p["w_kv"],
                                     p["tabs"])
    o_a = _mla_attn(seq(qm), seq(km), seq(vm))
    o_b = _swa_attn(seq(qs), seq(ks), seq(vs), p["sink"])
    xh, aff_t = _post(flat(o_a), flat(o_b), x, p["w_out_ab"], p["g_ffn0"], p["w_router0"])
    x = _moe(xh, aff_t, p["w_gate0"], p["w_up0"], p["w_down0"])

    qc, kc, vc, qd, kd, vd = _pre_cd(x, p["g_mix1"], p["w_big_cd"], p["tabs"])
    o_c = _dilated_attn(seq(qc), seq(kc), seq(vc))
    o_d = _na_attn(seq(qd), seq(kd), seq(vd), p["na_bias"])
    xh, aff_t = _post(flat(o_c), flat(o_d), x, p["w_out_cd"], p["g_ffn1"], p["w_router1"])
    x = _moe(xh, aff_t, p["w_gate1"], p["w_up1"], p["w_down1"])
    return _final_norm(x, p["g_final"]).reshape(b, SEQ, D_MODEL)


def kernel(x_prompt, x_sample, norm_mix, norm_ffn, norm_final, w_in_ab, mla_q_norm, mla_w_uq, mla_kv_norm,
           mla_w_ukv, swa_sink, w_out_ab, w_in_cd, na_rpb, w_out_cd, w_router, w_gate, w_up, w_down):
    w_big_ab, w_q, w_kv = _prep_ab(w_in_ab[0], mla_w_uq[0], mla_w_ukv[0])
    p = {
        "tabs": _rope_tables(),
        "g_mix0": norm_mix[0:1], "g_mix1": norm_mix[1:2], "g_ffn0": norm_ffn[0:1], "g_ffn1": norm_ffn[1:2],
        "g_final": norm_final.reshape(1, D_MODEL),
        "w_big_ab": w_big_ab, "w_q": w_q, "w_kv": w_kv,
        "q_norm": mla_q_norm[0:1], "kv_norm": mla_kv_norm[0:1],
        "sink": jnp.repeat(swa_sink[0][jnp.array(SWA_HEAD_ORDER)], SWA_BLOCK).reshape(SWA_HEADS * SWA_BLOCK, 1),
        "w_out_ab": _permute_swa_rows(w_out_ab[0]), "w_big_cd": _prep_cd(w_in_cd[0]), "w_out_cd": w_out_cd[0],
        "na_bias": _na_bias(na_rpb[0]),
        "w_router0": w_router[0], "w_router1": w_router[1],
    }
    for l in range(2):
        p[f"w_gate{l}"] = w_gate[l].astype(BF16)
        p[f"w_up{l}"] = w_up[l].astype(BF16)
        p[f"w_down{l}"] = w_down[l].astype(BF16)
    return _trunk(x_prompt, p), _trunk(x_sample, p)
```

```python
import functools

import jax
import jax.numpy as jnp
from jax import lax
from jax.experimental import pallas as pl
from jax.experimental.pallas import tpu as pltpu

D_MODEL = 1024
SEQ = 2048
HEAD_DIM = 64
ROPE_THETA = 10000.0
NEG_INF = -1e30
RMS_EPS = 1e-6

MLA_HEADS = 8
MLA_Q_RANK = 384
MLA_KV_RANK = 256
MLA_NOPE = 64
MLA_ROPE = 32
MLA_V = 64

SWA_HEADS = 8
SWA_KV_HEADS = 2
SWA_GROUP = SWA_HEADS // SWA_KV_HEADS
SWA_HALF_WINDOW = 128
SWA_BLOCK = 128

DIL_HEADS = 8
DIL_BRANCHES = ((128, 1), (512, 4), (2048, 16))

NA_HEADS = 8
GRID_W = 64
NA_KH = 8
NA_KW = 16

N_EXPERTS = 16
EC_CAPACITY_FACTOR = 2

SWA_HEAD_ORDER = tuple(g * SWA_GROUP + r for r in range(SWA_GROUP) for g in range(SWA_KV_HEADS))
LOG2E = 1.4426950408889634
LN2 = 0.6931471805599453

LANES = 128
VMEM_LIMIT = 56 * 1024 * 1024

BF16 = jnp.bfloat16
F32 = jnp.float32


def _cparams(*sem):
    return pltpu.CompilerParams(dimension_semantics=sem, vmem_limit_bytes=VMEM_LIMIT)


def _dot(a, b):
    return jnp.dot(a, b, preferred_element_type=F32)


def _dot_nt(a, b):
    return lax.dot_general(a, b, (((1,), (1,)), ((), ())), preferred_element_type=F32)


def _rms(x, g):
    return x * lax.rsqrt(jnp.mean(x * x, axis=-1, keepdims=True) + RMS_EPS) * g


def _rot_cols(w, n_heads, dim):
    k = w.shape[0]
    w = w.reshape(k, n_heads, dim)
    return jnp.concatenate([-w[..., dim // 2:], w[..., :dim // 2]], axis=-1).reshape(k, n_heads * dim)


def _rope_tables():
    def tables(dim):
        inv = 1.0 / (ROPE_THETA ** (jnp.arange(0, dim, 2, dtype=F32) / dim))
        ang = jnp.arange(SEQ, dtype=F32)[:, None] * inv[None, :]
        c, s = jnp.cos(ang), jnp.sin(ang)
        return jnp.concatenate([c, c], axis=-1), jnp.concatenate([s, s], axis=-1)

    c32, s32 = tables(MLA_ROPE)
    one = jnp.ones((SEQ, MLA_NOPE), F32)
    zero = jnp.zeros((SEQ, MLA_NOPE), F32)
    z32 = jnp.zeros((SEQ, LANES - MLA_NOPE - MLA_ROPE), F32)
    cos_a = jnp.concatenate([one, c32, z32], axis=-1)
    sin_a = jnp.concatenate([zero, s32, z32], axis=-1)
    c64, s64 = tables(HEAD_DIM)
    cos_b = jnp.concatenate([c64, c64], axis=-1)
    sin_b = jnp.concatenate([s64, s64], axis=-1)
    return cos_a, sin_a, cos_b, sin_b


def _pad_heads(w, n_heads, dim):
    k = w.shape[0]
    w = w.reshape(k, n_heads, dim)
    return jnp.pad(w, ((0, 0), (0, 0), (0, LANES - dim))).reshape(k, n_heads * LANES)


def _prep_ab(w_in, w_uq, w_ukv):
    cq = w_in[:, :MLA_Q_RANK]
    o = MLA_Q_RANK
    ckv = w_in[:, o:o + MLA_KV_RANK]
    o += MLA_KV_RANK
    kr = w_in[:, o:o + MLA_ROPE]
    o += MLA_ROPE
    qb = w_in[:, o:o + SWA_HEADS * HEAD_DIM]
    qb = qb.reshape(D_MODEL, SWA_HEADS, HEAD_DIM)[:, list(SWA_HEAD_ORDER)].reshape(D_MODEL, SWA_HEADS * HEAD_DIM)
    o += SWA_HEADS * HEAD_DIM
    kb = w_in[:, o:o + SWA_KV_HEADS * HEAD_DIM]
    o += SWA_KV_HEADS * HEAD_DIM
    vb = w_in[:, o:]
    kr_rot = _rot_cols(kr, 1, MLA_ROPE)
    lead = jnp.zeros((D_MODEL, MLA_NOPE), F32)
    tail = jnp.zeros((D_MODEL, LANES - MLA_NOPE - MLA_ROPE), F32)
    kr_pad = jnp.concatenate([lead, kr, tail], axis=1)
    kr_rot_pad = jnp.concatenate([lead, kr_rot, tail], axis=1)
    w_big = jnp.concatenate([cq, ckv, qb, _rot_cols(qb, SWA_HEADS, HEAD_DIM), kb,
                             _rot_cols(kb, SWA_KV_HEADS, HEAD_DIM), vb, kr_pad, kr_rot_pad], axis=1)
    dq = MLA_NOPE + MLA_ROPE
    wq = w_uq.reshape(MLA_Q_RANK, MLA_HEADS, dq)
    rope = wq[..., MLA_NOPE:]
    rope_rot = jnp.concatenate([-rope[..., MLA_ROPE // 2:], rope[..., :MLA_ROPE // 2]], axis=-1)
    wq_rot = jnp.concatenate([jnp.zeros_like(wq[..., :MLA_NOPE]), rope_rot], axis=-1)
    wq_pad = _pad_heads(wq.reshape(MLA_Q_RANK, -1), MLA_HEADS, dq)
    wq_rot_pad = _pad_heads(wq_rot.reshape(MLA_Q_RANK, -1), MLA_HEADS, dq)
    w_q = jnp.concatenate([wq_pad, wq_rot_pad], axis=1)
    wkv = w_ukv.reshape(MLA_KV_RANK, MLA_HEADS, MLA_NOPE + MLA_V)
    wk_pad = _pad_heads(wkv[..., :MLA_NOPE].reshape(MLA_KV_RANK, -1), MLA_HEADS, MLA_NOPE)
    wv = wkv[..., MLA_NOPE:].reshape(MLA_KV_RANK, MLA_HEADS * MLA_V)
    w_kv = jnp.concatenate([wk_pad, wv], axis=1)
    return w_big.astype(BF16), w_q.astype(BF16), w_kv.astype(BF16)


def _permute_swa_rows(w_out):
    n = MLA_HEADS * MLA_V
    wb = w_out[n:].reshape(SWA_HEADS, HEAD_DIM, D_MODEL)[jnp.array(SWA_HEAD_ORDER)]
    return jnp.concatenate([w_out[:n], wb.reshape(SWA_HEADS * HEAD_DIM, D_MODEL)], axis=0)


def _prep_cd(w_in):
    n = DIL_HEADS * HEAD_DIM
    q_c, k_c, v_c, q_d, k_d, v_d = [w_in[:, i * n:(i + 1) * n] for i in range(6)]
    w_big = jnp.concatenate([q_c, _rot_cols(q_c, DIL_HEADS, HEAD_DIM), k_c, _rot_cols(k_c, DIL_HEADS, HEAD_DIM),
                             v_c, q_d, k_d, v_d], axis=1)
    return w_big.astype(BF16)


def _pre_ab_kernel(x_ref, g_ref, wbig_ref, qn_ref, wq_ref, kvn_ref, wkv_ref, ca_ref, sa_ref, cb_ref, sb_ref,
                   qm_ref, km_ref, vm_ref, qs_ref, ks_ref, vs_ref):
    h = _rms(x_ref[...], g_ref[...]).astype(BF16)
    proj = _dot(h, wbig_ref[...])
    ca, sa, cb, sb = ca_ref[...], sa_ref[...], cb_ref[...], sb_ref[...]
    o = 0
    c_q = proj[:, o:o + MLA_Q_RANK]
    o += MLA_Q_RANK
    c_kv = proj[:, o:o + MLA_KV_RANK]
    o += MLA_KV_RANK
    nq = SWA_HEADS * HEAD_DIM
    q_b, q_b_rot = proj[:, o:o + nq], proj[:, o + nq:o + 2 * nq]
    o += 2 * nq
    k_b, k_b_rot, v_b = proj[:, o:o + LANES], proj[:, o + LANES:o + 2 * LANES], proj[:, o + 2 * LANES:o + 3 * LANES]
    o += 3 * LANES
    k_r, k_r_rot = proj[:, o:o + LANES], proj[:, o + LANES:o + 2 * LANES]

    c_swa = HEAD_DIM ** -0.5 * LOG2E
    qs_ref[...] = jnp.concatenate(
        [(q_b[:, i * LANES:(i + 1) * LANES] * cb + q_b_rot[:, i * LANES:(i + 1) * LANES] * sb) * c_swa
         for i in range(nq // LANES)], axis=1).astype(BF16)
    ks_ref[...] = (k_b * cb + k_b_rot * sb).astype(BF16)
    vs_ref[...] = v_b.astype(BF16)

    q_all = _dot(_rms(c_q, qn_ref[...]).astype(BF16), wq_ref[...])
    nh = MLA_HEADS * LANES
    c_mla = (MLA_NOPE + MLA_ROPE) ** -0.5 * LOG2E
    qm_ref[...] = jnp.concatenate(
        [(q_all[:, i * LANES:(i + 1) * LANES] * ca + q_all[:, nh + i * LANES:nh + (i + 1) * LANES] * sa) * c_mla
         for i in range(MLA_HEADS)], axis=1).astype(BF16)
    kv = _dot(_rms(c_kv, kvn_ref[...]).astype(BF16), wkv_ref[...])
    k_rope = k_r * ca + k_r_rot * sa
    km_ref[...] = jnp.concatenate(
        [kv[:, i * LANES:(i + 1) * LANES] + k_rope for i in range(MLA_HEADS)], axis=1).astype(BF16)
    vm_ref[...] = kv[:, nh:].astype(BF16)


def _pre_ab(x, g, w_big, qn, w_q, kvn, w_kv, tabs, tm=512):
    t = x.shape[0]
    nsb = SEQ // tm
    full = lambda a: pl.BlockSpec(a.shape, lambda i: (0,) * a.ndim)
    tab = pl.BlockSpec((tm, LANES), lambda i: (i % nsb, 0))
    row = lambda n: pl.BlockSpec((tm, n), lambda i: (i, 0))
    outs = [(MLA_HEADS * LANES, BF16), (MLA_HEADS * LANES, BF16), (MLA_HEADS * MLA_V, BF16),
            (SWA_HEADS * HEAD_DIM, BF16), (LANES, BF16), (LANES, BF16)]
    return pl.pallas_call(
        _pre_ab_kernel,
        grid=(t // tm,),
        in_specs=[row(D_MODEL), full(g), full(w_big), full(qn), full(w_q), full(kvn), full(w_kv), tab, tab, tab, tab],
        out_specs=[row(n) for n, _ in outs],
        out_shape=[jax.ShapeDtypeStruct((t, n), d) for n, d in outs],
        compiler_params=_cparams("parallel"),
        name="pre_ab",
    )(x, g, w_big, qn, w_q, kvn, w_kv, *tabs)


def _pre_cd_kernel(x_ref, g_ref, wbig_ref, cb_ref, sb_ref, qc_ref, kc_ref, vc_ref, qd_ref, kd_ref, vd_ref):
    h = _rms(x_ref[...], g_ref[...]).astype(BF16)
    cb, sb = cb_ref[...], sb_ref[...]
    n = DIL_HEADS * HEAD_DIM
    w = wbig_ref

    c_att = HEAD_DIM ** -0.5 * LOG2E

    def roped(o, c):
        a = _dot(h, w[:, o:o + n])
        b = _dot(h, w[:, o + n:o + 2 * n])
        return jnp.concatenate([(a[:, i * LANES:(i + 1) * LANES] * cb + b[:, i * LANES:(i + 1) * LANES] * sb) * c
                                for i in range(n // LANES)], axis=1).astype(BF16)

    qc_ref[...] = roped(0, c_att)
    kc_ref[...] = roped(2 * n, 1.0)
    vc_ref[...] = _dot(h, w[:, 4 * n:5 * n]).astype(BF16)
    qd_ref[...] = (_dot(h, w[:, 5 * n:6 * n]) * c_att).astype(BF16)
    kd_ref[...] = _dot(h, w[:, 6 * n:7 * n]).astype(BF16)
    vd_ref[...] = _dot(h, w[:, 7 * n:8 * n]).astype(BF16)


def _pre_cd(x, g, w_big, tabs, tm=512):
    t = x.shape[0]
    nsb = SEQ // tm
    n = DIL_HEADS * HEAD_DIM
    full = lambda a: pl.BlockSpec(a.shape, lambda i: (0,) * a.ndim)
    tab = pl.BlockSpec((tm, LANES), lambda i: (i % nsb, 0))
    row = lambda m: pl.BlockSpec((tm, m), lambda i: (i, 0))
    return pl.pallas_call(
        _pre_cd_kernel,
        grid=(t // tm,),
        in_specs=[row(D_MODEL), full(g), full(w_big), tab, tab],
        out_specs=[row(n)] * 6,
        out_shape=[jax.ShapeDtypeStruct((t, n), BF16)] * 6,
        compiler_params=_cparams("parallel"),
        name="pre_cd",
    )(x, g, w_big, tabs[2], tabs[3])


def _low_half():
    return lax.broadcasted_iota(jnp.int32, (1, LANES), 1) < LANES // 2


def _split_heads(x2, low):
    zero = jnp.zeros_like(x2)
    return jnp.where(low, x2, zero), jnp.where(low, zero, x2)


def _softmax_pv(s, v2):
    m = jnp.max(s, axis=-1, keepdims=True)
    e = jnp.exp2(s - m)
    l = jnp.sum(e, axis=-1, keepdims=True)
    return _dot(e.astype(BF16), v2) / l


def _mla_kernel(q_ref, k_ref, v_ref, o_ref):
    low = _low_half()
    outs = []
    for p in range(MLA_HEADS // 2):
        v2 = v_ref[0, :, p * LANES:(p + 1) * LANES]
        halves = [_softmax_pv(_dot_nt(q_ref[0, :, h * LANES:(h + 1) * LANES], k_ref[0, :, h * LANES:(h + 1) * LANES]), v2)
                  for h in (2 * p, 2 * p + 1)]
        outs.append(jnp.where(low, halves[0], halves[1]))
    o_ref[0] = jnp.concatenate(outs, axis=1).astype(o_ref.dtype)


def _mla_attn(q, k, v, tq=512):
    b = q.shape[0]
    nq, nv = q.shape[-1], v.shape[-1]
    return pl.pallas_call(
        _mla_kernel,
        grid=(b, SEQ // tq),
        in_specs=[pl.BlockSpec((1, tq, nq), lambda i, j: (i, j, 0)),
                  pl.BlockSpec((1, SEQ, nq), lambda i, j: (i, 0, 0)),
                  pl.BlockSpec((1, SEQ, nv), lambda i, j: (i, 0, 0))],
        out_specs=pl.BlockSpec((1, tq, nv), lambda i, j: (i, j, 0)),
        out_shape=jax.ShapeDtypeStruct((b, SEQ, nv), BF16),
        compiler_params=_cparams("parallel", "parallel"),
        name="mla_attn",
    )(q, k, v)


def _swa_kernel(q_ref, kp_ref, kc_ref, kn_ref, vp_ref, vc_ref, vn_ref, sink_ref, o_ref, *, blk):
    i = pl.program_id(1)
    low = _low_half()
    k = jnp.concatenate([kp_ref[0], kc_ref[0], kn_ref[0]], axis=0)
    v = jnp.concatenate([vp_ref[0], vc_ref[0], vn_ref[0]], axis=0)
    qpos = i * blk + lax.broadcasted_iota(jnp.int32, (blk, 3 * blk), 0)
    kpos = (i - 1) * blk + lax.broadcasted_iota(jnp.int32, (blk, 3 * blk), 1)
    valid = (jnp.abs(qpos - kpos) <= SWA_HALF_WINDOW) & (kpos >= 0) & (kpos < SEQ)
    mask = jnp.where(valid, 0.0, NEG_INF)
    tiles = []
    for r in range(SWA_GROUP):
        q = jnp.concatenate(_split_heads(q_ref[0, :, r * LANES:(r + 1) * LANES], low), axis=0)
        s = (_dot_nt(q, k).reshape(2, blk, 3 * blk) + mask[None]).reshape(2 * blk, 3 * blk)
        m = jnp.max(s, axis=-1, keepdims=True)
        e = jnp.exp2(s - m)
        l = jnp.sum(e, axis=-1, keepdims=True)
        lse = m * LN2 + jnp.log(l)
        keep = jax.nn.sigmoid(lse - sink_ref[2 * r * blk:(2 * r + 2) * blk])
        o = _dot(e.astype(BF16), v) * (keep / l)
        tiles.append(jnp.where(low, o[:blk], o[blk:]))
    o_ref[0] = jnp.concatenate(tiles, axis=1).astype(o_ref.dtype)


def _swa_attn(q, k, v, sink_rows, blk=SWA_BLOCK):
    b = q.shape[0]
    nb = SEQ // blk
    kv = lambda f: pl.BlockSpec((1, blk, LANES), f)
    prev = lambda i, j: (i, jnp.maximum(j - 1, 0), 0)
    cur = lambda i, j: (i, j, 0)
    nxt = lambda i, j: (i, jnp.minimum(j + 1, nb - 1), 0)
    return pl.pallas_call(
        functools.partial(_swa_kernel, blk=blk),
        grid=(b, nb),
        in_specs=[pl.BlockSpec((1, blk, SWA_HEADS * HEAD_DIM), cur), kv(prev), kv(cur), kv(nxt),
                  kv(prev), kv(cur), kv(nxt), pl.BlockSpec(sink_rows.shape, lambda i, j: (0, 0))],
        out_specs=pl.BlockSpec((1, blk, SWA_HEADS * HEAD_DIM), cur),
        out_shape=jax.ShapeDtypeStruct((b, SEQ, SWA_HEADS * HEAD_DIM), BF16),
        compiler_params=_cparams("parallel", "parallel"),
        name="swa_attn",
    )(q, k, k, k, v, v, v, sink_rows)


DIL_BLK = 128
DIL_WIN = 256


def _banded_blocks(q_blocks, k_wins, v_wins, offsets, half, low):
    n = len(q_blocks)
    win = k_wins[0].shape[0]
    q3 = jnp.stack([jnp.concatenate(_split_heads(q, low), axis=0) for q in q_blocks])
    k3, v3 = jnp.stack(k_wins), jnp.stack(v_wins)
    rel = (lax.broadcasted_iota(jnp.int32, (DIL_BLK, win), 0) - lax.broadcasted_iota(jnp.int32, (DIL_BLK, win), 1))
    masks = {off: jnp.where(jnp.abs(rel - off) <= half, 0.0, NEG_INF) for off in set(offsets)}
    mask3 = jnp.stack([masks[off] for off in offsets])
    s = jnp.einsum("bqd,bkd->bqk", q3, k3, preferred_element_type=F32)
    s = (s.reshape(n, 2, DIL_BLK, win) + mask3[:, None]).reshape(n, 2 * DIL_BLK, win)
    m = jnp.max(s, axis=-1, keepdims=True)
    e = jnp.exp2(s - m)
    l = jnp.sum(e, axis=-1, keepdims=True)
    o = jnp.einsum("bqk,bkd->bqd", e.astype(BF16), v3, preferred_element_type=F32)
    pick = lambda t: jnp.where(low, t[:, :DIL_BLK], t[:, DIL_BLK:])
    return pick(o), pick(jnp.broadcast_to(m, o.shape)), pick(jnp.broadcast_to(l, o.shape))


def _dilated_kernel(q_ref, k_ref, v_ref, o_ref, qf, kf, vf, *stats):
    low = _low_half()
    qf[...] = q_ref[0].astype(F32)
    kf[...] = k_ref[0].astype(F32)
    vf[...] = v_ref[0].astype(F32)
    for n, (window, dil) in enumerate(DIL_BRANCHES):
        length = SEQ // dil
        half = window // (2 * dil)
        win = min(DIL_WIN, length)
        per_stream = length // DIL_BLK
        starts = [min(max(a * DIL_BLK - half, 0), length - win) for a in range(per_stream)]
        q_blocks, k_wins, v_wins, offsets = [], [], [], []
        for r in range(dil):
            stream = lambda ref: ref[pl.ds(r, length, stride=dil), :].astype(BF16)
            qs, ks, vs = stream(qf), stream(kf), stream(vf)
            for a, w0 in enumerate(starts):
                q_blocks.append(qs[a * DIL_BLK:(a + 1) * DIL_BLK])
                k_wins.append(ks[w0:w0 + win])
                v_wins.append(vs[w0:w0 + win])
                offsets.append(w0 - a * DIL_BLK)
        tiles = _banded_blocks(q_blocks, k_wins, v_wins, offsets, half, low)
        for tile, dst in zip(tiles, stats[3 * n:3 * n + 3]):
            for r in range(dil):
                dst[pl.ds(r, length, stride=dil), :] = tile[r * per_stream:(r + 1) * per_stream].reshape(length, LANES)
    accs, ms, ls = stats[0::3], stats[1::3], stats[2::3]
    m_all = functools.reduce(jnp.maximum, [m[...] for m in ms])
    ws = [jnp.exp2(m[...] - m_all) for m in ms]
    num = sum(w * a[...] for w, a in zip(ws, accs))
    den = sum(w * l[...] for w, l in zip(ws, ls))
    o_ref[0] = (num / den).astype(o_ref.dtype)


def _dilated_attn(q, k, v):
    b = q.shape[0]
    spec = pl.BlockSpec((1, SEQ, LANES), lambda i, p: (i, 0, p))
    return pl.pallas_call(
        _dilated_kernel,
        grid=(b, DIL_HEADS // 2),
        in_specs=[spec, spec, spec],
        out_specs=spec,
        out_shape=jax.ShapeDtypeStruct((b, SEQ, DIL_HEADS * HEAD_DIM), BF16),
        scratch_shapes=[pltpu.VMEM((SEQ, LANES), F32)] * (3 + 3 * len(DIL_BRANCHES)),
        compiler_params=_cparams("parallel", "parallel"),
        name="dilated_attn",
    )(q, k, v)


NA_Q_ROWS = 4
NA_Q = NA_Q_ROWS * GRID_W
NA_K_ROWS = 12
NA_K = NA_K_ROWS * GRID_W
NA_GROUPS = SEQ // NA_Q
NA_RPB_ROWS = 2 * NA_KH - 1
NA_RPB_COLS = 2 * NA_KW - 1
NA_WIN0 = (lambda a: 0, lambda a: a, lambda a: NA_Q_ROWS)
NA_RHO0 = (NA_KH - 1, NA_KH - 1 - NA_Q_ROWS, NA_KH - 1 - 2 * NA_Q_ROWS)


def _na_bias_kernel(rpb_ref, o_ref):
    pair = pl.program_id(0)
    c = lax.broadcasted_iota(jnp.int32, (GRID_W, LANES), 0)
    lane = lax.broadcasted_iota(jnp.int32, (GRID_W, LANES), 1)
    odd = lane >= GRID_W
    cp = lane & (GRID_W - 1)
    c0 = jnp.clip(c - NA_KW // 2, 0, GRID_W - NA_KW)
    col_ok = (cp >= c0) & (cp < c0 + NA_KW)
    d = cp - c + (NA_KW - 1)
    neg = jnp.full((GRID_W, LANES), NEG_INF, F32)

    for a in range(2):
        head = 2 * pair + a
        cache = {}

        def tile(rho_even, ok_even, ok_odd):
            key = (rho_even, ok_even, ok_odd)
            if key not in cache:
                acc = neg
                for kk in range(NA_RPB_COLS):
                    lo = rpb_ref[(head * NA_RPB_ROWS + rho_even) * NA_RPB_COLS + kk] if ok_even else NEG_INF
                    hi = rpb_ref[(head * NA_RPB_ROWS + rho_even + 1) * NA_RPB_COLS + kk] if ok_odd else NEG_INF
                    acc = jnp.where(d == kk, jnp.where(odd, hi, lo), acc)
                cache[key] = jnp.where(col_ok, acc * LOG2E, NEG_INF)
            return cache[key]

        for pat in range(3):
            for qa in range(NA_Q_ROWS):
                w0 = NA_WIN0[pat](qa)
                tiles = []
                for j in range(0, NA_K_ROWS, 2):
                    ok_even = w0 <= j < w0 + NA_KH
                    ok_odd = w0 <= j + 1 < w0 + NA_KH
                    rho = j - qa + NA_RHO0[pat]
                    tiles.append(tile(rho, ok_even, ok_odd) if (ok_even or ok_odd) else neg)
                o_ref[pat, 0, a, qa * GRID_W:(qa + 1) * GRID_W, :] = jnp.concatenate(tiles, axis=1)


def _na_bias(rpb):
    return pl.pallas_call(
        _na_bias_kernel,
        grid=(NA_HEADS // 2,),
        in_specs=[pl.BlockSpec(memory_space=pltpu.SMEM)],
        out_specs=pl.BlockSpec((3, 1, 2, NA_Q, NA_K), lambda p: (0, p, 0, 0, 0)),
        out_shape=jax.ShapeDtypeStruct((3, NA_HEADS // 2, 2, NA_Q, NA_K), F32),
        compiler_params=_cparams("parallel"),
        name="na_bias",
    )(rpb.reshape(-1))


def _na_kernel(q_ref, k0_ref, k1_ref, k2_ref, v0_ref, v1_ref, v2_ref, bias_ref, o_ref):
    low = _low_half()
    k = jnp.concatenate([k0_ref[0], k1_ref[0], k2_ref[0]], axis=0)
    v = jnp.concatenate([v0_ref[0], v1_ref[0], v2_ref[0]], axis=0)
    halves = [_softmax_pv(_dot_nt(q, k) + bias_ref[0, 0, a], v)
              for a, q in enumerate(_split_heads(q_ref[0], low))]
    o_ref[0] = jnp.where(low, halves[0], halves[1]).astype(o_ref.dtype)


def _na_attn(q, k, v, bias):
    b = q.shape[0]
    first_block = lambda g: jnp.clip(g - 1, 0, NA_GROUPS - NA_K // NA_Q)
    pattern = lambda g: jnp.where(g == 0, 0, jnp.where(g == NA_GROUPS - 1, 2, 1))
    cur = pl.BlockSpec((1, NA_Q, LANES), lambda p, g, i: (i, g, p))
    kv = lambda n: pl.BlockSpec((1, NA_Q, LANES), lambda p, g, i: (i, first_block(g) + n, p))
    return pl.pallas_call(
        _na_kernel,
        grid=(NA_HEADS // 2, NA_GROUPS, b),
        in_specs=[cur, kv(0), kv(1), kv(2), kv(0), kv(1), kv(2),
                  pl.BlockSpec((1, 1, 2, NA_Q, NA_K), lambda p, g, i: (pattern(g), p, 0, 0, 0))],
        out_specs=cur,
        out_shape=jax.ShapeDtypeStruct((b, SEQ, NA_HEADS * HEAD_DIM), BF16),
        compiler_params=_cparams("parallel", "parallel", "parallel"),
        name="na_attn",
    )(q, k, k, k, v, v, v, bias)


def _post_kernel(oa_ref, ob_ref, x_ref, w1_ref, w2_ref, g_ref, wr_ref, xh_ref, aff_ref):
    x = x_ref[...] + _dot(oa_ref[...], w1_ref[...]) + _dot(ob_ref[...], w2_ref[...])
    h = _rms(x, g_ref[...])
    xh_ref[:, :D_MODEL] = x
    xh_ref[:, D_MODEL:] = h
    logits = _dot_nt(wr_ref[...], h.astype(BF16))
    e = jnp.exp(logits - jnp.max(logits, axis=0, keepdims=True))
    aff_ref[...] = e / jnp.sum(e, axis=0, keepdims=True)


def _post(oa, ob, x, w_out, g, w_router, tm=512):
    t = x.shape[0]
    half = oa.shape[1]
    w1, w2 = w_out[:half].astype(BF16), w_out[half:].astype(BF16)
    wr = w_router.T.astype(BF16)
    full = lambda a: pl.BlockSpec(a.shape, lambda i: (0,) * a.ndim)
    row = lambda n: pl.BlockSpec((tm, n), lambda i: (i, 0))
    return pl.pallas_call(
        _post_kernel,
        grid=(t // tm,),
        in_specs=[row(half), row(half), row(D_MODEL), full(w1), full(w2), full(g), full(wr)],
        out_specs=[row(2 * D_MODEL), pl.BlockSpec((N_EXPERTS, tm), lambda i: (0, i))],
        out_shape=[jax.ShapeDtypeStruct((t, 2 * D_MODEL), F32), jax.ShapeDtypeStruct((N_EXPERTS, t), F32)],
        compiler_params=_cparams("parallel"),
        name="post_attn",
    )(oa, ob, x, w1, w2, g, wr)


def _select_kernel(aff_ref, idx_ref, gate_ref, incl_s, a_hi, a_mid, a_lo, start_s, end_s, cnt_s, *, cap, tc):
    s_id = pl.program_id(1)
    rows = aff_ref.shape[1]

    @pl.when(s_id == 0)
    def _():
        aff = aff_ref[0]
        bits = pltpu.bitcast(aff, jnp.int32)

        def search(i, prefix):
            cand = prefix | jnp.left_shift(jnp.int32(1), 30 - i)
            n = jnp.sum((bits >= cand).astype(F32), keepdims=True)
            return jnp.where(n >= cap, cand, prefix)

        thr = lax.fori_loop(0, 31, search, jnp.zeros((1, 1), jnp.int32))
        gt = bits > thr
        eq = bits == thr
        need = cap - jnp.sum(gt.astype(F32), keepdims=True)
        li = lax.broadcasted_iota(jnp.int32, (LANES, LANES), 0)
        lj = lax.broadcasted_iota(jnp.int32, (LANES, LANES), 1)
        tri = (li <= lj).astype(BF16)
        ones = jnp.ones((LANES, LANES), BF16)
        ri = lax.broadcasted_iota(jnp.int32, (rows, rows), 0)
        rj = lax.broadcasted_iota(jnp.int32, (rows, rows), 1)
        below = (rj < ri).astype(BF16)
        above = (ri < rj).astype(BF16)

        eqb = eq.astype(BF16)
        eq_rank = _dot(below, _dot(eqb, ones).astype(BF16)) + _dot(eqb, tri) - eqb.astype(F32)
        sel = (gt | (eq & (eq_rank < need))).astype(BF16)
        incl_s[...] = _dot(sel, tri).astype(BF16)
        cnt = _dot_nt(jnp.ones((8, LANES), BF16), sel)
        start = _dot(cnt.astype(BF16), above)
        cnt_s[...] = cnt
        start_s[...] = start
        end_s[...] = start + cnt
        hi = aff.astype(BF16)
        r1 = aff - hi.astype(F32)
        mid = r1.astype(BF16)
        a_hi[...] = hi
        a_mid[...] = mid
        a_lo[...] = (r1 - mid.astype(F32)).astype(BF16)

    c = (s_id * tc + lax.broadcasted_iota(jnp.int32, (tc, rows), 0)).astype(F32)
    start, end, cnt = start_s[0:1, :], end_s[0:1, :], cnt_s[0:1, :]
    before = c >= end
    row_id = jnp.sum(before.astype(F32), axis=1, keepdims=True)
    row_start = jnp.sum(jnp.where(before, cnt, 0.0), axis=1, keepdims=True)
    onehot = ((c >= start) & (c < end)).astype(BF16)
    within = _dot(onehot, incl_s[...])
    k = c[:, 0:1] - row_start
    lane = jnp.sum((within <= k).astype(F32), axis=1, keepdims=True)
    idx_ref[0] = (row_id * LANES + lane).astype(jnp.int32)
    vals = _dot(onehot, a_hi[...]) + _dot(onehot, a_mid[...]) + _dot(onehot, a_lo[...])
    lane_iota = lax.broadcasted_iota(jnp.int32, (tc, LANES), 1).astype(F32)
    gate = jnp.sum(jnp.where(lane_iota == lane, vals, 0.0), axis=1, keepdims=True)
    gate_ref[0] = jnp.broadcast_to(gate, (tc, LANES))


def _select(aff_t, cap, tc=512):
    t = aff_t.shape[1]
    rows = t // LANES
    aff3 = aff_t.reshape(N_EXPERTS, rows, LANES)
    idx, gate = pl.pallas_call(
        functools.partial(_select_kernel, cap=cap, tc=tc),
        grid=(N_EXPERTS, cap // tc),
        in_specs=[pl.BlockSpec((1, rows, LANES), lambda e, s: (e, 0, 0))],
        out_specs=[pl.BlockSpec((1, tc, 1), lambda e, s: (e, s, 0)),
                   pl.BlockSpec((1, tc, LANES), lambda e, s: (e, s, 0))],
        out_shape=[jax.ShapeDtypeStruct((N_EXPERTS, cap, 1), jnp.int32),
                   jax.ShapeDtypeStruct((N_EXPERTS, cap, LANES), F32)],
        scratch_shapes=[pltpu.VMEM((rows, LANES), BF16)] * 4 + [pltpu.VMEM((8, rows), F32)] * 3,
        compiler_params=_cparams("parallel", "arbitrary"),
        name="ec_select",
    )(aff3)
    return idx.reshape(N_EXPERTS * cap), gate.reshape(N_EXPERTS * cap, LANES)


FFN_SUB = 4


def _ffn_kernel(idx_ref, gate_ref, wg_ref, wu_ref, wd_ref, xh_in, xh_hbm, *scratch, tc, n_steps):
    del xh_in
    bufs, (sem_g, sem_o) = scratch[:FFN_SUB], scratch[FFN_SUB:]
    e, s = pl.program_id(0), pl.program_id(1)
    step_base = (e * n_steps + s) * (FFN_SUB * tc)

    def start_gather(base, j):
        for i in range(tc):
            t = idx_ref[base + i]
            pltpu.make_async_copy(xh_hbm.at[pl.ds(t, 1)], bufs[j].at[pl.ds(i, 1)], sem_g.at[j]).start(priority=i % 2)

    def wait_gather(j):
        pltpu.make_async_copy(xh_hbm.at[pl.ds(0, tc)], bufs[j], sem_g.at[j]).wait()

    def start_scatter(base, j):
        for i in range(tc):
            t = idx_ref[base + i]
            pltpu.make_async_copy(bufs[j].at[pl.ds(i, 1), pl.ds(0, D_MODEL)],
                                  xh_hbm.at[pl.ds(t, 1), pl.ds(0, D_MODEL)], sem_o.at[j]).start(priority=i % 2)

    def wait_scatter(j):
        pltpu.make_async_copy(bufs[j].at[:, pl.ds(0, D_MODEL)],
                              xh_hbm.at[pl.ds(0, tc), pl.ds(0, D_MODEL)], sem_o.at[j]).wait()

    @pl.when(s == 0)
    def _():
        @pl.when(e > 0)
        def _():
            for j in range(FFN_SUB):
                wait_scatter(j)

        start_gather(step_base, 0)

    wait_gather(0)
    for j in range(FFN_SUB):
        if j + 1 < FFN_SUB:
            @pl.when(s > 0)
            def _():
                wait_scatter(j + 1)

            start_gather(step_base + (j + 1) * tc, j + 1)
        else:
            @pl.when(s + 1 < n_steps)
            def _():
                wait_scatter(0)
                start_gather(step_base + FFN_SUB * tc, 0)
        if j > 0:
            start_scatter(step_base + (j - 1) * tc, j - 1)

        h = bufs[j][:, D_MODEL:].astype(BF16)
        g = _dot(h, wg_ref[0])
        u = _dot(h, wu_ref[0])
        act = (g * jax.nn.sigmoid(g) * u).astype(BF16)
        y = _dot(act, wd_ref[0]) * gate_ref[j * tc:(j + 1) * tc, 0:1]
        bufs[j][:, :D_MODEL] = bufs[j][:, :D_MODEL] + y
        if j + 1 < FFN_SUB:
            wait_gather(j + 1)
    start_scatter(step_base + (FFN_SUB - 1) * tc, FFN_SUB - 1)

    @pl.when((e == N_EXPERTS - 1) & (s == n_steps - 1))
    def _():
        for j in range(FFN_SUB):
            wait_scatter(j)


def _ffn(idx, gate, xh, w_gate, w_up, w_down, cap, tc=256):
    n_steps = cap // (FFN_SUB * tc)
    assert n_steps * FFN_SUB * tc == cap
    wspec = pl.BlockSpec((1, D_MODEL, D_MODEL), lambda e, s, idx_ref: (e, 0, 0))
    any_spec = pl.BlockSpec(memory_space=pl.ANY)
    grid_spec = pltpu.PrefetchScalarGridSpec(
        num_scalar_prefetch=1,
        grid=(N_EXPERTS, n_steps),
        in_specs=[pl.BlockSpec((FFN_SUB * tc, LANES), lambda e, s, idx_ref: (e * n_steps + s, 0)),
                  wspec, wspec, wspec, any_spec],
        out_specs=any_spec,
        scratch_shapes=[pltpu.VMEM((tc, 2 * D_MODEL), F32)] * FFN_SUB
        + [pltpu.SemaphoreType.DMA((FFN_SUB,)), pltpu.SemaphoreType.DMA((FFN_SUB,))],
    )
    return pl.pallas_call(
        functools.partial(_ffn_kernel, tc=tc, n_steps=n_steps),
        grid_spec=grid_spec,
        out_shape=jax.ShapeDtypeStruct(xh.shape, F32),
        input_output_aliases={5: 0},
        compiler_params=_cparams("arbitrary", "arbitrary"),
        name="ec_ffn",
    )(idx, gate, w_gate, w_up, w_down, xh)


def _final_kernel(x_ref, g_ref, o_ref):
    o_ref[...] = _rms(x_ref[...], g_ref[...])


def _final_norm(x, g, tm=1024):
    t = x.shape[0]
    return pl.pallas_call(
        _final_kernel,
        grid=(t // tm,),
        in_specs=[pl.BlockSpec((tm, D_MODEL), lambda i: (i, 0)), pl.BlockSpec(g.shape, lambda i: (0, 0))],
        out_specs=pl.BlockSpec((tm, D_MODEL), lambda i: (i, 0)),
        out_shape=jax.ShapeDtypeStruct((t, D_MODEL), F32),
        compiler_params=_cparams("parallel"),
        name="final_norm",
    )(x, g)


def _moe(xh, aff_t, w_gate, w_up, w_down):
    cap = EC_CAPACITY_FACTOR * xh.shape[0] // N_EXPERTS
    idx, gate = _select(aff_t, cap)
    return _ffn(idx, gate, xh, w_gate, w_up, w_down, cap)


def _trunk(x3, p):
    b = x3.shape[0]
    t = b * SEQ
    x = x3.reshape(t, D_MODEL)
    seq = lambda a: a.reshape(b, SEQ, a.shape[-1])
    flat = lambda a: a.reshape(t, a.shape[-1])

    qm, km, vm, qs, ks, vs = _pre_ab(x, p["g_mix0"], p["w_big_ab"], p["q_norm"], p["w_q"], p["kv_norm"], p["w_kv"],
                                     p["tabs"])
    o_a = _mla_attn(seq(qm), seq(km), seq(vm))
    o_b = _swa_attn(seq(qs), seq(ks), seq(vs), p["sink"])
    xh, aff_t = _post(flat(o_a), flat(o_b), x, p["w_out_ab"], p["g_ffn0"], p["w_router0"])
    x = _moe(xh, aff_t, p["w_gate0"], p["w_up0"], p["w_down0"])

    qc, kc, vc, qd, kd, vd = _pre_cd(x, p["g_mix1"], p["w_big_cd"], p["tabs"])
    o_c = _dilated_attn(seq(qc), seq(kc), seq(vc))
    o_d = _na_attn(seq(qd), seq(kd), seq(vd), p["na_bias"])
    xh, aff_t = _post(flat(o_c), flat(o_d), x, p["w_out_cd"], p["g_ffn1"], p["w_router1"])
    x = _moe(xh, aff_t, p["w_gate1"], p["w_up1"], p["w_down1"])
    return _final_norm(x, p["g_final"]).reshape(b, SEQ, D_MODEL)


def kernel(x_prompt, x_sample, norm_mix, norm_ffn, norm_final, w_in_ab, mla_q_norm, mla_w_uq, mla_kv_norm,
           mla_w_ukv, swa_sink, w_out_ab, w_in_cd, na_rpb, w_out_cd, w_router, w_gate, w_up, w_down):
    w_big_ab, w_q, w_kv = _prep_ab(w_in_ab[0], mla_w_uq[0], mla_w_ukv[0])
    p = {
        "tabs": _rope_tables(),
        "g_mix0": norm_mix[0:1], "g_mix1": norm_mix[1:2], "g_ffn0": norm_ffn[0:1], "g_ffn1": norm_ffn[1:2],
        "g_final": norm_final.reshape(1, D_MODEL),
        "w_big_ab": w_big_ab, "w_q": w_q, "w_kv": w_kv,
        "q_norm": mla_q_norm[0:1], "kv_norm": mla_kv_norm[0:1],
        "sink": jnp.repeat(swa_sink[0][jnp.array(SWA_HEAD_ORDER)], SWA_BLOCK).reshape(SWA_HEADS * SWA_BLOCK, 1),
        "w_out_ab": _permute_swa_rows(w_out_ab[0]), "w_big_cd": _prep_cd(w_in_cd[0]), "w_out_cd": w_out_cd[0],
        "na_bias": _na_bias(na_rpb[0]),
        "w_router0": w_router[0], "w_router1": w_router[1],
    }
    for l in range(2):
        p[f"w_gate{l}"] = w_gate[l].astype(BF16)
        p[f"w_up{l}"] = w_up[l].astype(BF16)
        p[f"w_down{l}"] = w_down[l].astype(BF16)
    return _trunk(x_prompt, p), _trunk(x_sample, p)
```

```python
import functools

import jax
import jax.numpy as jnp
from jax import lax
from jax.experimental import pallas as pl
from jax.experimental.pallas import tpu as pltpu

D_MODEL = 1024
SEQ = 2048
HEAD_DIM = 64
ROPE_THETA = 10000.0
NEG_INF = -1e30
RMS_EPS = 1e-6

MLA_HEADS = 8
MLA_Q_RANK = 384
MLA_KV_RANK = 256
MLA_NOPE = 64
MLA_ROPE = 32
MLA_V = 64

SWA_HEADS = 8
SWA_KV_HEADS = 2
SWA_GROUP = SWA_HEADS // SWA_KV_HEADS
SWA_HALF_WINDOW = 128
SWA_BLOCK = 128

DIL_HEADS = 8
DIL_BRANCHES = ((128, 1), (512, 4), (2048, 16))

NA_HEADS = 8
GRID_W = 64
NA_KH = 8
NA_KW = 16

N_EXPERTS = 16
EC_CAPACITY_FACTOR = 2

SWA_HEAD_ORDER = tuple(g * SWA_GROUP + r for r in range(SWA_GROUP) for g in range(SWA_KV_HEADS))
LOG2E = 1.4426950408889634
LN2 = 0.6931471805599453

LANES = 128
VMEM_LIMIT = 56 * 1024 * 1024

BF16 = jnp.bfloat16
F32 = jnp.float32


def _cparams(*sem):
    return pltpu.CompilerParams(dimension_semantics=sem, vmem_limit_bytes=VMEM_LIMIT)


def _dot(a, b):
    return jnp.dot(a, b, preferred_element_type=F32)


def _dot_nt(a, b):
    return lax.dot_general(a, b, (((1,), (1,)), ((), ())), preferred_element_type=F32)


def _rms(x, g):
    return x * lax.rsqrt(jnp.mean(x * x, axis=-1, keepdims=True) + RMS_EPS) * g


COL_TILES = D_MODEL // LANES
ROW_TILES = 2 * COL_TILES


def _packed_cols(ref, first, n):
    return jnp.concatenate([ref[pl.ds(first + k, n, stride=ROW_TILES), :] for k in range(COL_TILES)], axis=1)


def _store_packed(ref, first, val):
    for k in range(COL_TILES):
        ref[pl.ds(first + k, val.shape[0], stride=ROW_TILES), :] = val[:, k * LANES:(k + 1) * LANES]


def _rot_cols(w, n_heads, dim):
    k = w.shape[0]
    w = w.reshape(k, n_heads, dim)
    return jnp.concatenate([-w[..., dim // 2:], w[..., :dim // 2]], axis=-1).reshape(k, n_heads * dim)


def _rope_tables():
    def tables(dim):
        inv = 1.0 / (ROPE_THETA ** (jnp.arange(0, dim, 2, dtype=F32) / dim))
        ang = jnp.arange(SEQ, dtype=F32)[:, None] * inv[None, :]
        c, s = jnp.cos(ang), jnp.sin(ang)
        return jnp.concatenate([c, c], axis=-1), jnp.concatenate([s, s], axis=-1)

    c32, s32 = tables(MLA_ROPE)
    one = jnp.ones((SEQ, MLA_NOPE), F32)
    zero = jnp.zeros((SEQ, MLA_NOPE), F32)
    z32 = jnp.zeros((SEQ, LANES - MLA_NOPE - MLA_ROPE), F32)
    cos_a = jnp.concatenate([one, c32, z32], axis=-1)
    sin_a = jnp.concatenate([zero, s32, z32], axis=-1)
    c64, s64 = tables(HEAD_DIM)
    cos_b = jnp.concatenate([c64, c64], axis=-1)
    sin_b = jnp.concatenate([s64, s64], axis=-1)
    return cos_a, sin_a, cos_b, sin_b


def _pad_heads(w, n_heads, dim):
    k = w.shape[0]
    w = w.reshape(k, n_heads, dim)
    return jnp.pad(w, ((0, 0), (0, 0), (0, LANES - dim))).reshape(k, n_heads * LANES)


def _prep_ab(w_in, w_uq, w_ukv):
    cq = w_in[:, :MLA_Q_RANK]
    o = MLA_Q_RANK
    ckv = w_in[:, o:o + MLA_KV_RANK]
    o += MLA_KV_RANK
    kr = w_in[:, o:o + MLA_ROPE]
    o += MLA_ROPE
    qb = w_in[:, o:o + SWA_HEADS * HEAD_DIM]
    qb = qb.reshape(D_MODEL, SWA_HEADS, HEAD_DIM)[:, list(SWA_HEAD_ORDER)].reshape(D_MODEL, SWA_HEADS * HEAD_DIM)
    o += SWA_HEADS * HEAD_DIM
    kb = w_in[:, o:o + SWA_KV_HEADS * HEAD_DIM]
    o += SWA_KV_HEADS * HEAD_DIM
    vb = w_in[:, o:]
    kr_rot = _rot_cols(kr, 1, MLA_ROPE)
    lead = jnp.zeros((D_MODEL, MLA_NOPE), F32)
    tail = jnp.zeros((D_MODEL, LANES - MLA_NOPE - MLA_ROPE), F32)
    kr_pad = jnp.concatenate([lead, kr, tail], axis=1)
    kr_rot_pad = jnp.concatenate([lead, kr_rot, tail], axis=1)
    w_big = jnp.concatenate([cq, ckv, qb, _rot_cols(qb, SWA_HEADS, HEAD_DIM), kb,
                             _rot_cols(kb, SWA_KV_HEADS, HEAD_DIM), vb, kr_pad, kr_rot_pad], axis=1)
    dq = MLA_NOPE + MLA_ROPE
    wq = w_uq.reshape(MLA_Q_RANK, MLA_HEADS, dq)
    rope = wq[..., MLA_NOPE:]
    rope_rot = jnp.concatenate([-rope[..., MLA_ROPE // 2:], rope[..., :MLA_ROPE // 2]], axis=-1)
    wq_rot = jnp.concatenate([jnp.zeros_like(wq[..., :MLA_NOPE]), rope_rot], axis=-1)
    wq_pad = _pad_heads(wq.reshape(MLA_Q_RANK, -1), MLA_HEADS, dq)
    wq_rot_pad = _pad_heads(wq_rot.reshape(MLA_Q_RANK, -1), MLA_HEADS, dq)
    w_q = jnp.concatenate([wq_pad, wq_rot_pad], axis=1)
    wkv = w_ukv.reshape(MLA_KV_RANK, MLA_HEADS, MLA_NOPE + MLA_V)
    wk_pad = _pad_heads(wkv[..., :MLA_NOPE].reshape(MLA_KV_RANK, -1), MLA_HEADS, MLA_NOPE)
    wv = wkv[..., MLA_NOPE:].reshape(MLA_KV_RANK, MLA_HEADS * MLA_V)
    w_kv = jnp.concatenate([wk_pad, wv], axis=1)
    return w_big.astype(BF16), w_q.astype(BF16), w_kv.astype(BF16)


def _permute_swa_rows(w_out):
    n = MLA_HEADS * MLA_V
    wb = w_out[n:].reshape(SWA_HEADS, HEAD_DIM, D_MODEL)[jnp.array(SWA_HEAD_ORDER)]
    return jnp.concatenate([w_out[:n], wb.reshape(SWA_HEADS * HEAD_DIM, D_MODEL)], axis=0)


def _prep_cd(w_in):
    n = DIL_HEADS * HEAD_DIM
    q_c, k_c, v_c, q_d, k_d, v_d = [w_in[:, i * n:(i + 1) * n] for i in range(6)]
    w_big = jnp.concatenate([q_c, _rot_cols(q_c, DIL_HEADS, HEAD_DIM), k_c, _rot_cols(k_c, DIL_HEADS, HEAD_DIM),
                             v_c, q_d, k_d, v_d], axis=1)
    return w_big.astype(BF16)


def _pre_ab_kernel(x_ref, g_ref, wbig_ref, qn_ref, wq_ref, kvn_ref, wkv_ref, ca_ref, sa_ref, cb_ref, sb_ref,
                   qm_ref, km_ref, vm_ref, qs_ref, ks_ref, vs_ref):
    h = _rms(x_ref[...], g_ref[...]).astype(BF16)
    proj = _dot(h, wbig_ref[...])
    ca, sa, cb, sb = ca_ref[...], sa_ref[...], cb_ref[...], sb_ref[...]
    o = 0
    c_q = proj[:, o:o + MLA_Q_RANK]
    o += MLA_Q_RANK
    c_kv = proj[:, o:o + MLA_KV_RANK]
    o += MLA_KV_RANK
    nq = SWA_HEADS * HEAD_DIM
    q_b, q_b_rot = proj[:, o:o + nq], proj[:, o + nq:o + 2 * nq]
    o += 2 * nq
    k_b, k_b_rot, v_b = proj[:, o:o + LANES], proj[:, o + LANES:o + 2 * LANES], proj[:, o + 2 * LANES:o + 3 * LANES]
    o += 3 * LANES
    k_r, k_r_rot = proj[:, o:o + LANES], proj[:, o + LANES:o + 2 * LANES]

    c_swa = HEAD_DIM ** -0.5 * LOG2E
    qs_ref[...] = jnp.concatenate(
        [(q_b[:, i * LANES:(i + 1) * LANES] * cb + q_b_rot[:, i * LANES:(i + 1) * LANES] * sb) * c_swa
         for i in range(nq // LANES)], axis=1).astype(BF16)
    ks_ref[...] = (k_b * cb + k_b_rot * sb).astype(BF16)
    vs_ref[...] = v_b.astype(BF16)

    q_all = _dot(_rms(c_q, qn_ref[...]).astype(BF16), wq_ref[...])
    nh = MLA_HEADS * LANES
    c_mla = (MLA_NOPE + MLA_ROPE) ** -0.5 * LOG2E
    qm_ref[...] = jnp.concatenate(
        [(q_all[:, i * LANES:(i + 1) * LANES] * ca + q_all[:, nh + i * LANES:nh + (i + 1) * LANES] * sa) * c_mla
         for i in range(MLA_HEADS)], axis=1).astype(BF16)
    kv = _dot(_rms(c_kv, kvn_ref[...]).astype(BF16), wkv_ref[...])
    k_rope = k_r * ca + k_r_rot * sa
    km_ref[...] = jnp.concatenate(
        [kv[:, i * LANES:(i + 1) * LANES] + k_rope for i in range(MLA_HEADS)], axis=1).astype(BF16)
    vm_ref[...] = kv[:, nh:].astype(BF16)


def _pre_ab(x, g, w_big, qn, w_q, kvn, w_kv, tabs, tm=512):
    t = x.shape[0]
    nsb = SEQ // tm
    full = lambda a: pl.BlockSpec(a.shape, lambda i: (0,) * a.ndim)
    tab = pl.BlockSpec((tm, LANES), lambda i: (i % nsb, 0))
    row = lambda n: pl.BlockSpec((tm, n), lambda i: (i, 0))
    outs = [(MLA_HEADS * LANES, BF16), (MLA_HEADS * LANES, BF16), (MLA_HEADS * MLA_V, BF16),
            (SWA_HEADS * HEAD_DIM, BF16), (LANES, BF16), (LANES, BF16)]
    return pl.pallas_call(
        _pre_ab_kernel,
        grid=(t // tm,),
        in_specs=[row(D_MODEL), full(g), full(w_big), full(qn), full(w_q), full(kvn), full(w_kv), tab, tab, tab, tab],
        out_specs=[row(n) for n, _ in outs],
        out_shape=[jax.ShapeDtypeStruct((t, n), d) for n, d in outs],
        compiler_params=_cparams("parallel"),
        name="pre_ab",
    )(x, g, w_big, qn, w_q, kvn, w_kv, *tabs)


def _pre_cd_kernel(x_ref, g_ref, wbig_ref, cb_ref, sb_ref, qc_ref, kc_ref, vc_ref, qd_ref, kd_ref, vd_ref):
    h = _rms(_packed_cols(x_ref, 0, qc_ref.shape[0]), g_ref[...]).astype(BF16)
    cb, sb = cb_ref[...], sb_ref[...]
    n = DIL_HEADS * HEAD_DIM
    w = wbig_ref

    c_att = HEAD_DIM ** -0.5 * LOG2E

    def roped(o, c):
        a = _dot(h, w[:, o:o + n])
        b = _dot(h, w[:, o + n:o + 2 * n])
        return jnp.concatenate([(a[:, i * LANES:(i + 1) * LANES] * cb + b[:, i * LANES:(i + 1) * LANES] * sb) * c
                                for i in range(n // LANES)], axis=1).astype(BF16)

    qc_ref[...] = roped(0, c_att)
    kc_ref[...] = roped(2 * n, 1.0)
    vc_ref[...] = _dot(h, w[:, 4 * n:5 * n]).astype(BF16)
    qd_ref[...] = (_dot(h, w[:, 5 * n:6 * n]) * c_att).astype(BF16)
    kd_ref[...] = _dot(h, w[:, 6 * n:7 * n]).astype(BF16)
    vd_ref[...] = _dot(h, w[:, 7 * n:8 * n]).astype(BF16)


def _pre_cd(x, g, w_big, tabs, tm=512):
    t = x.shape[0] // ROW_TILES
    nsb = SEQ // tm
    n = DIL_HEADS * HEAD_DIM
    full = lambda a: pl.BlockSpec(a.shape, lambda i: (0,) * a.ndim)
    tab = pl.BlockSpec((tm, LANES), lambda i: (i % nsb, 0))
    row = lambda m: pl.BlockSpec((tm, m), lambda i: (i, 0))
    return pl.pallas_call(
        _pre_cd_kernel,
        grid=(t // tm,),
        in_specs=[pl.BlockSpec((tm * ROW_TILES, LANES), lambda i: (i, 0)), full(g), full(w_big), tab, tab],
        out_specs=[row(n)] * 6,
        out_shape=[jax.ShapeDtypeStruct((t, n), BF16)] * 6,
        compiler_params=_cparams("parallel"),
        name="pre_cd",
    )(x, g, w_big, tabs[2], tabs[3])


def _low_half():
    return lax.broadcasted_iota(jnp.int32, (1, LANES), 1) < LANES // 2


def _split_heads(x2, low):
    zero = jnp.zeros_like(x2)
    return jnp.where(low, x2, zero), jnp.where(low, zero, x2)


def _softmax_pv(s, v2):
    m = jnp.max(s, axis=-1, keepdims=True)
    e = jnp.exp2(s - m)
    l = jnp.sum(e, axis=-1, keepdims=True)
    return _dot(e.astype(BF16), v2) / l


def _mla_kernel(q_ref, k_ref, v_ref, o_ref):
    low = _low_half()
    outs = []
    for p in range(MLA_HEADS // 2):
        v2 = v_ref[0, :, p * LANES:(p + 1) * LANES]
        halves = [_softmax_pv(_dot_nt(q_ref[0, :, h * LANES:(h + 1) * LANES], k_ref[0, :, h * LANES:(h + 1) * LANES]), v2)
                  for h in (2 * p, 2 * p + 1)]
        outs.append(jnp.where(low, halves[0], halves[1]))
    o_ref[0] = jnp.concatenate(outs, axis=1).astype(o_ref.dtype)


def _mla_attn(q, k, v, tq=512):
    b = q.shape[0]
    nq, nv = q.shape[-1], v.shape[-1]
    return pl.pallas_call(
        _mla_kernel,
        grid=(b, SEQ // tq),
        in_specs=[pl.BlockSpec((1, tq, nq), lambda i, j: (i, j, 0)),
                  pl.BlockSpec((1, SEQ, nq), lambda i, j: (i, 0, 0)),
                  pl.BlockSpec((1, SEQ, nv), lambda i, j: (i, 0, 0))],
        out_specs=pl.BlockSpec((1, tq, nv), lambda i, j: (i, j, 0)),
        out_shape=jax.ShapeDtypeStruct((b, SEQ, nv), BF16),
        compiler_params=_cparams("parallel", "parallel"),
        name="mla_attn",
    )(q, k, v)


def _swa_kernel(q_ref, kp_ref, kc_ref, kn_ref, vp_ref, vc_ref, vn_ref, sink_ref, o_ref, *, blk):
    i = pl.program_id(1)
    low = _low_half()
    k = jnp.concatenate([kp_ref[0], kc_ref[0], kn_ref[0]], axis=0)
    v = jnp.concatenate([vp_ref[0], vc_ref[0], vn_ref[0]], axis=0)
    qpos = i * blk + lax.broadcasted_iota(jnp.int32, (blk, 3 * blk), 0)
    kpos = (i - 1) * blk + lax.broadcasted_iota(jnp.int32, (blk, 3 * blk), 1)
    valid = (jnp.abs(qpos - kpos) <= SWA_HALF_WINDOW) & (kpos >= 0) & (kpos < SEQ)
    mask = jnp.where(valid, 0.0, NEG_INF)
    tiles = []
    for r in range(SWA_GROUP):
        q = jnp.concatenate(_split_heads(q_ref[0, :, r * LANES:(r + 1) * LANES], low), axis=0)
        s = (_dot_nt(q, k).reshape(2, blk, 3 * blk) + mask[None]).reshape(2 * blk, 3 * blk)
        m = jnp.max(s, axis=-1, keepdims=True)
        e = jnp.exp2(s - m)
        l = jnp.sum(e, axis=-1, keepdims=True)
        lse = m * LN2 + jnp.log(l)
        keep = jax.nn.sigmoid(lse - sink_ref[2 * r * blk:(2 * r + 2) * blk])
        o = _dot(e.astype(BF16), v) * (keep / l)
        tiles.append(jnp.where(low, o[:blk], o[blk:]))
    o_ref[0] = jnp.concatenate(tiles, axis=1).astype(o_ref.dtype)


def _swa_attn(q, k, v, sink_rows, blk=SWA_BLOCK):
    b = q.shape[0]
    nb = SEQ // blk
    kv = lambda f: pl.BlockSpec((1, blk, LANES), f)
    prev = lambda i, j: (i, jnp.maximum(j - 1, 0), 0)
    cur = lambda i, j: (i, j, 0)
    nxt = lambda i, j: (i, jnp.minimum(j + 1, nb - 1), 0)
    return pl.pallas_call(
        functools.partial(_swa_kernel, blk=blk),
        grid=(b, nb),
        in_specs=[pl.BlockSpec((1, blk, SWA_HEADS * HEAD_DIM), cur), kv(prev), kv(cur), kv(nxt),
                  kv(prev), kv(cur), kv(nxt), pl.BlockSpec(sink_rows.shape, lambda i, j: (0, 0))],
        out_specs=pl.BlockSpec((1, blk, SWA_HEADS * HEAD_DIM), cur),
        out_shape=jax.ShapeDtypeStruct((b, SEQ, SWA_HEADS * HEAD_DIM), BF16),
        compiler_params=_cparams("parallel", "parallel"),
        name="swa_attn",
    )(q, k, k, k, v, v, v, sink_rows)


DIL_BLK = 128
DIL_WIN = 256


def _banded_blocks(q_blocks, k_wins, v_wins, offsets, half, low):
    n = len(q_blocks)
    win = k_wins[0].shape[0]
    q3 = jnp.stack([jnp.concatenate(_split_heads(q, low), axis=0) for q in q_blocks])
    k3, v3 = jnp.stack(k_wins), jnp.stack(v_wins)
    rel = (lax.broadcasted_iota(jnp.int32, (DIL_BLK, win), 0) - lax.broadcasted_iota(jnp.int32, (DIL_BLK, win), 1))
    masks = {off: jnp.where(jnp.abs(rel - off) <= half, 0.0, NEG_INF) for off in set(offsets)}
    mask3 = jnp.stack([masks[off] for off in offsets])
    s = jnp.einsum("bqd,bkd->bqk", q3, k3, preferred_element_type=F32)
    s = (s.reshape(n, 2, DIL_BLK, win) + mask3[:, None]).reshape(n, 2 * DIL_BLK, win)
    m = jnp.max(s, axis=-1, keepdims=True)
    e = jnp.exp2(s - m)
    l = jnp.sum(e, axis=-1, keepdims=True)
    o = jnp.einsum("bqk,bkd->bqd", e.astype(BF16), v3, preferred_element_type=F32)
    pick = lambda t: jnp.where(low, t[:, :DIL_BLK], t[:, DIL_BLK:])
    return pick(o), pick(jnp.broadcast_to(m, o.shape)), pick(jnp.broadcast_to(l, o.shape))


def _dilated_kernel(q_ref, k_ref, v_ref, o_ref, qf, kf, vf, *stats):
    low = _low_half()
    qf[...] = q_ref[0].astype(F32)
    kf[...] = k_ref[0].astype(F32)
    vf[...] = v_ref[0].astype(F32)
    for n, (window, dil) in enumerate(DIL_BRANCHES):
        length = SEQ // dil
        half = window // (2 * dil)
        win = min(DIL_WIN, length)
        per_stream = length // DIL_BLK
        starts = [min(max(a * DIL_BLK - half, 0), length - win) for a in range(per_stream)]
        q_blocks, k_wins, v_wins, offsets = [], [], [], []
        for r in range(dil):
            stream = lambda ref: ref[pl.ds(r, length, stride=dil), :].astype(BF16)
            qs, ks, vs = stream(qf), stream(kf), stream(vf)
            for a, w0 in enumerate(starts):
                q_blocks.append(qs[a * DIL_BLK:(a + 1) * DIL_BLK])
                k_wins.append(ks[w0:w0 + win])
                v_wins.append(vs[w0:w0 + win])
                offsets.append(w0 - a * DIL_BLK)
        tiles = _banded_blocks(q_blocks, k_wins, v_wins, offsets, half, low)
        for tile, dst in zip(tiles, stats[3 * n:3 * n + 3]):
            for r in range(dil):
                dst[pl.ds(r, length, stride=dil), :] = tile[r * per_stream:(r + 1) * per_stream].reshape(length, LANES)
    accs, ms, ls = stats[0::3], stats[1::3], stats[2::3]
    m_all = functools.reduce(jnp.maximum, [m[...] for m in ms])
    ws = [jnp.exp2(m[...] - m_all) for m in ms]
    num = sum(w * a[...] for w, a in zip(ws, accs))
    den = sum(w * l[...] for w, l in zip(ws, ls))
    o_ref[0] = (num / den).astype(o_ref.dtype)


def _dilated_attn(q, k, v):
    b = q.shape[0]
    spec = pl.BlockSpec((1, SEQ, LANES), lambda i, p: (i, 0, p))
    return pl.pallas_call(
        _dilated_kernel,
        grid=(b, DIL_HEADS // 2),
        in_specs=[spec, spec, spec],
        out_specs=spec,
        out_shape=jax.ShapeDtypeStruct((b, SEQ, DIL_HEADS * HEAD_DIM), BF16),
        scratch_shapes=[pltpu.VMEM((SEQ, LANES), F32)] * (3 + 3 * len(DIL_BRANCHES)),
        compiler_params=_cparams("parallel", "parallel"),
        name="dilated_attn",
    )(q, k, v)


NA_Q_ROWS = 4
NA_Q = NA_Q_ROWS * GRID_W
NA_K_ROWS = 12
NA_K = NA_K_ROWS * GRID_W
NA_GROUPS = SEQ // NA_Q
NA_RPB_ROWS = 2 * NA_KH - 1
NA_RPB_COLS = 2 * NA_KW - 1
NA_WIN0 = (lambda a: 0, lambda a: a, lambda a: NA_Q_ROWS)
NA_RHO0 = (NA_KH - 1, NA_KH - 1 - NA_Q_ROWS, NA_KH - 1 - 2 * NA_Q_ROWS)


def _na_bias_kernel(rpb_ref, o_ref):
    pair = pl.program_id(0)
    c = lax.broadcasted_iota(jnp.int32, (GRID_W, LANES), 0)
    lane = lax.broadcasted_iota(jnp.int32, (GRID_W, LANES), 1)
    odd = lane >= GRID_W
    cp = lane & (GRID_W - 1)
    c0 = jnp.clip(c - NA_KW // 2, 0, GRID_W - NA_KW)
    col_ok = (cp >= c0) & (cp < c0 + NA_KW)
    d = cp - c + (NA_KW - 1)
    neg = jnp.full((GRID_W, LANES), NEG_INF, F32)

    for a in range(2):
        head = 2 * pair + a
        cache = {}

        def tile(rho_even, ok_even, ok_odd):
            key = (rho_even, ok_even, ok_odd)
            if key not in cache:
                acc = neg
                for kk in range(NA_RPB_COLS):
                    lo = rpb_ref[(head * NA_RPB_ROWS + rho_even) * NA_RPB_COLS + kk] if ok_even else NEG_INF
                    hi = rpb_ref[(head * NA_RPB_ROWS + rho_even + 1) * NA_RPB_COLS + kk] if ok_odd else NEG_INF
                    acc = jnp.where(d == kk, jnp.where(odd, hi, lo), acc)
                cache[key] = jnp.where(col_ok, acc * LOG2E, NEG_INF)
            return cache[key]

        for pat in range(3):
            for qa in range(NA_Q_ROWS):
                w0 = NA_WIN0[pat](qa)
                tiles = []
                for j in range(0, NA_K_ROWS, 2):
                    ok_even = w0 <= j < w0 + NA_KH
                    ok_odd = w0 <= j + 1 < w0 + NA_KH
                    rho = j - qa + NA_RHO0[pat]
                    tiles.append(tile(rho, ok_even, ok_odd) if (ok_even or ok_odd) else neg)
                o_ref[pat, 0, a, qa * GRID_W:(qa + 1) * GRID_W, :] = jnp.concatenate(tiles, axis=1)


def _na_bias(rpb):
    return pl.pallas_call(
        _na_bias_kernel,
        grid=(NA_HEADS // 2,),
        in_specs=[pl.BlockSpec(memory_space=pltpu.SMEM)],
        out_specs=pl.BlockSpec((3, 1, 2, NA_Q, NA_K), lambda p: (0, p, 0, 0, 0)),
        out_shape=jax.ShapeDtypeStruct((3, NA_HEADS // 2, 2, NA_Q, NA_K), F32),
        compiler_params=_cparams("parallel"),
        name="na_bias",
    )(rpb.reshape(-1))


def _na_kernel(q_ref, k0_ref, k1_ref, k2_ref, v0_ref, v1_ref, v2_ref, bias_ref, o_ref):
    low = _low_half()
    k = jnp.concatenate([k0_ref[0], k1_ref[0], k2_ref[0]], axis=0)
    v = jnp.concatenate([v0_ref[0], v1_ref[0], v2_ref[0]], axis=0)
    halves = [_softmax_pv(_dot_nt(q, k) + bias_ref[0, 0, a], v)
              for a, q in enumerate(_split_heads(q_ref[0], low))]
    o_ref[0] = jnp.where(low, halves[0], halves[1]).astype(o_ref.dtype)


def _na_attn(q, k, v, bias):
    b = q.shape[0]
    first_block = lambda g: jnp.clip(g - 1, 0, NA_GROUPS - NA_K // NA_Q)
    pattern = lambda g: jnp.where(g == 0, 0, jnp.where(g == NA_GROUPS - 1, 2, 1))
    cur = pl.BlockSpec((1, NA_Q, LANES), lambda p, g, i: (i, g, p))
    kv = lambda n: pl.BlockSpec((1, NA_Q, LANES), lambda p, g, i: (i, first_block(g) + n, p))
    return pl.pallas_call(
        _na_kernel,
        grid=(NA_HEADS // 2, NA_GROUPS, b),
        in_specs=[cur, kv(0), kv(1), kv(2), kv(0), kv(1), kv(2),
                  pl.BlockSpec((1, 1, 2, NA_Q, NA_K), lambda p, g, i: (pattern(g), p, 0, 0, 0))],
        out_specs=cur,
        out_shape=jax.ShapeDtypeStruct((b, SEQ, NA_HEADS * HEAD_DIM), BF16),
        compiler_params=_cparams("parallel", "parallel", "parallel"),
        name="na_attn",
    )(q, k, k, k, v, v, v, bias)


def _post_kernel(oa_ref, ob_ref, x_ref, w1_ref, w2_ref, g_ref, wr_ref, xh_ref, aff_ref, *, x_packed):
    x = _packed_cols(x_ref, 0, oa_ref.shape[0]) if x_packed else x_ref[...]
    x = x + _dot(oa_ref[...], w1_ref[...]) + _dot(ob_ref[...], w2_ref[...])
    h = _rms(x, g_ref[...])
    _store_packed(xh_ref, 0, x)
    _store_packed(xh_ref, COL_TILES, h)
    logits = _dot_nt(wr_ref[...], h.astype(BF16))
    e = jnp.exp(logits - jnp.max(logits, axis=0, keepdims=True))
    aff_ref[...] = e / jnp.sum(e, axis=0, keepdims=True)


def _post(oa, ob, x, w_out, g, w_router, tm=512):
    t = oa.shape[0]
    x_packed = x.shape[0] != t
    half = oa.shape[1]
    w1, w2 = w_out[:half].astype(BF16), w_out[half:].astype(BF16)
    wr = w_router.T.astype(BF16)
    full = lambda a: pl.BlockSpec(a.shape, lambda i: (0,) * a.ndim)
    row = lambda n: pl.BlockSpec((tm, n), lambda i: (i, 0))
    packed = pl.BlockSpec((tm * ROW_TILES, LANES), lambda i: (i, 0))
    return pl.pallas_call(
        functools.partial(_post_kernel, x_packed=x_packed),
        grid=(t // tm,),
        in_specs=[row(half), row(half), packed if x_packed else row(D_MODEL), full(w1), full(w2), full(g), full(wr)],
        out_specs=[packed, pl.BlockSpec((N_EXPERTS, tm), lambda i: (0, i))],
        out_shape=[jax.ShapeDtypeStruct((t * ROW_TILES, LANES), F32), jax.ShapeDtypeStruct((N_EXPERTS, t), F32)],
        compiler_params=_cparams("parallel"),
        name="post_attn",
    )(oa, ob, x, w1, w2, g, wr)


def _select_kernel(aff_ref, idx_ref, gate_ref, incl_s, a_hi, a_mid, a_lo, start_s, end_s, cnt_s, *, cap, tc):
    s_id = pl.program_id(1)
    rows = aff_ref.shape[1]

    @pl.when(s_id == 0)
    def _():
        aff = aff_ref[0]
        bits = pltpu.bitcast(aff, jnp.int32)

        def search(i, prefix):
            cand = prefix | jnp.left_shift(jnp.int32(1), 30 - i)
            n = jnp.sum((bits >= cand).astype(F32), keepdims=True)
            return jnp.where(n >= cap, cand, prefix)

        thr = lax.fori_loop(0, 31, search, jnp.zeros((1, 1), jnp.int32))
        gt = bits > thr
        eq = bits == thr
        need = cap - jnp.sum(gt.astype(F32), keepdims=True)
        li = lax.broadcasted_iota(jnp.int32, (LANES, LANES), 0)
        lj = lax.broadcasted_iota(jnp.int32, (LANES, LANES), 1)
        tri = (li <= lj).astype(BF16)
        ones = jnp.ones((LANES, LANES), BF16)
        ri = lax.broadcasted_iota(jnp.int32, (rows, rows), 0)
        rj = lax.broadcasted_iota(jnp.int32, (rows, rows), 1)
        below = (rj < ri).astype(BF16)
        above = (ri < rj).astype(BF16)

        eqb = eq.astype(BF16)
        eq_rank = _dot(below, _dot(eqb, ones).astype(BF16)) + _dot(eqb, tri) - eqb.astype(F32)
        sel = (gt | (eq & (eq_rank < need))).astype(BF16)
        incl_s[...] = _dot(sel, tri).astype(BF16)
        cnt = _dot_nt(jnp.ones((8, LANES), BF16), sel)
        start = _dot(cnt.astype(BF16), above)
        cnt_s[...] = cnt
        start_s[...] = start
        end_s[...] = start + cnt
        hi = aff.astype(BF16)
        r1 = aff - hi.astype(F32)
        mid = r1.astype(BF16)
        a_hi[...] = hi
        a_mid[...] = mid
        a_lo[...] = (r1 - mid.astype(F32)).astype(BF16)

    c = (s_id * tc + lax.broadcasted_iota(jnp.int32, (tc, rows), 0)).astype(F32)
    start, end, cnt = start_s[0:1, :], end_s[0:1, :], cnt_s[0:1, :]
    before = c >= end
    row_id = jnp.sum(before.astype(F32), axis=1, keepdims=True)
    row_start = jnp.sum(jnp.where(before, cnt, 0.0), axis=1, keepdims=True)
    onehot = ((c >= start) & (c < end)).astype(BF16)
    within = _dot(onehot, incl_s[...])
    k = c[:, 0:1] - row_start
    lane = jnp.sum((within <= k).astype(F32), axis=1, keepdims=True)
    idx_ref[0] = (row_id * LANES + lane).astype(jnp.int32)
    vals = _dot(onehot, a_hi[...]) + _dot(onehot, a_mid[...]) + _dot(onehot, a_lo[...])
    lane_iota = lax.broadcasted_iota(jnp.int32, (tc, LANES), 1).astype(F32)
    gate = jnp.sum(jnp.where(lane_iota == lane, vals, 0.0), axis=1, keepdims=True)
    gate_ref[0] = jnp.broadcast_to(gate, (tc, LANES))


def _select(aff_t, cap, tc=512):
    t = aff_t.shape[1]
    rows = t // LANES
    aff3 = aff_t.reshape(N_EXPERTS, rows, LANES)
    idx, gate = pl.pallas_call(
        functools.partial(_select_kernel, cap=cap, tc=tc),
        grid=(N_EXPERTS, cap // tc),
        in_specs=[pl.BlockSpec((1, rows, LANES), lambda e, s: (e, 0, 0))],
        out_specs=[pl.BlockSpec((1, tc, 1), lambda e, s: (e, s, 0)),
                   pl.BlockSpec((1, tc, LANES), lambda e, s: (e, s, 0))],
        out_shape=[jax.ShapeDtypeStruct((N_EXPERTS, cap, 1), jnp.int32),
                   jax.ShapeDtypeStruct((N_EXPERTS, cap, LANES), F32)],
        scratch_shapes=[pltpu.VMEM((rows, LANES), BF16)] * 4 + [pltpu.VMEM((8, rows), F32)] * 3,
        compiler_params=_cparams("parallel", "arbitrary"),
        name="ec_select",
    )(aff3)
    return idx.reshape(N_EXPERTS * cap), gate.reshape(N_EXPERTS * cap, LANES)


FFN_SUB = 4


def _ffn_kernel(idx_ref, gate_ref, wg_ref, wu_ref, wd_ref, xh_in, xh_hbm, *scratch, tc, n_steps):
    del xh_in
    bufs, (sem_g, sem_o) = scratch[:FFN_SUB], scratch[FFN_SUB:]
    e, s = pl.program_id(0), pl.program_id(1)
    step_base = (e * n_steps + s) * (FFN_SUB * tc)

    def start_gather(base, j):
        for i in range(tc):
            t = idx_ref[base + i]
            pltpu.make_async_copy(xh_hbm.at[pl.ds(t * ROW_TILES, ROW_TILES)],
                                  bufs[j].at[pl.ds(i * ROW_TILES, ROW_TILES)], sem_g.at[j]).start(priority=i % 2)

    def wait_gather(j):
        pltpu.make_async_copy(xh_hbm.at[pl.ds(0, tc * ROW_TILES)], bufs[j], sem_g.at[j]).wait()

    def start_scatter(base, j):
        for i in range(tc):
            t = idx_ref[base + i]
            pltpu.make_async_copy(bufs[j].at[pl.ds(i * ROW_TILES, COL_TILES)],
                                  xh_hbm.at[pl.ds(t * ROW_TILES, COL_TILES)], sem_o.at[j]).start(priority=i % 2)

    def wait_scatter(j):
        pltpu.make_async_copy(bufs[j].at[pl.ds(0, tc * COL_TILES)],
                              xh_hbm.at[pl.ds(0, tc * COL_TILES)], sem_o.at[j]).wait()

    @pl.when(s == 0)
    def _():
        @pl.when(e > 0)
        def _():
            for j in range(FFN_SUB):
                wait_scatter(j)

        start_gather(step_base, 0)

    wait_gather(0)
    for j in range(FFN_SUB):
        if j + 1 < FFN_SUB:
            @pl.when(s > 0)
            def _():
                wait_scatter(j + 1)

            start_gather(step_base + (j + 1) * tc, j + 1)
        else:
            @pl.when(s + 1 < n_steps)
            def _():
                wait_scatter(0)
                start_gather(step_base + FFN_SUB * tc, 0)
        if j > 0:
            start_scatter(step_base + (j - 1) * tc, j - 1)

        h = _packed_cols(bufs[j], COL_TILES, tc).astype(BF16)
        g = _dot(h, wg_ref[0])
        u = _dot(h, wu_ref[0])
        act = (g * jax.nn.sigmoid(g) * u).astype(BF16)
        y = _dot(act, wd_ref[0]) * gate_ref[j * tc:(j + 1) * tc, 0:1]
        _store_packed(bufs[j], 0, _packed_cols(bufs[j], 0, tc) + y)
        if j + 1 < FFN_SUB:
            wait_gather(j + 1)
    start_scatter(step_base + (FFN_SUB - 1) * tc, FFN_SUB - 1)

    @pl.when((e == N_EXPERTS - 1) & (s == n_steps - 1))
    def _():
        for j in range(FFN_SUB):
            wait_scatter(j)


def _ffn(idx, gate, xh, w_gate, w_up, w_down, cap, tc=256):
    n_steps = cap // (FFN_SUB * tc)
    assert n_steps * FFN_SUB * tc == cap
    wspec = pl.BlockSpec((1, D_MODEL, D_MODEL), lambda e, s, idx_ref: (e, 0, 0))
    any_spec = pl.BlockSpec(memory_space=pl.ANY)
    grid_spec = pltpu.PrefetchScalarGridSpec(
        num_scalar_prefetch=1,
        grid=(N_EXPERTS, n_steps),
        in_specs=[pl.BlockSpec((FFN_SUB * tc, LANES), lambda e, s, idx_ref: (e * n_steps + s, 0)),
                  wspec, wspec, wspec, any_spec],
        out_specs=any_spec,
        scratch_shapes=[pltpu.VMEM((tc * ROW_TILES, LANES), F32)] * FFN_SUB
        + [pltpu.SemaphoreType.DMA((FFN_SUB,)), pltpu.SemaphoreType.DMA((FFN_SUB,))],
    )
    return pl.pallas_call(
        functools.partial(_ffn_kernel, tc=tc, n_steps=n_steps),
        grid_spec=grid_spec,
        out_shape=jax.ShapeDtypeStruct(xh.shape, F32),
        input_output_aliases={5: 0},
        compiler_params=_cparams("arbitrary", "arbitrary"),
        name="ec_ffn",
    )(idx, gate, w_gate, w_up, w_down, xh)


def _final_kernel(x_ref, g_ref, o_ref):
    o_ref[...] = _rms(_packed_cols(x_ref, 0, o_ref.shape[0]), g_ref[...])


def _final_norm(x, g, tm=1024):
    t = x.shape[0] // ROW_TILES
    return pl.pallas_call(
        _final_kernel,
        grid=(t // tm,),
        in_specs=[pl.BlockSpec((tm * ROW_TILES, LANES), lambda i: (i, 0)), pl.BlockSpec(g.shape, lambda i: (0, 0))],
        out_specs=pl.BlockSpec((tm, D_MODEL), lambda i: (i, 0)),
        out_shape=jax.ShapeDtypeStruct((t, D_MODEL), F32),
        compiler_params=_cparams("parallel"),
        name="final_norm",
    )(x, g)


def _moe(xh, aff_t, w_gate, w_up, w_down):
    cap = EC_CAPACITY_FACTOR * (xh.shape[0] // ROW_TILES) // N_EXPERTS
    idx, gate = _select(aff_t, cap)
    return _ffn(idx, gate, xh, w_gate, w_up, w_down, cap)


def _trunk(x3, p):
    b = x3.shape[0]
    t = b * SEQ
    x = x3.reshape(t, D_MODEL)
    seq = lambda a: a.reshape(b, SEQ, a.shape[-1])
    flat = lambda a: a.reshape(t, a.shape[-1])

    qm, km, vm, qs, ks, vs = _pre_ab(x, p["g_mix0"], p["w_big_ab"], p["q_norm"], p["w_q"], p["kv_norm"], p["w_kv"],
                                     p["tabs"])
    o_a = _mla_attn(seq(qm), seq(km), seq(vm))
    o_b = _swa_attn(seq(qs), seq(ks), seq(vs), p["sink"])
    xh, aff_t = _post(flat(o_a), flat(o_b), x, p["w_out_ab"], p["g_ffn0"], p["w_router0"])
    x = _moe(xh, aff_t, p["w_gate0"], p["w_up0"], p["w_down0"])

    qc, kc, vc, qd, kd, vd = _pre_cd(x, p["g_mix1"], p["w_big_cd"], p["tabs"])
    o_c = _dilated_attn(seq(qc), seq(kc), seq(vc))
    o_d = _na_attn(seq(qd), seq(kd), seq(vd), p["na_bias"])
    xh, aff_t = _post(flat(o_c), flat(o_d), x, p["w_out_cd"], p["g_ffn1"], p["w_router1"])
    x = _moe(xh, aff_t, p["w_gate1"], p["w_up1"], p["w_down1"])
    return _final_norm(x, p["g_final"]).reshape(b, SEQ, D_MODEL)


def kernel(x_prompt, x_sample, norm_mix, norm_ffn, norm_final, w_in_ab, mla_q_norm, mla_w_uq, mla_kv_norm,
           mla_w_ukv, swa_sink, w_out_ab, w_in_cd, na_rpb, w_out_cd, w_router, w_gate, w_up, w_down):
    w_big_ab, w_q, w_kv = _prep_ab(w_in_ab[0], mla_w_uq[0], mla_w_ukv[0])
    p = {
        "tabs": _rope_tables(),
        "g_mix0": norm_mix[0:1], "g_mix1": norm_mix[1:2], "g_ffn0": norm_ffn[0:1], "g_ffn1": norm_ffn[1:2],
        "g_final": norm_final.reshape(1, D_MODEL),
        "w_big_ab": w_big_ab, "w_q": w_q, "w_kv": w_kv,
        "q_norm": mla_q_norm[0:1], "kv_norm": mla_kv_norm[0:1],
        "sink": jnp.repeat(swa_sink[0][jnp.array(SWA_HEAD_ORDER)], SWA_BLOCK).reshape(SWA_HEADS * SWA_BLOCK, 1),
        "w_out_ab": _permute_swa_rows(w_out_ab[0]), "w_big_cd": _prep_cd(w_in_cd[0]), "w_out_cd": w_out_cd[0],
        "na_bias": _na_bias(na_rpb[0]),
        "w_router0": w_router[0], "w_router1": w_router[1],
    }
    for l in range(2):
        p[f"w_gate{l}"] = w_gate[l].astype(BF16)
        p[f"w_up{l}"] = w_up[l].astype(BF16)
        p[f"w_down{l}"] = w_down[l].astype(BF16)
    return _trunk(x_prompt, p), _trunk(x_sample, p)
```

```python
import functools

import jax
import jax.numpy as jnp
from jax import lax
from jax.experimental import pallas as pl
from jax.experimental.pallas import tpu as pltpu

D_MODEL = 1024
SEQ = 2048
HEAD_DIM = 64
ROPE_THETA = 10000.0
NEG_INF = -1e30
RMS_EPS = 1e-6

MLA_HEADS = 8
MLA_Q_RANK = 384
MLA_KV_RANK = 256
MLA_NOPE = 64
MLA_ROPE = 32
MLA_V = 64

SWA_HEADS = 8
SWA_KV_HEADS = 2
SWA_GROUP = SWA_HEADS // SWA_KV_HEADS
SWA_HALF_WINDOW = 128
SWA_BLOCK = 128

DIL_HEADS = 8
DIL_BRANCHES = ((128, 1), (512, 4), (2048, 16))

NA_HEADS = 8
GRID_W = 64
NA_KH = 8
NA_KW = 16

N_EXPERTS = 16
EC_CAPACITY_FACTOR = 2

SWA_HEAD_ORDER = tuple(g * SWA_GROUP + r for r in range(SWA_GROUP) for g in range(SWA_KV_HEADS))
LOG2E = 1.4426950408889634
LN2 = 0.6931471805599453

LANES = 128
VMEM_LIMIT = 56 * 1024 * 1024

BF16 = jnp.bfloat16
F32 = jnp.float32


def _cparams(*sem):
    return pltpu.CompilerParams(dimension_semantics=sem, vmem_limit_bytes=VMEM_LIMIT)


def _dot(a, b):
    return jnp.dot(a, b, preferred_element_type=F32)


def _dot_nt(a, b):
    return lax.dot_general(a, b, (((1,), (1,)), ((), ())), preferred_element_type=F32)


def _rms(x, g):
    return x * lax.rsqrt(jnp.mean(x * x, axis=-1, keepdims=True) + RMS_EPS) * g


def _rot_cols(w, n_heads, dim):
    k = w.shape[0]
    w = w.reshape(k, n_heads, dim)
    return jnp.concatenate([-w[..., dim // 2:], w[..., :dim // 2]], axis=-1).reshape(k, n_heads * dim)


def _rope_tables():
    def tables(dim):
        inv = 1.0 / (ROPE_THETA ** (jnp.arange(0, dim, 2, dtype=F32) / dim))
        ang = jnp.arange(SEQ, dtype=F32)[:, None] * inv[None, :]
        c, s = jnp.cos(ang), jnp.sin(ang)
        return jnp.concatenate([c, c], axis=-1), jnp.concatenate([s, s], axis=-1)

    c32, s32 = tables(MLA_ROPE)
    one = jnp.ones((SEQ, MLA_NOPE), F32)
    zero = jnp.zeros((SEQ, MLA_NOPE), F32)
    z32 = jnp.zeros((SEQ, LANES - MLA_NOPE - MLA_ROPE), F32)
    cos_a = jnp.concatenate([one, c32, z32], axis=-1)
    sin_a = jnp.concatenate([zero, s32, z32], axis=-1)
    c64, s64 = tables(HEAD_DIM)
    cos_b = jnp.concatenate([c64, c64], axis=-1)
    sin_b = jnp.concatenate([s64, s64], axis=-1)
    return cos_a, sin_a, cos_b, sin_b


def _pad_heads(w, n_heads, dim):
    k = w.shape[0]
    w = w.reshape(k, n_heads, dim)
    return jnp.pad(w, ((0, 0), (0, 0), (0, LANES - dim))).reshape(k, n_heads * LANES)


def _prep_ab(w_in, w_uq, w_ukv):
    cq = w_in[:, :MLA_Q_RANK]
    o = MLA_Q_RANK
    ckv = w_in[:, o:o + MLA_KV_RANK]
    o += MLA_KV_RANK
    kr = w_in[:, o:o + MLA_ROPE]
    o += MLA_ROPE
    qb = w_in[:, o:o + SWA_HEADS * HEAD_DIM]
    qb = qb.reshape(D_MODEL, SWA_HEADS, HEAD_DIM)[:, list(SWA_HEAD_ORDER)].reshape(D_MODEL, SWA_HEADS * HEAD_DIM)
    o += SWA_HEADS * HEAD_DIM
    kb = w_in[:, o:o + SWA_KV_HEADS * HEAD_DIM]
    o += SWA_KV_HEADS * HEAD_DIM
    vb = w_in[:, o:]
    kr_rot = _rot_cols(kr, 1, MLA_ROPE)
    lead = jnp.zeros((D_MODEL, MLA_NOPE), F32)
    tail = jnp.zeros((D_MODEL, LANES - MLA_NOPE - MLA_ROPE), F32)
    kr_pad = jnp.concatenate([lead, kr, tail], axis=1)
    kr_rot_pad = jnp.concatenate([lead, kr_rot, tail], axis=1)
    w_big = jnp.concatenate([cq, ckv, qb, _rot_cols(qb, SWA_HEADS, HEAD_DIM), kb,
                             _rot_cols(kb, SWA_KV_HEADS, HEAD_DIM), vb, kr_pad, kr_rot_pad], axis=1)
    dq = MLA_NOPE + MLA_ROPE
    wq = w_uq.reshape(MLA_Q_RANK, MLA_HEADS, dq)
    rope = wq[..., MLA_NOPE:]
    rope_rot = jnp.concatenate([-rope[..., MLA_ROPE // 2:], rope[..., :MLA_ROPE // 2]], axis=-1)
    wq_rot = jnp.concatenate([jnp.zeros_like(wq[..., :MLA_NOPE]), rope_rot], axis=-1)
    wq_pad = _pad_heads(wq.reshape(MLA_Q_RANK, -1), MLA_HEADS, dq)
    wq_rot_pad = _pad_heads(wq_rot.reshape(MLA_Q_RANK, -1), MLA_HEADS, dq)
    w_q = jnp.concatenate([wq_pad, wq_rot_pad], axis=1)
    wkv = w_ukv.reshape(MLA_KV_RANK, MLA_HEADS, MLA_NOPE + MLA_V)
    wk_pad = _pad_heads(wkv[..., :MLA_NOPE].reshape(MLA_KV_RANK, -1), MLA_HEADS, MLA_NOPE)
    wv = wkv[..., MLA_NOPE:].reshape(MLA_KV_RANK, MLA_HEADS * MLA_V)
    w_kv = jnp.concatenate([wk_pad, wv], axis=1)
    return w_big.astype(BF16), w_q.astype(BF16), w_kv.astype(BF16)


def _permute_swa_rows(w_out):
    n = MLA_HEADS * MLA_V
    wb = w_out[n:].reshape(SWA_HEADS, HEAD_DIM, D_MODEL)[jnp.array(SWA_HEAD_ORDER)]
    return jnp.concatenate([w_out[:n], wb.reshape(SWA_HEADS * HEAD_DIM, D_MODEL)], axis=0)


def _prep_cd(w_in):
    n = DIL_HEADS * HEAD_DIM
    q_c, k_c, v_c, q_d, k_d, v_d = [w_in[:, i * n:(i + 1) * n] for i in range(6)]
    w_big = jnp.concatenate([q_c, _rot_cols(q_c, DIL_HEADS, HEAD_DIM), k_c, _rot_cols(k_c, DIL_HEADS, HEAD_DIM),
                             v_c, q_d, k_d, v_d], axis=1)
    return w_big.astype(BF16)


def _pre_ab_kernel(x_ref, g_ref, wbig_ref, qn_ref, wq_ref, kvn_ref, wkv_ref, ca_ref, sa_ref, cb_ref, sb_ref,
                   qm_ref, km_ref, vm_ref, qs_ref, ks_ref, vs_ref):
    h = _rms(x_ref[...], g_ref[...]).astype(BF16)
    proj = _dot(h, wbig_ref[...])
    ca, sa, cb, sb = ca_ref[...], sa_ref[...], cb_ref[...], sb_ref[...]
    o = 0
    c_q = proj[:, o:o + MLA_Q_RANK]
    o += MLA_Q_RANK
    c_kv = proj[:, o:o + MLA_KV_RANK]
    o += MLA_KV_RANK
    nq = SWA_HEADS * HEAD_DIM
    q_b, q_b_rot = proj[:, o:o + nq], proj[:, o + nq:o + 2 * nq]
    o += 2 * nq
    k_b, k_b_rot, v_b = proj[:, o:o + LANES], proj[:, o + LANES:o + 2 * LANES], proj[:, o + 2 * LANES:o + 3 * LANES]
    o += 3 * LANES
    k_r, k_r_rot = proj[:, o:o + LANES], proj[:, o + LANES:o + 2 * LANES]

    c_swa = HEAD_DIM ** -0.5 * LOG2E
    qs_ref[...] = jnp.concatenate(
        [(q_b[:, i * LANES:(i + 1) * LANES] * cb + q_b_rot[:, i * LANES:(i + 1) * LANES] * sb) * c_swa
         for i in range(nq // LANES)], axis=1).astype(BF16)
    ks_ref[...] = (k_b * cb + k_b_rot * sb).astype(BF16)
    vs_ref[...] = v_b.astype(BF16)

    q_all = _dot(_rms(c_q, qn_ref[...]).astype(BF16), wq_ref[...])
    nh = MLA_HEADS * LANES
    c_mla = (MLA_NOPE + MLA_ROPE) ** -0.5 * LOG2E
    qm_ref[...] = jnp.concatenate(
        [(q_all[:, i * LANES:(i + 1) * LANES] * ca + q_all[:, nh + i * LANES:nh + (i + 1) * LANES] * sa) * c_mla
         for i in range(MLA_HEADS)], axis=1).astype(BF16)
    kv = _dot(_rms(c_kv, kvn_ref[...]).astype(BF16), wkv_ref[...])
    k_rope = k_r * ca + k_r_rot * sa
    km_ref[...] = jnp.concatenate(
        [kv[:, i * LANES:(i + 1) * LANES] + k_rope for i in range(MLA_HEADS)], axis=1).astype(BF16)
    vm_ref[...] = kv[:, nh:].astype(BF16)


def _pre_ab(x, g, w_big, qn, w_q, kvn, w_kv, tabs, tm=512):
    t = x.shape[0]
    nsb = SEQ // tm
    full = lambda a: pl.BlockSpec(a.shape, lambda i: (0,) * a.ndim)
    tab = pl.BlockSpec((tm, LANES), lambda i: (i % nsb, 0))
    row = lambda n: pl.BlockSpec((tm, n), lambda i: (i, 0))
    outs = [(MLA_HEADS * LANES, BF16), (MLA_HEADS * LANES, BF16), (MLA_HEADS * MLA_V, BF16),
            (SWA_HEADS * HEAD_DIM, BF16), (LANES, BF16), (LANES, BF16)]
    return pl.pallas_call(
        _pre_ab_kernel,
        grid=(t // tm,),
        in_specs=[row(D_MODEL), full(g), full(w_big), full(qn), full(w_q), full(kvn), full(w_kv), tab, tab, tab, tab],
        out_specs=[row(n) for n, _ in outs],
        out_shape=[jax.ShapeDtypeStruct((t, n), d) for n, d in outs],
        compiler_params=_cparams("parallel"),
        name="pre_ab",
    )(x, g, w_big, qn, w_q, kvn, w_kv, *tabs)


def _pre_cd_kernel(x_ref, g_ref, wbig_ref, cb_ref, sb_ref, qc_ref, kc_ref, vc_ref, qd_ref, kd_ref, vd_ref):
    h = _rms(x_ref[...], g_ref[...]).astype(BF16)
    cb, sb = cb_ref[...], sb_ref[...]
    n = DIL_HEADS * HEAD_DIM
    w = wbig_ref

    c_att = HEAD_DIM ** -0.5 * LOG2E

    def roped(o, c):
        a = _dot(h, w[:, o:o + n])
        b = _dot(h, w[:, o + n:o + 2 * n])
        return jnp.concatenate([(a[:, i * LANES:(i + 1) * LANES] * cb + b[:, i * LANES:(i + 1) * LANES] * sb) * c
                                for i in range(n // LANES)], axis=1).astype(BF16)

    qc_ref[...] = roped(0, c_att)
    kc_ref[...] = roped(2 * n, 1.0)
    vc_ref[...] = _dot(h, w[:, 4 * n:5 * n]).astype(BF16)
    qd_ref[...] = (_dot(h, w[:, 5 * n:6 * n]) * c_att).astype(BF16)
    kd_ref[...] = _dot(h, w[:, 6 * n:7 * n]).astype(BF16)
    vd_ref[...] = _dot(h, w[:, 7 * n:8 * n]).astype(BF16)


def _pre_cd(x, g, w_big, tabs, tm=512):
    t = x.shape[0]
    nsb = SEQ // tm
    n = DIL_HEADS * HEAD_DIM
    full = lambda a: pl.BlockSpec(a.shape, lambda i: (0,) * a.ndim)
    tab = pl.BlockSpec((tm, LANES), lambda i: (i % nsb, 0))
    row = lambda m: pl.BlockSpec((tm, m), lambda i: (i, 0))
    return pl.pallas_call(
        _pre_cd_kernel,
        grid=(t // tm,),
        in_specs=[row(D_MODEL), full(g), full(w_big), tab, tab],
        out_specs=[row(n)] * 6,
        out_shape=[jax.ShapeDtypeStruct((t, n), BF16)] * 6,
        compiler_params=_cparams("parallel"),
        name="pre_cd",
    )(x, g, w_big, tabs[2], tabs[3])


def _low_half():
    return lax.broadcasted_iota(jnp.int32, (1, LANES), 1) < LANES // 2


def _split_heads(x2, low):
    zero = jnp.zeros_like(x2)
    return jnp.where(low, x2, zero), jnp.where(low, zero, x2)


def _softmax_pv(s, v2):
    m = jnp.max(s, axis=-1, keepdims=True)
    e = jnp.exp2(s - m)
    l = jnp.sum(e, axis=-1, keepdims=True)
    return _dot(e.astype(BF16), v2) / l


def _mla_kernel(q_ref, k_ref, v_ref, o_ref):
    low = _low_half()
    outs = []
    for p in range(MLA_HEADS // 2):
        v2 = v_ref[0, :, p * LANES:(p + 1) * LANES]
        halves = [_softmax_pv(_dot_nt(q_ref[0, :, h * LANES:(h + 1) * LANES], k_ref[0, :, h * LANES:(h + 1) * LANES]), v2)
                  for h in (2 * p, 2 * p + 1)]
        outs.append(jnp.where(low, halves[0], halves[1]))
    o_ref[0] = jnp.concatenate(outs, axis=1).astype(o_ref.dtype)


def _mla_attn(q, k, v, tq=512):
    b = q.shape[0]
    nq, nv = q.shape[-1], v.shape[-1]
    return pl.pallas_call(
        _mla_kernel,
        grid=(b, SEQ // tq),
        in_specs=[pl.BlockSpec((1, tq, nq), lambda i, j: (i, j, 0)),
                  pl.BlockSpec((1, SEQ, nq), lambda i, j: (i, 0, 0)),
                  pl.BlockSpec((1, SEQ, nv), lambda i, j: (i, 0, 0))],
        out_specs=pl.BlockSpec((1, tq, nv), lambda i, j: (i, j, 0)),
        out_shape=jax.ShapeDtypeStruct((b, SEQ, nv), BF16),
        compiler_params=_cparams("parallel", "parallel"),
        name="mla_attn",
    )(q, k, v)


def _swa_kernel(q_ref, kp_ref, kc_ref, kn_ref, vp_ref, vc_ref, vn_ref, sink_ref, o_ref, *, blk):
    i = pl.program_id(1)
    low = _low_half()
    k = jnp.concatenate([kp_ref[0], kc_ref[0], kn_ref[0]], axis=0)
    v = jnp.concatenate([vp_ref[0], vc_ref[0], vn_ref[0]], axis=0)
    qpos = i * blk + lax.broadcasted_iota(jnp.int32, (blk, 3 * blk), 0)
    kpos = (i - 1) * blk + lax.broadcasted_iota(jnp.int32, (blk, 3 * blk), 1)
    valid = (jnp.abs(qpos - kpos) <= SWA_HALF_WINDOW) & (kpos >= 0) & (kpos < SEQ)
    mask = jnp.where(valid, 0.0, NEG_INF)
    tiles = []
    for r in range(SWA_GROUP):
        q = jnp.concatenate(_split_heads(q_ref[0, :, r * LANES:(r + 1) * LANES], low), axis=0)
        s = (_dot_nt(q, k).reshape(2, blk, 3 * blk) + mask[None]).reshape(2 * blk, 3 * blk)
        m = jnp.max(s, axis=-1, keepdims=True)
        e = jnp.exp2(s - m)
        l = jnp.sum(e, axis=-1, keepdims=True)
        lse = m * LN2 + jnp.log(l)
        keep = jax.nn.sigmoid(lse - sink_ref[2 * r * blk:(2 * r + 2) * blk])
        o = _dot(e.astype(BF16), v) * (keep / l)
        tiles.append(jnp.where(low, o[:blk], o[blk:]))
    o_ref[0] = jnp.concatenate(tiles, axis=1).astype(o_ref.dtype)


def _swa_attn(q, k, v, sink_rows, blk=SWA_BLOCK):
    b = q.shape[0]
    nb = SEQ // blk
    kv = lambda f: pl.BlockSpec((1, blk, LANES), f)
    prev = lambda i, j: (i, jnp.maximum(j - 1, 0), 0)
    cur = lambda i, j: (i, j, 0)
    nxt = lambda i, j: (i, jnp.minimum(j + 1, nb - 1), 0)
    return pl.pallas_call(
        functools.partial(_swa_kernel, blk=blk),
        grid=(b, nb),
        in_specs=[pl.BlockSpec((1, blk, SWA_HEADS * HEAD_DIM), cur), kv(prev), kv(cur), kv(nxt),
                  kv(prev), kv(cur), kv(nxt), pl.BlockSpec(sink_rows.shape, lambda i, j: (0, 0))],
        out_specs=pl.BlockSpec((1, blk, SWA_HEADS * HEAD_DIM), cur),
        out_shape=jax.ShapeDtypeStruct((b, SEQ, SWA_HEADS * HEAD_DIM), BF16),
        compiler_params=_cparams("parallel", "parallel"),
        name="swa_attn",
    )(q, k, k, k, v, v, v, sink_rows)


DIL_BLK = 128
DIL_WIN = 256


def _banded_blocks(q_blocks, k_wins, v_wins, offsets, half, low):
    n = len(q_blocks)
    win = k_wins[0].shape[0]
    q3 = jnp.stack([jnp.concatenate(_split_heads(q, low), axis=0) for q in q_blocks])
    k3, v3 = jnp.stack(k_wins), jnp.stack(v_wins)
    rel = (lax.broadcasted_iota(jnp.int32, (DIL_BLK, win), 0) - lax.broadcasted_iota(jnp.int32, (DIL_BLK, win), 1))
    masks = {off: jnp.where(jnp.abs(rel - off) <= half, 0.0, NEG_INF) for off in set(offsets)}
    mask3 = jnp.stack([masks[off] for off in offsets])
    s = jnp.einsum("bqd,bkd->bqk", q3, k3, preferred_element_type=F32)
    s = (s.reshape(n, 2, DIL_BLK, win) + mask3[:, None]).reshape(n, 2 * DIL_BLK, win)
    m = jnp.max(s, axis=-1, keepdims=True)
    e = jnp.exp2(s - m)
    l = jnp.sum(e, axis=-1, keepdims=True)
    o = jnp.einsum("bqk,bkd->bqd", e.astype(BF16), v3, preferred_element_type=F32)
    pick = lambda t: jnp.where(low, t[:, :DIL_BLK], t[:, DIL_BLK:])
    return pick(o), pick(jnp.broadcast_to(m, o.shape)), pick(jnp.broadcast_to(l, o.shape))


def _dilated_kernel(q_ref, k_ref, v_ref, o_ref, qf, kf, vf, *stats):
    low = _low_half()
    qf[...] = q_ref[0].astype(F32)
    kf[...] = k_ref[0].astype(F32)
    vf[...] = v_ref[0].astype(F32)
    for n, (window, dil) in enumerate(DIL_BRANCHES):
        length = SEQ // dil
        half = window // (2 * dil)
        win = min(DIL_WIN, length)
        per_stream = length // DIL_BLK
        starts = [min(max(a * DIL_BLK - half, 0), length - win) for a in range(per_stream)]
        q_blocks, k_wins, v_wins, offsets = [], [], [], []
        for r in range(dil):
            stream = lambda ref: ref[pl.ds(r, length, stride=dil), :].astype(BF16)
            qs, ks, vs = stream(qf), stream(kf), stream(vf)
            for a, w0 in enumerate(starts):
                q_blocks.append(qs[a * DIL_BLK:(a + 1) * DIL_BLK])
                k_wins.append(ks[w0:w0 + win])
                v_wins.append(vs[w0:w0 + win])
                offsets.append(w0 - a * DIL_BLK)
        tiles = _banded_blocks(q_blocks, k_wins, v_wins, offsets, half, low)
        for tile, dst in zip(tiles, stats[3 * n:3 * n + 3]):
            for r in range(dil):
                dst[pl.ds(r, length, stride=dil), :] = tile[r * per_stream:(r + 1) * per_stream].reshape(length, LANES)
    accs, ms, ls = stats[0::3], stats[1::3], stats[2::3]
    m_all = functools.reduce(jnp.maximum, [m[...] for m in ms])
    ws = [jnp.exp2(m[...] - m_all) for m in ms]
    num = sum(w * a[...] for w, a in zip(ws, accs))
    den = sum(w * l[...] for w, l in zip(ws, ls))
    o_ref[0] = (num / den).astype(o_ref.dtype)


def _dilated_attn(q, k, v):
    b = q.shape[0]
    spec = pl.BlockSpec((1, SEQ, LANES), lambda i, p: (i, 0, p))
    return pl.pallas_call(
        _dilated_kernel,
        grid=(b, DIL_HEADS // 2),
        in_specs=[spec, spec, spec],
        out_specs=spec,
        out_shape=jax.ShapeDtypeStruct((b, SEQ, DIL_HEADS * HEAD_DIM), BF16),
        scratch_shapes=[pltpu.VMEM((SEQ, LANES), F32)] * (3 + 3 * len(DIL_BRANCHES)),
        compiler_params=_cparams("parallel", "parallel"),
        name="dilated_attn",
    )(q, k, v)


NA_Q_ROWS = 4
NA_Q = NA_Q_ROWS * GRID_W
NA_K_ROWS = 12
NA_K = NA_K_ROWS * GRID_W
NA_GROUPS = SEQ // NA_Q
NA_RPB_ROWS = 2 * NA_KH - 1
NA_RPB_COLS = 2 * NA_KW - 1
NA_WIN0 = (lambda a: 0, lambda a: a, lambda a: NA_Q_ROWS)
NA_RHO0 = (NA_KH - 1, NA_KH - 1 - NA_Q_ROWS, NA_KH - 1 - 2 * NA_Q_ROWS)


def _na_bias_kernel(rpb_ref, o_ref):
    pair = pl.program_id(0)
    c = lax.broadcasted_iota(jnp.int32, (GRID_W, LANES), 0)
    lane = lax.broadcasted_iota(jnp.int32, (GRID_W, LANES), 1)
    odd = lane >= GRID_W
    cp = lane & (GRID_W - 1)
    c0 = jnp.clip(c - NA_KW // 2, 0, GRID_W - NA_KW)
    col_ok = (cp >= c0) & (cp < c0 + NA_KW)
    d = cp - c + (NA_KW - 1)
    neg = jnp.full((GRID_W, LANES), NEG_INF, F32)

    for a in range(2):
        head = 2 * pair + a
        cache = {}

        def tile(rho_even, ok_even, ok_odd):
            key = (rho_even, ok_even, ok_odd)
            if key not in cache:
                acc = neg
                for kk in range(NA_RPB_COLS):
                    lo = rpb_ref[(head * NA_RPB_ROWS + rho_even) * NA_RPB_COLS + kk] if ok_even else NEG_INF
                    hi = rpb_ref[(head * NA_RPB_ROWS + rho_even + 1) * NA_RPB_COLS + kk] if ok_odd else NEG_INF
                    acc = jnp.where(d == kk, jnp.where(odd, hi, lo), acc)
                cache[key] = jnp.where(col_ok, acc * LOG2E, NEG_INF)
            return cache[key]

        for pat in range(3):
            for qa in range(NA_Q_ROWS):
                w0 = NA_WIN0[pat](qa)
                tiles = []
                for j in range(0, NA_K_ROWS, 2):
                    ok_even = w0 <= j < w0 + NA_KH
                    ok_odd = w0 <= j + 1 < w0 + NA_KH
                    rho = j - qa + NA_RHO0[pat]
                    tiles.append(tile(rho, ok_even, ok_odd) if (ok_even or ok_odd) else neg)
                o_ref[pat, 0, a, qa * GRID_W:(qa + 1) * GRID_W, :] = jnp.concatenate(tiles, axis=1)


def _na_bias(rpb):
    return pl.pallas_call(
        _na_bias_kernel,
        grid=(NA_HEADS // 2,),
        in_specs=[pl.BlockSpec(memory_space=pltpu.SMEM)],
        out_specs=pl.BlockSpec((3, 1, 2, NA_Q, NA_K), lambda p: (0, p, 0, 0, 0)),
        out_shape=jax.ShapeDtypeStruct((3, NA_HEADS // 2, 2, NA_Q, NA_K), F32),
        compiler_params=_cparams("parallel"),
        name="na_bias",
    )(rpb.reshape(-1))


def _na_kernel(q_ref, k0_ref, k1_ref, k2_ref, v0_ref, v1_ref, v2_ref, bias_ref, o_ref):
    low = _low_half()
    k = jnp.concatenate([k0_ref[0], k1_ref[0], k2_ref[0]], axis=0)
    v = jnp.concatenate([v0_ref[0], v1_ref[0], v2_ref[0]], axis=0)
    halves = [_softmax_pv(_dot_nt(q, k) + bias_ref[0, 0, a], v)
              for a, q in enumerate(_split_heads(q_ref[0], low))]
    o_ref[0] = jnp.where(low, halves[0], halves[1]).astype(o_ref.dtype)


def _na_attn(q, k, v, bias):
    b = q.shape[0]
    first_block = lambda g: jnp.clip(g - 1, 0, NA_GROUPS - NA_K // NA_Q)
    pattern = lambda g: jnp.where(g == 0, 0, jnp.where(g == NA_GROUPS - 1, 2, 1))
    cur = pl.BlockSpec((1, NA_Q, LANES), lambda p, g, i: (i, g, p))
    kv = lambda n: pl.BlockSpec((1, NA_Q, LANES), lambda p, g, i: (i, first_block(g) + n, p))
    return pl.pallas_call(
        _na_kernel,
        grid=(NA_HEADS // 2, NA_GROUPS, b),
        in_specs=[cur, kv(0), kv(1), kv(2), kv(0), kv(1), kv(2),
                  pl.BlockSpec((1, 1, 2, NA_Q, NA_K), lambda p, g, i: (pattern(g), p, 0, 0, 0))],
        out_specs=cur,
        out_shape=jax.ShapeDtypeStruct((b, SEQ, NA_HEADS * HEAD_DIM), BF16),
        compiler_params=_cparams("parallel", "parallel", "parallel"),
        name="na_attn",
    )(q, k, k, k, v, v, v, bias)


def _post_kernel(oa_ref, ob_ref, x_ref, w1_ref, w2_ref, g_ref, wr_ref, xh_ref, aff_ref):
    x = x_ref[...] + _dot(oa_ref[...], w1_ref[...]) + _dot(ob_ref[...], w2_ref[...])
    h = _rms(x, g_ref[...])
    xh_ref[:, :D_MODEL] = x
    xh_ref[:, D_MODEL:] = h
    logits = _dot_nt(wr_ref[...], h.astype(BF16))
    e = jnp.exp(logits - jnp.max(logits, axis=0, keepdims=True))
    aff_ref[...] = e / jnp.sum(e, axis=0, keepdims=True)


def _post(oa, ob, x, w_out, g, w_router, tm=512):
    t = x.shape[0]
    half = oa.shape[1]
    w1, w2 = w_out[:half].astype(BF16), w_out[half:].astype(BF16)
    wr = w_router.T.astype(BF16)
    full = lambda a: pl.BlockSpec(a.shape, lambda i: (0,) * a.ndim)
    row = lambda n: pl.BlockSpec((tm, n), lambda i: (i, 0))
    return pl.pallas_call(
        _post_kernel,
        grid=(t // tm,),
        in_specs=[row(half), row(half), row(D_MODEL), full(w1), full(w2), full(g), full(wr)],
        out_specs=[row(2 * D_MODEL), pl.BlockSpec((N_EXPERTS, tm), lambda i: (0, i))],
        out_shape=[jax.ShapeDtypeStruct((t, 2 * D_MODEL), F32), jax.ShapeDtypeStruct((N_EXPERTS, t), F32)],
        compiler_params=_cparams("parallel"),
        name="post_attn",
    )(oa, ob, x, w1, w2, g, wr)


def _select_kernel(aff_ref, idx_ref, gate_ref, incl_s, a_hi, a_mid, a_lo, start_s, end_s, cnt_s, *, cap, tc):
    s_id = pl.program_id(1)
    rows = aff_ref.shape[1]

    @pl.when(s_id == 0)
    def _():
        aff = aff_ref[0]
        bits = pltpu.bitcast(aff, jnp.int32)

        def search(i, prefix):
            cand = prefix | jnp.left_shift(jnp.int32(1), 30 - i)
            n = jnp.sum((bits >= cand).astype(F32), keepdims=True)
            return jnp.where(n >= cap, cand, prefix)

        thr = lax.fori_loop(0, 31, search, jnp.zeros((1, 1), jnp.int32))
        gt = bits > thr
        eq = bits == thr
        need = cap - jnp.sum(gt.astype(F32), keepdims=True)
        li = lax.broadcasted_iota(jnp.int32, (LANES, LANES), 0)
        lj = lax.broadcasted_iota(jnp.int32, (LANES, LANES), 1)
        tri = (li <= lj).astype(BF16)
        ones = jnp.ones((LANES, LANES), BF16)
        ri = lax.broadcasted_iota(jnp.int32, (rows, rows), 0)
        rj = lax.broadcasted_iota(jnp.int32, (rows, rows), 1)
        below = (rj < ri).astype(BF16)
        above = (ri < rj).astype(BF16)

        eqb = eq.astype(BF16)
        eq_rank = _dot(below, _dot(eqb, ones).astype(BF16)) + _dot(eqb, tri) - eqb.astype(F32)
        sel = (gt | (eq & (eq_rank < need))).astype(BF16)
        incl_s[...] = _dot(sel, tri).astype(BF16)
        cnt = _dot_nt(jnp.ones((8, LANES), BF16), sel)
        start = _dot(cnt.astype(BF16), above)
        cnt_s[...] = cnt
        start_s[...] = start
        end_s[...] = start + cnt
        hi = aff.astype(BF16)
        r1 = aff - hi.astype(F32)
        mid = r1.astype(BF16)
        a_hi[...] = hi
        a_mid[...] = mid
        a_lo[...] = (r1 - mid.astype(F32)).astype(BF16)

    c = (s_id * tc + lax.broadcasted_iota(jnp.int32, (tc, rows), 0)).astype(F32)
    start, end, cnt = start_s[0:1, :], end_s[0:1, :], cnt_s[0:1, :]
    before = c >= end
    row_id = jnp.sum(before.astype(F32), axis=1, keepdims=True)
    row_start = jnp.sum(jnp.where(before, cnt, 0.0), axis=1, keepdims=True)
    onehot = ((c >= start) & (c < end)).astype(BF16)
    within = _dot(onehot, incl_s[...])
    k = c[:, 0:1] - row_start
    lane = jnp.sum((within <= k).astype(F32), axis=1, keepdims=True)
    idx_ref[0] = (row_id * LANES + lane).astype(jnp.int32)
    vals = _dot(onehot, a_hi[...]) + _dot(onehot, a_mid[...]) + _dot(onehot, a_lo[...])
    lane_iota = lax.broadcasted_iota(jnp.int32, (tc, LANES), 1).astype(F32)
    gate = jnp.sum(jnp.where(lane_iota == lane, vals, 0.0), axis=1, keepdims=True)
    gate_ref[0] = jnp.broadcast_to(gate, (tc, LANES))


def _select(aff_t, cap, tc=512):
    t = aff_t.shape[1]
    rows = t // LANES
    aff3 = aff_t.reshape(N_EXPERTS, rows, LANES)
    idx, gate = pl.pallas_call(
        functools.partial(_select_kernel, cap=cap, tc=tc),
        grid=(N_EXPERTS, cap // tc),
        in_specs=[pl.BlockSpec((1, rows, LANES), lambda e, s: (e, 0, 0))],
        out_specs=[pl.BlockSpec((1, tc, 1), lambda e, s: (e, s, 0)),
                   pl.BlockSpec((1, tc, LANES), lambda e, s: (e, s, 0))],
        out_shape=[jax.ShapeDtypeStruct((N_EXPERTS, cap, 1), jnp.int32),
                   jax.ShapeDtypeStruct((N_EXPERTS, cap, LANES), F32)],
        scratch_shapes=[pltpu.VMEM((rows, LANES), BF16)] * 4 + [pltpu.VMEM((8, rows), F32)] * 3,
        compiler_params=_cparams("parallel", "arbitrary"),
        name="ec_select",
    )(aff3)
    return idx.reshape(N_EXPERTS * cap), gate.reshape(N_EXPERTS * cap, LANES)


FFN_SUB = 4
FFN_AHEAD = 2


def _ffn_kernel(idx_ref, gate_ref, wg_ref, wu_ref, wd_ref, xh_in, xh_hbm, *scratch, tc, n_steps):
    del xh_in
    bufs, (sem_g, sem_o) = scratch[:FFN_SUB], scratch[FFN_SUB:]
    e, s = pl.program_id(0), pl.program_id(1)
    step_base = (e * n_steps + s) * (FFN_SUB * tc)

    def start_gather(base, j):
        for i in range(tc):
            t = idx_ref[base + i]
            pltpu.make_async_copy(xh_hbm.at[pl.ds(t, 1)], bufs[j].at[pl.ds(i, 1)], sem_g.at[j]).start(priority=i % 2)

    def wait_gather(j):
        pltpu.make_async_copy(xh_hbm.at[pl.ds(0, tc)], bufs[j], sem_g.at[j]).wait()

    def start_scatter(base, j):
        for i in range(tc):
            t = idx_ref[base + i]
            pltpu.make_async_copy(bufs[j].at[pl.ds(i, 1), pl.ds(0, D_MODEL)],
                                  xh_hbm.at[pl.ds(t, 1), pl.ds(0, D_MODEL)], sem_o.at[j]).start(priority=i % 2)

    def wait_scatter(j):
        pltpu.make_async_copy(bufs[j].at[:, pl.ds(0, D_MODEL)],
                              xh_hbm.at[pl.ds(0, tc), pl.ds(0, D_MODEL)], sem_o.at[j]).wait()

    def drain():
        for j in range(FFN_AHEAD):
            wait_gather(j)
        for j in range(FFN_AHEAD, FFN_SUB):
            wait_scatter(j)

    @pl.when(s == 0)
    def _():
        pl.when(e > 0)(drain)
        for j in range(FFN_AHEAD):
            start_gather(step_base + j * tc, j)

    for j in range(FFN_SUB):
        wait_gather(j)
        nj = (j + FFN_AHEAD) % FFN_SUB
        if j + FFN_AHEAD < FFN_SUB:
            pl.when(s > 0)(functools.partial(wait_scatter, nj))
            ahead = step_base + (j + FFN_AHEAD) * tc
        else:
            wait_scatter(nj)
            ahead = jnp.where(s + 1 < n_steps, step_base + (j + FFN_AHEAD) * tc, step_base + nj * tc)
        start_gather(ahead, nj)
        if j > 0:
            start_scatter(step_base + (j - 1) * tc, j - 1)

        h = bufs[j][:, D_MODEL:].astype(BF16)
        g = _dot(h, wg_ref[0])
        u = _dot(h, wu_ref[0])
        act = (g * jax.nn.sigmoid(g) * u).astype(BF16)
        y = _dot(act, wd_ref[0]) * gate_ref[j * tc:(j + 1) * tc, 0:1]
        bufs[j][:, :D_MODEL] = bufs[j][:, :D_MODEL] + y
    start_scatter(step_base + (FFN_SUB - 1) * tc, FFN_SUB - 1)

    pl.when((e == N_EXPERTS - 1) & (s == n_steps - 1))(drain)


def _ffn(idx, gate, xh, w_gate, w_up, w_down, cap, tc=256):
    n_steps = cap // (FFN_SUB * tc)
    assert n_steps * FFN_SUB * tc == cap
    wspec = pl.BlockSpec((1, D_MODEL, D_MODEL), lambda e, s, idx_ref: (e, 0, 0))
    any_spec = pl.BlockSpec(memory_space=pl.ANY)
    grid_spec = pltpu.PrefetchScalarGridSpec(
        num_scalar_prefetch=1,
        grid=(N_EXPERTS, n_steps),
        in_specs=[pl.BlockSpec((FFN_SUB * tc, LANES), lambda e, s, idx_ref: (e * n_steps + s, 0)),
                  wspec, wspec, wspec, any_spec],
        out_specs=any_spec,
        scratch_shapes=[pltpu.VMEM((tc, 2 * D_MODEL), F32)] * FFN_SUB
        + [pltpu.SemaphoreType.DMA((FFN_SUB,)), pltpu.SemaphoreType.DMA((FFN_SUB,))],
    )
    return pl.pallas_call(
        functools.partial(_ffn_kernel, tc=tc, n_steps=n_steps),
        grid_spec=grid_spec,
        out_shape=jax.ShapeDtypeStruct(xh.shape, F32),
        input_output_aliases={5: 0},
        compiler_params=_cparams("arbitrary", "arbitrary"),
        name="ec_ffn",
    )(idx, gate, w_gate, w_up, w_down, xh)


def _final_kernel(x_ref, g_ref, o_ref):
    o_ref[...] = _rms(x_ref[...], g_ref[...])


def _final_norm(x, g, tm=1024):
    t = x.shape[0]
    return pl.pallas_call(
        _final_kernel,
        grid=(t // tm,),
        in_specs=[pl.BlockSpec((tm, D_MODEL), lambda i: (i, 0)), pl.BlockSpec(g.shape, lambda i: (0, 0))],
        out_specs=pl.BlockSpec((tm, D_MODEL), lambda i: (i, 0)),
        out_shape=jax.ShapeDtypeStruct((t, D_MODEL), F32),
        compiler_params=_cparams("parallel"),
        name="final_norm",
    )(x, g)


def _moe(xh, aff_t, w_gate, w_up, w_down):
    cap = EC_CAPACITY_FACTOR * xh.shape[0] // N_EXPERTS
    idx, gate = _select(aff_t, cap)
    return _ffn(idx, gate, xh, w_gate, w_up, w_down, cap)


def _trunk(x3, p):
    b = x3.shape[0]
    t = b * SEQ
    x = x3.reshape(t, D_MODEL)
    seq = lambda a: a.reshape(b, SEQ, a.shape[-1])
    flat = lambda a: a.reshape(t, a.shape[-1])

    qm, km, vm, qs, ks, vs = _pre_ab(x, p["g_mix0"], p["w_big_ab"], p["q_norm"], p["w_q"], p["kv_norm"], p["w_kv"],
                                     p["tabs"])
    o_a = _mla_attn(seq(qm), seq(km), seq(vm))
    o_b = _swa_attn(seq(qs), seq(ks), seq(vs), p["sink"])
    xh, aff_t = _post(flat(o_a), flat(o_b), x, p["w_out_ab"], p["g_ffn0"], p["w_router0"])
    x = _moe(xh, aff_t, p["w_gate0"], p["w_up0"], p["w_down0"])

    qc, kc, vc, qd, kd, vd = _pre_cd(x, p["g_mix1"], p["w_big_cd"], p["tabs"])
    o_c = _dilated_attn(seq(qc), seq(kc), seq(vc))
    o_d = _na_attn(seq(qd), seq(kd), seq(vd), p["na_bias"])
    xh, aff_t = _post(flat(o_c), flat(o_d), x, p["w_out_cd"], p["g_ffn1"], p["w_router1"])
    x = _moe(xh, aff_t, p["w_gate1"], p["w_up1"], p["w_down1"])
    return _final_norm(x, p["g_final"]).reshape(b, SEQ, D_MODEL)


def kernel(x_prompt, x_sample, norm_mix, norm_ffn, norm_final, w_in_ab, mla_q_norm, mla_w_uq, mla_kv_norm,
           mla_w_ukv, swa_sink, w_out_ab, w_in_cd, na_rpb, w_out_cd, w_router, w_gate, w_up, w_down):
    w_big_ab, w_q, w_kv = _prep_ab(w_in_ab[0], mla_w_uq[0], mla_w_ukv[0])
    p = {
        "tabs": _rope_tables(),
        "g_mix0": norm_mix[0:1], "g_mix1": norm_mix[1:2], "g_ffn0": norm_ffn[0:1], "g_ffn1": norm_ffn[1:2],
        "g_final": norm_final.reshape(1, D_MODEL),
        "w_big_ab": w_big_ab, "w_q": w_q, "w_kv": w_kv,
        "q_norm": mla_q_norm[0:1], "kv_norm": mla_kv_norm[0:1],
        "sink": jnp.repeat(swa_sink[0][jnp.array(SWA_HEAD_ORDER)], SWA_BLOCK).reshape(SWA_HEADS * SWA_BLOCK, 1),
        "w_out_ab": _permute_swa_rows(w_out_ab[0]), "w_big_cd": _prep_cd(w_in_cd[0]), "w_out_cd": w_out_cd[0],
        "na_bias": _na_bias(na_rpb[0]),
        "w_router0": w_router[0], "w_router1": w_router[1],
    }
    for l in range(2):
        p[f"w_gate{l}"] = w_gate[l].astype(BF16)
        p[f"w_up{l}"] = w_up[l].astype(BF16)
        p[f"w_down{l}"] = w_down[l].astype(BF16)
    return _trunk(x_prompt, p), _trunk(x_sample, p)
```

```python
import functools

import jax
import jax.numpy as jnp
from jax import lax
from jax.experimental import pallas as pl
from jax.experimental.pallas import tpu as pltpu

D_MODEL = 1024
SEQ = 2048
HEAD_DIM = 64
ROPE_THETA = 10000.0
NEG_INF = -1e30
RMS_EPS = 1e-6

MLA_HEADS = 8
MLA_Q_RANK = 384
MLA_KV_RANK = 256
MLA_NOPE = 64
MLA_ROPE = 32
MLA_V = 64

SWA_HEADS = 8
SWA_KV_HEADS = 2
SWA_GROUP = SWA_HEADS // SWA_KV_HEADS
SWA_HALF_WINDOW = 128
SWA_BLOCK = 128

DIL_HEADS = 8
DIL_BRANCHES = ((128, 1), (512, 4), (2048, 16))

NA_HEADS = 8
GRID_W = 64
NA_KH = 8
NA_KW = 16

N_EXPERTS = 16
EC_CAPACITY_FACTOR = 2

SWA_HEAD_ORDER = tuple(g * SWA_GROUP + r for r in range(SWA_GROUP) for g in range(SWA_KV_HEADS))
LOG2E = 1.4426950408889634
LN2 = 0.6931471805599453

LANES = 128
VMEM_LIMIT = 56 * 1024 * 1024

BF16 = jnp.bfloat16
F32 = jnp.float32


def _cparams(*sem):
    return pltpu.CompilerParams(dimension_semantics=sem, vmem_limit_bytes=VMEM_LIMIT)


def _dot(a, b):
    return jnp.dot(a, b, preferred_element_type=F32)


def _dot_nt(a, b):
    return lax.dot_general(a, b, (((1,), (1,)), ((), ())), preferred_element_type=F32)


def _rms(x, g):
    return x * lax.rsqrt(jnp.mean(x * x, axis=-1, keepdims=True) + RMS_EPS) * g


def _rot_cols(w, n_heads, dim):
    k = w.shape[0]
    w = w.reshape(k, n_heads, dim)
    return jnp.concatenate([-w[..., dim // 2:], w[..., :dim // 2]], axis=-1).reshape(k, n_heads * dim)


def _rope_tables():
    def tables(dim):
        inv = 1.0 / (ROPE_THETA ** (jnp.arange(0, dim, 2, dtype=F32) / dim))
        ang = jnp.arange(SEQ, dtype=F32)[:, None] * inv[None, :]
        c, s = jnp.cos(ang), jnp.sin(ang)
        return jnp.concatenate([c, c], axis=-1), jnp.concatenate([s, s], axis=-1)

    c32, s32 = tables(MLA_ROPE)
    one = jnp.ones((SEQ, MLA_NOPE), F32)
    zero = jnp.zeros((SEQ, MLA_NOPE), F32)
    z32 = jnp.zeros((SEQ, LANES - MLA_NOPE - MLA_ROPE), F32)
    cos_a = jnp.concatenate([one, c32, z32], axis=-1)
    sin_a = jnp.concatenate([zero, s32, z32], axis=-1)
    c64, s64 = tables(HEAD_DIM)
    cos_b = jnp.concatenate([c64, c64], axis=-1)
    sin_b = jnp.concatenate([s64, s64], axis=-1)
    return cos_a, sin_a, cos_b, sin_b


def _pad_heads(w, n_heads, dim):
    k = w.shape[0]
    w = w.reshape(k, n_heads, dim)
    return jnp.pad(w, ((0, 0), (0, 0), (0, LANES - dim))).reshape(k, n_heads * LANES)


def _prep_ab(w_in, w_uq, w_ukv):
    cq = w_in[:, :MLA_Q_RANK]
    o = MLA_Q_RANK
    ckv = w_in[:, o:o + MLA_KV_RANK]
    o += MLA_KV_RANK
    kr = w_in[:, o:o + MLA_ROPE]
    o += MLA_ROPE
    qb = w_in[:, o:o + SWA_HEADS * HEAD_DIM]
    qb = qb.reshape(D_MODEL, SWA_HEADS, HEAD_DIM)[:, list(SWA_HEAD_ORDER)].reshape(D_MODEL, SWA_HEADS * HEAD_DIM)
    o += SWA_HEADS * HEAD_DIM
    kb = w_in[:, o:o + SWA_KV_HEADS * HEAD_DIM]
    o += SWA_KV_HEADS * HEAD_DIM
    vb = w_in[:, o:]
    kr_rot = _rot_cols(kr, 1, MLA_ROPE)
    lead = jnp.zeros((D_MODEL, MLA_NOPE), F32)
    tail = jnp.zeros((D_MODEL, LANES - MLA_NOPE - MLA_ROPE), F32)
    kr_pad = jnp.concatenate([lead, kr, tail], axis=1)
    kr_rot_pad = jnp.concatenate([lead, kr_rot, tail], axis=1)
    w_big = jnp.concatenate([cq, ckv, qb, _rot_cols(qb, SWA_HEADS, HEAD_DIM), kb,
                             _rot_cols(kb, SWA_KV_HEADS, HEAD_DIM), vb, kr_pad, kr_rot_pad], axis=1)
    dq = MLA_NOPE + MLA_ROPE
    wq = w_uq.reshape(MLA_Q_RANK, MLA_HEADS, dq)
    rope = wq[..., MLA_NOPE:]
    rope_rot = jnp.concatenate([-rope[..., MLA_ROPE // 2:], rope[..., :MLA_ROPE // 2]], axis=-1)
    wq_rot = jnp.concatenate([jnp.zeros_like(wq[..., :MLA_NOPE]), rope_rot], axis=-1)
    wq_pad = _pad_heads(wq.reshape(MLA_Q_RANK, -1), MLA_HEADS, dq)
    wq_rot_pad = _pad_heads(wq_rot.reshape(MLA_Q_RANK, -1), MLA_HEADS, dq)
    w_q = jnp.concatenate([wq_pad, wq_rot_pad], axis=1)
    wkv = w_ukv.reshape(MLA_KV_RANK, MLA_HEADS, MLA_NOPE + MLA_V)
    wk_pad = _pad_heads(wkv[..., :MLA_NOPE].reshape(MLA_KV_RANK, -1), MLA_HEADS, MLA_NOPE)
    wv = wkv[..., MLA_NOPE:].reshape(MLA_KV_RANK, MLA_HEADS * MLA_V)
    w_kv = jnp.concatenate([wk_pad, wv], axis=1)
    return w_big.astype(BF16), w_q.astype(BF16), w_kv.astype(BF16)


def _permute_swa_rows(w_out):
    n = MLA_HEADS * MLA_V
    wb = w_out[n:].reshape(SWA_HEADS, HEAD_DIM, D_MODEL)[jnp.array(SWA_HEAD_ORDER)]
    return jnp.concatenate([w_out[:n], wb.reshape(SWA_HEADS * HEAD_DIM, D_MODEL)], axis=0)


def _prep_cd(w_in):
    n = DIL_HEADS * HEAD_DIM
    q_c, k_c, v_c, q_d, k_d, v_d = [w_in[:, i * n:(i + 1) * n] for i in range(6)]
    w_big = jnp.concatenate([q_c, _rot_cols(q_c, DIL_HEADS, HEAD_DIM), k_c, _rot_cols(k_c, DIL_HEADS, HEAD_DIM),
                             v_c, q_d, k_d, v_d], axis=1)
    return w_big.astype(BF16)


def _pre_ab_kernel(x_ref, g_ref, wbig_ref, qn_ref, wq_ref, kvn_ref, wkv_ref, ca_ref, sa_ref, cb_ref, sb_ref,
                   qm_ref, km_ref, vm_ref, qs_ref, ks_ref, vs_ref):
    h = _rms(x_ref[...], g_ref[...]).astype(BF16)
    proj = _dot(h, wbig_ref[...])
    ca, sa, cb, sb = ca_ref[...], sa_ref[...], cb_ref[...], sb_ref[...]
    o = 0
    c_q = proj[:, o:o + MLA_Q_RANK]
    o += MLA_Q_RANK
    c_kv = proj[:, o:o + MLA_KV_RANK]
    o += MLA_KV_RANK
    nq = SWA_HEADS * HEAD_DIM
    q_b, q_b_rot = proj[:, o:o + nq], proj[:, o + nq:o + 2 * nq]
    o += 2 * nq
    k_b, k_b_rot, v_b = proj[:, o:o + LANES], proj[:, o + LANES:o + 2 * LANES], proj[:, o + 2 * LANES:o + 3 * LANES]
    o += 3 * LANES
    k_r, k_r_rot = proj[:, o:o + LANES], proj[:, o + LANES:o + 2 * LANES]

    c_swa = HEAD_DIM ** -0.5 * LOG2E
    qs_ref[...] = jnp.concatenate(
        [(q_b[:, i * LANES:(i + 1) * LANES] * cb + q_b_rot[:, i * LANES:(i + 1) * LANES] * sb) * c_swa
         for i in range(nq // LANES)], axis=1).astype(BF16)
    ks_ref[...] = (k_b * cb + k_b_rot * sb).astype(BF16)
    vs_ref[...] = v_b.astype(BF16)

    q_all = _dot(_rms(c_q, qn_ref[...]).astype(BF16), wq_ref[...])
    nh = MLA_HEADS * LANES
    c_mla = (MLA_NOPE + MLA_ROPE) ** -0.5 * LOG2E
    qm_ref[...] = jnp.concatenate(
        [(q_all[:, i * LANES:(i + 1) * LANES] * ca + q_all[:, nh + i * LANES:nh + (i + 1) * LANES] * sa) * c_mla
         for i in range(MLA_HEADS)], axis=1).astype(BF16)
    kv = _dot(_rms(c_kv, kvn_ref[...]).astype(BF16), wkv_ref[...])
    k_rope = k_r * ca + k_r_rot * sa
    km_ref[...] = jnp.concatenate(
        [kv[:, i * LANES:(i + 1) * LANES] + k_rope for i in range(MLA_HEADS)], axis=1).astype(BF16)
    vm_ref[...] = kv[:, nh:].astype(BF16)


def _pre_ab(x, g, w_big, qn, w_q, kvn, w_kv, tabs, tm=512):
    t = x.shape[0]
    nsb = SEQ // tm
    full = lambda a: pl.BlockSpec(a.shape, lambda i: (0,) * a.ndim)
    tab = pl.BlockSpec((tm, LANES), lambda i: (i % nsb, 0))
    row = lambda n: pl.BlockSpec((tm, n), lambda i: (i, 0))
    outs = [(MLA_HEADS * LANES, BF16), (MLA_HEADS * LANES, BF16), (MLA_HEADS * MLA_V, BF16),
            (SWA_HEADS * HEAD_DIM, BF16), (LANES, BF16), (LANES, BF16)]
    return pl.pallas_call(
        _pre_ab_kernel,
        grid=(t // tm,),
        in_specs=[row(D_MODEL), full(g), full(w_big), full(qn), full(w_q), full(kvn), full(w_kv), tab, tab, tab, tab],
        out_specs=[row(n) for n, _ in outs],
        out_shape=[jax.ShapeDtypeStruct((t, n), d) for n, d in outs],
        compiler_params=_cparams("parallel"),
        name="pre_ab",
    )(x, g, w_big, qn, w_q, kvn, w_kv, *tabs)


def _pre_cd_kernel(x_ref, g_ref, wbig_ref, cb_ref, sb_ref, qc_ref, kc_ref, vc_ref, qd_ref, kd_ref, vd_ref):
    h = _rms(x_ref[...], g_ref[...]).astype(BF16)
    cb, sb = cb_ref[...], sb_ref[...]
    n = DIL_HEADS * HEAD_DIM
    w = wbig_ref

    c_att = HEAD_DIM ** -0.5 * LOG2E

    def roped(o, c):
        a = _dot(h, w[:, o:o + n])
        b = _dot(h, w[:, o + n:o + 2 * n])
        return jnp.concatenate([(a[:, i * LANES:(i + 1) * LANES] * cb + b[:, i * LANES:(i + 1) * LANES] * sb) * c
                                for i in range(n // LANES)], axis=1).astype(BF16)

    qc_ref[...] = roped(0, c_att)
    kc_ref[...] = roped(2 * n, 1.0)
    vc_ref[...] = _dot(h, w[:, 4 * n:5 * n]).astype(BF16)
    qd_ref[...] = (_dot(h, w[:, 5 * n:6 * n]) * c_att).astype(BF16)
    kd_ref[...] = _dot(h, w[:, 6 * n:7 * n]).astype(BF16)
    vd_ref[...] = _dot(h, w[:, 7 * n:8 * n]).astype(BF16)


def _pre_cd(x, g, w_big, tabs, tm=512):
    t = x.shape[0]
    nsb = SEQ // tm
    n = DIL_HEADS * HEAD_DIM
    full = lambda a: pl.BlockSpec(a.shape, lambda i: (0,) * a.ndim)
    tab = pl.BlockSpec((tm, LANES), lambda i: (i % nsb, 0))
    row = lambda m: pl.BlockSpec((tm, m), lambda i: (i, 0))
    return pl.pallas_call(
        _pre_cd_kernel,
        grid=(t // tm,),
        in_specs=[row(D_MODEL), full(g), full(w_big), tab, tab],
        out_specs=[row(n)] * 6,
        out_shape=[jax.ShapeDtypeStruct((t, n), BF16)] * 6,
        compiler_params=_cparams("parallel"),
        name="pre_cd",
    )(x, g, w_big, tabs[2], tabs[3])


def _low_half():
    return lax.broadcasted_iota(jnp.int32, (1, LANES), 1) < LANES // 2


def _split_heads(x2, low):
    zero = jnp.zeros_like(x2)
    return jnp.where(low, x2, zero), jnp.where(low, zero, x2)


def _softmax_pv(s, v2):
    m = jnp.max(s, axis=-1, keepdims=True)
    e = jnp.exp2(s - m)
    l = jnp.sum(e, axis=-1, keepdims=True)
    return _dot(e.astype(BF16), v2) / l


def _mla_kernel(q_ref, k_ref, v_ref, o_ref):
    low = _low_half()
    outs = []
    for p in range(MLA_HEADS // 2):
        v2 = v_ref[0, :, p * LANES:(p + 1) * LANES]
        halves = [_softmax_pv(_dot_nt(q_ref[0, :, h * LANES:(h + 1) * LANES], k_ref[0, :, h * LANES:(h + 1) * LANES]), v2)
                  for h in (2 * p, 2 * p + 1)]
        outs.append(jnp.where(low, halves[0], halves[1]))
    o_ref[0] = jnp.concatenate(outs, axis=1).astype(o_ref.dtype)


def _mla_attn(q, k, v, tq=512):
    b = q.shape[0]
    nq, nv = q.shape[-1], v.shape[-1]
    return pl.pallas_call(
        _mla_kernel,
        grid=(b, SEQ // tq),
        in_specs=[pl.BlockSpec((1, tq, nq), lambda i, j: (i, j, 0)),
                  pl.BlockSpec((1, SEQ, nq), lambda i, j: (i, 0, 0)),
                  pl.BlockSpec((1, SEQ, nv), lambda i, j: (i, 0, 0))],
        out_specs=pl.BlockSpec((1, tq, nv), lambda i, j: (i, j, 0)),
        out_shape=jax.ShapeDtypeStruct((b, SEQ, nv), BF16),
        compiler_params=_cparams("parallel", "parallel"),
        name="mla_attn",
    )(q, k, v)


def _swa_kernel(q_ref, kp_ref, kc_ref, kn_ref, vp_ref, vc_ref, vn_ref, sink_ref, o_ref, *, blk):
    i = pl.program_id(1)
    low = _low_half()
    k = jnp.concatenate([kp_ref[0], kc_ref[0], kn_ref[0]], axis=0)
    v = jnp.concatenate([vp_ref[0], vc_ref[0], vn_ref[0]], axis=0)
    qpos = i * blk + lax.broadcasted_iota(jnp.int32, (blk, 3 * blk), 0)
    kpos = (i - 1) * blk + lax.broadcasted_iota(jnp.int32, (blk, 3 * blk), 1)
    valid = (jnp.abs(qpos - kpos) <= SWA_HALF_WINDOW) & (kpos >= 0) & (kpos < SEQ)
    mask = jnp.where(valid, 0.0, NEG_INF)
    tiles = []
    for r in range(SWA_GROUP):
        q = jnp.concatenate(_split_heads(q_ref[0, :, r * LANES:(r + 1) * LANES], low), axis=0)
        s = (_dot_nt(q, k).reshape(2, blk, 3 * blk) + mask[None]).reshape(2 * blk, 3 * blk)
        m = jnp.max(s, axis=-1, keepdims=True)
        e = jnp.exp2(s - m)
        l = jnp.sum(e, axis=-1, keepdims=True)
        lse = m * LN2 + jnp.log(l)
        keep = jax.nn.sigmoid(lse - sink_ref[2 * r * blk:(2 * r + 2) * blk])
        o = _dot(e.astype(BF16), v) * (keep / l)
        tiles.append(jnp.where(low, o[:blk], o[blk:]))
    o_ref[0] = jnp.concatenate(tiles, axis=1).astype(o_ref.dtype)


def _swa_attn(q, k, v, sink_rows, blk=SWA_BLOCK):
    b = q.shape[0]
    nb = SEQ // blk
    kv = lambda f: pl.BlockSpec((1, blk, LANES), f)
    prev = lambda i, j: (i, jnp.maximum(j - 1, 0), 0)
    cur = lambda i, j: (i, j, 0)
    nxt = lambda i, j: (i, jnp.minimum(j + 1, nb - 1), 0)
    return pl.pallas_call(
        functools.partial(_swa_kernel, blk=blk),
        grid=(b, nb),
        in_specs=[pl.BlockSpec((1, blk, SWA_HEADS * HEAD_DIM), cur), kv(prev), kv(cur), kv(nxt),
                  kv(prev), kv(cur), kv(nxt), pl.BlockSpec(sink_rows.shape, lambda i, j: (0, 0))],
        out_specs=pl.BlockSpec((1, blk, SWA_HEADS * HEAD_DIM), cur),
        out_shape=jax.ShapeDtypeStruct((b, SEQ, SWA_HEADS * HEAD_DIM), BF16),
        compiler_params=_cparams("parallel", "parallel"),
        name="swa_attn",
    )(q, k, k, k, v, v, v, sink_rows)


DIL_BLK = 128
DIL_WIN = 256


def _banded_blocks(q_blocks, k_wins, v_wins, offsets, half, low):
    n = len(q_blocks)
    win = k_wins[0].shape[0]
    q3 = jnp.stack([jnp.concatenate(_split_heads(q, low), axis=0) for q in q_blocks])
    k3, v3 = jnp.stack(k_wins), jnp.stack(v_wins)
    rel = (lax.broadcasted_iota(jnp.int32, (DIL_BLK, win), 0) - lax.broadcasted_iota(jnp.int32, (DIL_BLK, win), 1))
    masks = {off: jnp.where(jnp.abs(rel - off) <= half, 0.0, NEG_INF) for off in set(offsets)}
    mask3 = jnp.stack([masks[off] for off in offsets])
    s = jnp.einsum("bqd,bkd->bqk", q3, k3, preferred_element_type=F32)
    s = (s.reshape(n, 2, DIL_BLK, win) + mask3[:, None]).reshape(n, 2 * DIL_BLK, win)
    m = jnp.max(s, axis=-1, keepdims=True)
    e = jnp.exp2(s - m)
    l = jnp.sum(e, axis=-1, keepdims=True)
    o = jnp.einsum("bqk,bkd->bqd", e.astype(BF16), v3, preferred_element_type=F32)
    pick = lambda t: jnp.where(low, t[:, :DIL_BLK], t[:, DIL_BLK:])
    return pick(o), pick(jnp.broadcast_to(m, o.shape)), pick(jnp.broadcast_to(l, o.shape))


def _dilated_kernel(q_ref, k_ref, v_ref, o_ref, qf, kf, vf, *stats):
    low = _low_half()
    qf[...] = q_ref[0].astype(F32)
    kf[...] = k_ref[0].astype(F32)
    vf[...] = v_ref[0].astype(F32)
    for n, (window, dil) in enumerate(DIL_BRANCHES):
        length = SEQ // dil
        half = window // (2 * dil)
        win = min(DIL_WIN, length)
        per_stream = length // DIL_BLK
        starts = [min(max(a * DIL_BLK - half, 0), length - win) for a in range(per_stream)]
        q_blocks, k_wins, v_wins, offsets = [], [], [], []
        for r in range(dil):
            stream = lambda ref: ref[pl.ds(r, length, stride=dil), :].astype(BF16)
            qs, ks, vs = stream(qf), stream(kf), stream(vf)
            for a, w0 in enumerate(starts):
                q_blocks.append(qs[a * DIL_BLK:(a + 1) * DIL_BLK])
                k_wins.append(ks[w0:w0 + win])
                v_wins.append(vs[w0:w0 + win])
                offsets.append(w0 - a * DIL_BLK)
        tiles = _banded_blocks(q_blocks, k_wins, v_wins, offsets, half, low)
        for tile, dst in zip(tiles, stats[3 * n:3 * n + 3]):
            for r in range(dil):
                dst[pl.ds(r, length, stride=dil), :] = tile[r * per_stream:(r + 1) * per_stream].reshape(length, LANES)
    accs, ms, ls = stats[0::3], stats[1::3], stats[2::3]
    m_all = functools.reduce(jnp.maximum, [m[...] for m in ms])
    ws = [jnp.exp2(m[...] - m_all) for m in ms]
    num = sum(w * a[...] for w, a in zip(ws, accs))
    den = sum(w * l[...] for w, l in zip(ws, ls))
    o_ref[0] = (num / den).astype(o_ref.dtype)


def _dilated_attn(q, k, v):
    b = q.shape[0]
    spec = pl.BlockSpec((1, SEQ, LANES), lambda i, p: (i, 0, p))
    return pl.pallas_call(
        _dilated_kernel,
        grid=(b, DIL_HEADS // 2),
        in_specs=[spec, spec, spec],
        out_specs=spec,
        out_shape=jax.ShapeDtypeStruct((b, SEQ, DIL_HEADS * HEAD_DIM), BF16),
        scratch_shapes=[pltpu.VMEM((SEQ, LANES), F32)] * (3 + 3 * len(DIL_BRANCHES)),
        compiler_params=_cparams("parallel", "parallel"),
        name="dilated_attn",
    )(q, k, v)


NA_Q_ROWS = 4
NA_Q = NA_Q_ROWS * GRID_W
NA_K_ROWS = 12
NA_K = NA_K_ROWS * GRID_W
NA_GROUPS = SEQ // NA_Q
NA_RPB_ROWS = 2 * NA_KH - 1
NA_RPB_COLS = 2 * NA_KW - 1
NA_WIN0 = (lambda a: 0, lambda a: a, lambda a: NA_Q_ROWS)
NA_RHO0 = (NA_KH - 1, NA_KH - 1 - NA_Q_ROWS, NA_KH - 1 - 2 * NA_Q_ROWS)


def _na_bias_kernel(rpb_ref, o_ref):
    pair = pl.program_id(0)
    c = lax.broadcasted_iota(jnp.int32, (GRID_W, LANES), 0)
    lane = lax.broadcasted_iota(jnp.int32, (GRID_W, LANES), 1)
    odd = lane >= GRID_W
    cp = lane & (GRID_W - 1)
    c0 = jnp.clip(c - NA_KW // 2, 0, GRID_W - NA_KW)
    col_ok = (cp >= c0) & (cp < c0 + NA_KW)
    d = cp - c + (NA_KW - 1)
    neg = jnp.full((GRID_W, LANES), NEG_INF, F32)

    for a in range(2):
        head = 2 * pair + a
        cache = {}

        def tile(rho_even, ok_even, ok_odd):
            key = (rho_even, ok_even, ok_odd)
            if key not in cache:
                acc = neg
                for kk in range(NA_RPB_COLS):
                    lo = rpb_ref[(head * NA_RPB_ROWS + rho_even) * NA_RPB_COLS + kk] if ok_even else NEG_INF
                    hi = rpb_ref[(head * NA_RPB_ROWS + rho_even + 1) * NA_RPB_COLS + kk] if ok_odd else NEG_INF
                    acc = jnp.where(d == kk, jnp.where(odd, hi, lo), acc)
                cache[key] = jnp.where(col_ok, acc * LOG2E, NEG_INF)
            return cache[key]

        for pat in range(3):
            for qa in range(NA_Q_ROWS):
                w0 = NA_WIN0[pat](qa)
                tiles = []
                for j in range(0, NA_K_ROWS, 2):
                    ok_even = w0 <= j < w0 + NA_KH
                    ok_odd = w0 <= j + 1 < w0 + NA_KH
                    rho = j - qa + NA_RHO0[pat]
                    tiles.append(tile(rho, ok_even, ok_odd) if (ok_even or ok_odd) else neg)
                o_ref[pat, 0, a, qa * GRID_W:(qa + 1) * GRID_W, :] = jnp.concatenate(tiles, axis=1)


def _na_bias(rpb):
    return pl.pallas_call(
        _na_bias_kernel,
        grid=(NA_HEADS // 2,),
        in_specs=[pl.BlockSpec(memory_space=pltpu.SMEM)],
        out_specs=pl.BlockSpec((3, 1, 2, NA_Q, NA_K), lambda p: (0, p, 0, 0, 0)),
        out_shape=jax.ShapeDtypeStruct((3, NA_HEADS // 2, 2, NA_Q, NA_K), F32),
        compiler_params=_cparams("parallel"),
        name="na_bias",
    )(rpb.reshape(-1))


NA_STEP_PAIRS = 4


def _na_kernel(q_ref, k0_ref, k1_ref, k2_ref, v0_ref, v1_ref, v2_ref, bias_ref, o_ref):
    low = _low_half()
    outs = []
    for p in range(NA_STEP_PAIRS):
        sl = slice(p * LANES, (p + 1) * LANES)
        k = jnp.concatenate([k0_ref[0, :, sl], k1_ref[0, :, sl], k2_ref[0, :, sl]], axis=0)
        v = jnp.concatenate([v0_ref[0, :, sl], v1_ref[0, :, sl], v2_ref[0, :, sl]], axis=0)
        halves = [_softmax_pv(_dot_nt(q, k) + bias_ref[0, p, a], v)
                  for a, q in enumerate(_split_heads(q_ref[0, :, sl], low))]
        outs.append(jnp.where(low, halves[0], halves[1]))
    o_ref[0] = jnp.concatenate(outs, axis=1).astype(o_ref.dtype)


def _na_attn(q, k, v, bias):
    b = q.shape[0]
    width = NA_STEP_PAIRS * LANES
    first_block = lambda g: jnp.clip(g - 1, 0, NA_GROUPS - NA_K // NA_Q)
    pattern = lambda g: jnp.where(g == 0, 0, jnp.where(g == NA_GROUPS - 1, 2, 1))
    cur = pl.BlockSpec((1, NA_Q, width), lambda p, g, i: (i, g, p))
    kv = lambda n: pl.BlockSpec((1, NA_Q, width), lambda p, g, i: (i, first_block(g) + n, p))
    return pl.pallas_call(
        _na_kernel,
        grid=(NA_HEADS // 2 // NA_STEP_PAIRS, NA_GROUPS, b),
        in_specs=[cur, kv(0), kv(1), kv(2), kv(0), kv(1), kv(2),
                  pl.BlockSpec((1, NA_STEP_PAIRS, 2, NA_Q, NA_K), lambda p, g, i: (pattern(g), p, 0, 0, 0))],
        out_specs=cur,
        out_shape=jax.ShapeDtypeStruct((b, SEQ, NA_HEADS * HEAD_DIM), BF16),
        compiler_params=_cparams("parallel", "parallel", "parallel"),
        name="na_attn",
    )(q, k, k, k, v, v, v, bias)


def _post_kernel(oa_ref, ob_ref, x_ref, w1_ref, w2_ref, g_ref, wr_ref, xh_ref, aff_ref):
    x = x_ref[...] + _dot(oa_ref[...], w1_ref[...]) + _dot(ob_ref[...], w2_ref[...])
    h = _rms(x, g_ref[...])
    xh_ref[:, :D_MODEL] = x
    xh_ref[:, D_MODEL:] = h
    logits = _dot_nt(wr_ref[...], h.astype(BF16))
    e = jnp.exp(logits - jnp.max(logits, axis=0, keepdims=True))
    aff_ref[...] = e / jnp.sum(e, axis=0, keepdims=True)


def _post(oa, ob, x, w_out, g, w_router, tm=512):
    t = x.shape[0]
    half = oa.shape[1]
    w1, w2 = w_out[:half].astype(BF16), w_out[half:].astype(BF16)
    wr = w_router.T.astype(BF16)
    full = lambda a: pl.BlockSpec(a.shape, lambda i: (0,) * a.ndim)
    row = lambda n: pl.BlockSpec((tm, n), lambda i: (i, 0))
    return pl.pallas_call(
        _post_kernel,
        grid=(t // tm,),
        in_specs=[row(half), row(half), row(D_MODEL), full(w1), full(w2), full(g), full(wr)],
        out_specs=[row(2 * D_MODEL), pl.BlockSpec((N_EXPERTS, tm), lambda i: (0, i))],
        out_shape=[jax.ShapeDtypeStruct((t, 2 * D_MODEL), F32), jax.ShapeDtypeStruct((N_EXPERTS, t), F32)],
        compiler_params=_cparams("parallel"),
        name="post_attn",
    )(oa, ob, x, w1, w2, g, wr)


def _select_kernel(aff_ref, idx_ref, gate_ref, incl_s, a_hi, a_mid, a_lo, start_s, end_s, cnt_s, *, cap, tc):
    s_id = pl.program_id(1)
    rows = aff_ref.shape[1]

    @pl.when(s_id == 0)
    def _():
        aff = aff_ref[0]
        bits = pltpu.bitcast(aff, jnp.int32)

        def search(i, prefix):
            cand = prefix | jnp.left_shift(jnp.int32(1), 30 - i)
            n = jnp.sum((bits >= cand).astype(F32), keepdims=True)
            return jnp.where(n >= cap, cand, prefix)

        thr = lax.fori_loop(0, 31, search, jnp.zeros((1, 1), jnp.int32))
        gt = bits > thr
        eq = bits == thr
        need = cap - jnp.sum(gt.astype(F32), keepdims=True)
        li = lax.broadcasted_iota(jnp.int32, (LANES, LANES), 0)
        lj = lax.broadcasted_iota(jnp.int32, (LANES, LANES), 1)
        tri = (li <= lj).astype(BF16)
        ones = jnp.ones((LANES, LANES), BF16)
        ri = lax.broadcasted_iota(jnp.int32, (rows, rows), 0)
        rj = lax.broadcasted_iota(jnp.int32, (rows, rows), 1)
        below = (rj < ri).astype(BF16)
        above = (ri < rj).astype(BF16)

        eqb = eq.astype(BF16)
        eq_rank = _dot(below, _dot(eqb, ones).astype(BF16)) + _dot(eqb, tri) - eqb.astype(F32)
        sel = (gt | (eq & (eq_rank < need))).astype(BF16)
        incl_s[...] = _dot(sel, tri).astype(BF16)
        cnt = _dot_nt(jnp.ones((8, LANES), BF16), sel)
        start = _dot(cnt.astype(BF16), above)
        cnt_s[...] = cnt
        start_s[...] = start
        end_s[...] = start + cnt
        hi = aff.astype(BF16)
        r1 = aff - hi.astype(F32)
        mid = r1.astype(BF16)
        a_hi[...] = hi
        a_mid[...] = mid
        a_lo[...] = (r1 - mid.astype(F32)).astype(BF16)

    c = (s_id * tc + lax.broadcasted_iota(jnp.int32, (tc, rows), 0)).astype(F32)
    start, end, cnt = start_s[0:1, :], end_s[0:1, :], cnt_s[0:1, :]
    before = c >= end
    row_id = jnp.sum(before.astype(F32), axis=1, keepdims=True)
    row_start = jnp.sum(jnp.where(before, cnt, 0.0), axis=1, keepdims=True)
    onehot = ((c >= start) & (c < end)).astype(BF16)
    within = _dot(onehot, incl_s[...])
    k = c[:, 0:1] - row_start
    lane = jnp.sum((within <= k).astype(F32), axis=1, keepdims=True)
    idx_ref[0] = (row_id * LANES + lane).astype(jnp.int32)
    vals = _dot(onehot, a_hi[...]) + _dot(onehot, a_mid[...]) + _dot(onehot, a_lo[...])
    lane_iota = lax.broadcasted_iota(jnp.int32, (tc, LANES), 1).astype(F32)
    gate = jnp.sum(jnp.where(lane_iota == lane, vals, 0.0), axis=1, keepdims=True)
    gate_ref[0] = jnp.broadcast_to(gate, (tc, LANES))


def _select(aff_t, cap, tc=512):
    t = aff_t.shape[1]
    rows = t // LANES
    aff3 = aff_t.reshape(N_EXPERTS, rows, LANES)
    idx, gate = pl.pallas_call(
        functools.partial(_select_kernel, cap=cap, tc=tc),
        grid=(N_EXPERTS, cap // tc),
        in_specs=[pl.BlockSpec((1, rows, LANES), lambda e, s: (e, 0, 0))],
        out_specs=[pl.BlockSpec((1, tc, 1), lambda e, s: (e, s, 0)),
                   pl.BlockSpec((1, tc, LANES), lambda e, s: (e, s, 0))],
        out_shape=[jax.ShapeDtypeStruct((N_EXPERTS, cap, 1), jnp.int32),
                   jax.ShapeDtypeStruct((N_EXPERTS, cap, LANES), F32)],
        scratch_shapes=[pltpu.VMEM((rows, LANES), BF16)] * 4 + [pltpu.VMEM((8, rows), F32)] * 3,
        compiler_params=_cparams("parallel", "arbitrary"),
        name="ec_select",
    )(aff3)
    return idx.reshape(N_EXPERTS * cap), gate.reshape(N_EXPERTS * cap, LANES)


FFN_SUB = 4


def _ffn_kernel(idx_ref, gate_ref, wg_ref, wu_ref, wd_ref, xh_in, xh_hbm, *scratch, tc, n_steps):
    del xh_in
    bufs, (sem_g, sem_o) = scratch[:FFN_SUB], scratch[FFN_SUB:]
    e, s = pl.program_id(0), pl.program_id(1)
    step_base = (e * n_steps + s) * (FFN_SUB * tc)

    def start_gather(base, j):
        for i in range(tc):
            t = idx_ref[base + i]
            pltpu.make_async_copy(xh_hbm.at[pl.ds(t, 1)], bufs[j].at[pl.ds(i, 1)], sem_g.at[j]).start(priority=i % 2)

    def wait_gather(j):
        pltpu.make_async_copy(xh_hbm.at[pl.ds(0, tc)], bufs[j], sem_g.at[j]).wait()

    def start_scatter(base, j):
        for i in range(tc):
            t = idx_ref[base + i]
            pltpu.make_async_copy(bufs[j].at[pl.ds(i, 1), pl.ds(0, D_MODEL)],
                                  xh_hbm.at[pl.ds(t, 1), pl.ds(0, D_MODEL)], sem_o.at[j]).start(priority=i % 2)

    def wait_scatter(j):
        pltpu.make_async_copy(bufs[j].at[:, pl.ds(0, D_MODEL)],
                              xh_hbm.at[pl.ds(0, tc), pl.ds(0, D_MODEL)], sem_o.at[j]).wait()

    @pl.when(s == 0)
    def _():
        @pl.when(e > 0)
        def _():
            for j in range(FFN_SUB):
                wait_scatter(j)

        start_gather(step_base, 0)

    wait_gather(0)
    for j in range(FFN_SUB):
        if j + 1 < FFN_SUB:
            @pl.when(s > 0)
            def _():
                wait_scatter(j + 1)

            start_gather(step_base + (j + 1) * tc, j + 1)
        else:
            @pl.when(s + 1 < n_steps)
            def _():
                wait_scatter(0)
                start_gather(step_base + FFN_SUB * tc, 0)
        if j > 0:
            start_scatter(step_base + (j - 1) * tc, j - 1)

        h = bufs[j][:, D_MODEL:].astype(BF16)
        g = _dot(h, wg_ref[0])
        u = _dot(h, wu_ref[0])
        act = (g * jax.nn.sigmoid(g) * u).astype(BF16)
        y = _dot(act, wd_ref[0]) * gate_ref[j * tc:(j + 1) * tc, 0:1]
        bufs[j][:, :D_MODEL] = bufs[j][:, :D_MODEL] + y
        if j + 1 < FFN_SUB:
            wait_gather(j + 1)
    start_scatter(step_base + (FFN_SUB - 1) * tc, FFN_SUB - 1)

    @pl.when((e == N_EXPERTS - 1) & (s == n_steps - 1))
    def _():
        for j in range(FFN_SUB):
            wait_scatter(j)


def _ffn(idx, gate, xh, w_gate, w_up, w_down, cap, tc=256):
    n_steps = cap // (FFN_SUB * tc)
    assert n_steps * FFN_SUB * tc == cap
    wspec = pl.BlockSpec((1, D_MODEL, D_MODEL), lambda e, s, idx_ref: (e, 0, 0))
    any_spec = pl.BlockSpec(memory_space=pl.ANY)
    grid_spec = pltpu.PrefetchScalarGridSpec(
        num_scalar_prefetch=1,
        grid=(N_EXPERTS, n_steps),
        in_specs=[pl.BlockSpec((FFN_SUB * tc, LANES), lambda e, s, idx_ref: (e * n_steps + s, 0)),
                  wspec, wspec, wspec, any_spec],
        out_specs=any_spec,
        scratch_shapes=[pltpu.VMEM((tc, 2 * D_MODEL), F32)] * FFN_SUB
        + [pltpu.SemaphoreType.DMA((FFN_SUB,)), pltpu.SemaphoreType.DMA((FFN_SUB,))],
    )
    return pl.pallas_call(
        functools.partial(_ffn_kernel, tc=tc, n_steps=n_steps),
        grid_spec=grid_spec,
        out_shape=jax.ShapeDtypeStruct(xh.shape, F32),
        input_output_aliases={5: 0},
        compiler_params=_cparams("arbitrary", "arbitrary"),
        name="ec_ffn",
    )(idx, gate, w_gate, w_up, w_down, xh)


def _final_kernel(x_ref, g_ref, o_ref):
    o_ref[...] = _rms(x_ref[...], g_ref[...])


def _final_norm(x, g, tm=1024):
    t = x.shape[0]
    return pl.pallas_call(
        _final_kernel,
        grid=(t // tm,),
        in_specs=[pl.BlockSpec((tm, D_MODEL), lambda i: (i, 0)), pl.BlockSpec(g.shape, lambda i: (0, 0))],
        out_specs=pl.BlockSpec((tm, D_MODEL), lambda i: (i, 0)),
        out_shape=jax.ShapeDtypeStruct((t, D_MODEL), F32),
        compiler_params=_cparams("parallel"),
        name="final_norm",
    )(x, g)


def _moe(xh, aff_t, w_gate, w_up, w_down):
    cap = EC_CAPACITY_FACTOR * xh.shape[0] // N_EXPERTS
    idx, gate = _select(aff_t, cap)
    return _ffn(idx, gate, xh, w_gate, w_up, w_down, cap)


def _trunk(x3, p):
    b = x3.shape[0]
    t = b * SEQ
    x = x3.reshape(t, D_MODEL)
    seq = lambda a: a.reshape(b, SEQ, a.shape[-1])
    flat = lambda a: a.reshape(t, a.shape[-1])

    qm, km, vm, qs, ks, vs = _pre_ab(x, p["g_mix0"], p["w_big_ab"], p["q_norm"], p["w_q"], p["kv_norm"], p["w_kv"],
                                     p["tabs"])
    o_a = _mla_attn(seq(qm), seq(km), seq(vm))
    o_b = _swa_attn(seq(qs), seq(ks), seq(vs), p["sink"])
    xh, aff_t = _post(flat(o_a), flat(o_b), x, p["w_out_ab"], p["g_ffn0"], p["w_router0"])
    x = _moe(xh, aff_t, p["w_gate0"], p["w_up0"], p["w_down0"])

    qc, kc, vc, qd, kd, vd = _pre_cd(x, p["g_mix1"], p["w_big_cd"], p["tabs"])
    o_c = _dilated_attn(seq(qc), seq(kc), seq(vc))
    o_d = _na_attn(seq(qd), seq(kd), seq(vd), p["na_bias"])
    xh, aff_t = _post(flat(o_c), flat(o_d), x, p["w_out_cd"], p["g_ffn1"], p["w_router1"])
    x = _moe(xh, aff_t, p["w_gate1"], p["w_up1"], p["w_down1"])
    return _final_norm(x, p["g_final"]).reshape(b, SEQ, D_MODEL)


def kernel(x_prompt, x_sample, norm_mix, norm_ffn, norm_final, w_in_ab, mla_q_norm, mla_w_uq, mla_kv_norm,
           mla_w_ukv, swa_sink, w_out_ab, w_in_cd, na_rpb, w_out_cd, w_router, w_gate, w_up, w_down):
    w_big_ab, w_q, w_kv = _prep_ab(w_in_ab[0], mla_w_uq[0], mla_w_ukv[0])
    p = {
        "tabs": _rope_tables(),
        "g_mix0": norm_mix[0:1], "g_mix1": norm_mix[1:2], "g_ffn0": norm_ffn[0:1], "g_ffn1": norm_ffn[1:2],
        "g_final": norm_final.reshape(1, D_MODEL),
        "w_big_ab": w_big_ab, "w_q": w_q, "w_kv": w_kv,
        "q_norm": mla_q_norm[0:1], "kv_norm": mla_kv_norm[0:1],
        "sink": jnp.repeat(swa_sink[0][jnp.array(SWA_HEAD_ORDER)], SWA_BLOCK).reshape(SWA_HEADS * SWA_BLOCK, 1),
        "w_out_ab": _permute_swa_rows(w_out_ab[0]), "w_big_cd": _prep_cd(w_in_cd[0]), "w_out_cd": w_out_cd[0],
        "na_bias": _na_bias(na_rpb[0]),
        "w_router0": w_router[0], "w_router1": w_router[1],
    }
    for l in range(2):
        p[f"w_gate{l}"] = w_gate[l].astype(BF16)
        p[f"w_up{l}"] = w_up[l].astype(BF16)
        p[f"w_down{l}"] = w_down[l].astype(BF16)
    return _trunk(x_prompt, p), _trunk(x_sample, p)
```

```python
import functools

import jax
import jax.numpy as jnp
from jax import lax
from jax.experimental import pallas as pl
from jax.experimental.pallas import tpu as pltpu

D_MODEL = 1024
SEQ = 2048
HEAD_DIM = 64
ROPE_THETA = 10000.0
NEG_INF = -1e30
RMS_EPS = 1e-6

MLA_HEADS = 8
MLA_Q_RANK = 384
MLA_KV_RANK = 256
MLA_NOPE = 64
MLA_ROPE = 32
MLA_V = 64

SWA_HEADS = 8
SWA_KV_HEADS = 2
SWA_GROUP = SWA_HEADS // SWA_KV_HEADS
SWA_HALF_WINDOW = 128
SWA_BLOCK = 128

DIL_HEADS = 8
DIL_BRANCHES = ((128, 1), (512, 4), (2048, 16))

NA_HEADS = 8
GRID_W = 64
NA_KH = 8
NA_KW = 16

N_EXPERTS = 16
EC_CAPACITY_FACTOR = 2

SWA_HEAD_ORDER = tuple(g * SWA_GROUP + r for r in range(SWA_GROUP) for g in range(SWA_KV_HEADS))
LOG2E = 1.4426950408889634
LN2 = 0.6931471805599453

LANES = 128
VMEM_LIMIT = 56 * 1024 * 1024

BF16 = jnp.bfloat16
F32 = jnp.float32


def _cparams(*sem):
    return pltpu.CompilerParams(dimension_semantics=sem, vmem_limit_bytes=VMEM_LIMIT)


def _dot(a, b):
    return jnp.dot(a, b, preferred_element_type=F32)


def _dot_nt(a, b):
    return lax.dot_general(a, b, (((1,), (1,)), ((), ())), preferred_element_type=F32)


def _rms(x, g):
    return x * lax.rsqrt(jnp.mean(x * x, axis=-1, keepdims=True) + RMS_EPS) * g


def _rot_cols(w, n_heads, dim):
    k = w.shape[0]
    w = w.reshape(k, n_heads, dim)
    return jnp.concatenate([-w[..., dim // 2:], w[..., :dim // 2]], axis=-1).reshape(k, n_heads * dim)


def _rope_tables():
    def tables(dim):
        inv = 1.0 / (ROPE_THETA ** (jnp.arange(0, dim, 2, dtype=F32) / dim))
        ang = jnp.arange(SEQ, dtype=F32)[:, None] * inv[None, :]
        c, s = jnp.cos(ang), jnp.sin(ang)
        return jnp.concatenate([c, c], axis=-1), jnp.concatenate([s, s], axis=-1)

    c32, s32 = tables(MLA_ROPE)
    one = jnp.ones((SEQ, MLA_NOPE), F32)
    zero = jnp.zeros((SEQ, MLA_NOPE), F32)
    z32 = jnp.zeros((SEQ, LANES - MLA_NOPE - MLA_ROPE), F32)
    cos_a = jnp.concatenate([one, c32, z32], axis=-1)
    sin_a = jnp.concatenate([zero, s32, z32], axis=-1)
    c64, s64 = tables(HEAD_DIM)
    cos_b = jnp.concatenate([c64, c64], axis=-1)
    sin_b = jnp.concatenate([s64, s64], axis=-1)
    return cos_a, sin_a, cos_b, sin_b


def _pad_heads(w, n_heads, dim):
    k = w.shape[0]
    w = w.reshape(k, n_heads, dim)
    return jnp.pad(w, ((0, 0), (0, 0), (0, LANES - dim))).reshape(k, n_heads * LANES)


def _prep_ab(w_in, w_uq, w_ukv):
    cq = w_in[:, :MLA_Q_RANK]
    o = MLA_Q_RANK
    ckv = w_in[:, o:o + MLA_KV_RANK]
    o += MLA_KV_RANK
    kr = w_in[:, o:o + MLA_ROPE]
    o += MLA_ROPE
    qb = w_in[:, o:o + SWA_HEADS * HEAD_DIM]
    qb = qb.reshape(D_MODEL, SWA_HEADS, HEAD_DIM)[:, list(SWA_HEAD_ORDER)].reshape(D_MODEL, SWA_HEADS * HEAD_DIM)
    o += SWA_HEADS * HEAD_DIM
    kb = w_in[:, o:o + SWA_KV_HEADS * HEAD_DIM]
    o += SWA_KV_HEADS * HEAD_DIM
    vb = w_in[:, o:]
    kr_rot = _rot_cols(kr, 1, MLA_ROPE)
    lead = jnp.zeros((D_MODEL, MLA_NOPE), F32)
    tail = jnp.zeros((D_MODEL, LANES - MLA_NOPE - MLA_ROPE), F32)
    kr_pad = jnp.concatenate([lead, kr, tail], axis=1)
    kr_rot_pad = jnp.concatenate([lead, kr_rot, tail], axis=1)
    w_big = jnp.concatenate([cq, ckv, qb, _rot_cols(qb, SWA_HEADS, HEAD_DIM), kb,
                             _rot_cols(kb, SWA_KV_HEADS, HEAD_DIM), vb, kr_pad, kr_rot_pad], axis=1)
    dq = MLA_NOPE + MLA_ROPE
    wq = w_uq.reshape(MLA_Q_RANK, MLA_HEADS, dq)
    rope = wq[..., MLA_NOPE:]
    rope_rot = jnp.concatenate([-rope[..., MLA_ROPE // 2:], rope[..., :MLA_ROPE // 2]], axis=-1)
    wq_rot = jnp.concatenate([jnp.zeros_like(wq[..., :MLA_NOPE]), rope_rot], axis=-1)
    wq_pad = _pad_heads(wq.reshape(MLA_Q_RANK, -1), MLA_HEADS, dq)
    wq_rot_pad = _pad_heads(wq_rot.reshape(MLA_Q_RANK, -1), MLA_HEADS, dq)
    w_q = jnp.concatenate([wq_pad, wq_rot_pad], axis=1)
    wkv = w_ukv.reshape(MLA_KV_RANK, MLA_HEADS, MLA_NOPE + MLA_V)
    wk_pad = _pad_heads(wkv[..., :MLA_NOPE].reshape(MLA_KV_RANK, -1), MLA_HEADS, MLA_NOPE)
    wv = wkv[..., MLA_NOPE:].reshape(MLA_KV_RANK, MLA_HEADS * MLA_V)
    w_kv = jnp.concatenate([wk_pad, wv], axis=1)
    return w_big.astype(BF16), w_q.astype(BF16), w_kv.astype(BF16)


def _permute_swa_rows(w_out):
    n = MLA_HEADS * MLA_V
    wb = w_out[n:].reshape(SWA_HEADS, HEAD_DIM, D_MODEL)[jnp.array(SWA_HEAD_ORDER)]
    return jnp.concatenate([w_out[:n], wb.reshape(SWA_HEADS * HEAD_DIM, D_MODEL)], axis=0)


def _prep_cd(w_in):
    n = DIL_HEADS * HEAD_DIM
    q_c, k_c, v_c, q_d, k_d, v_d = [w_in[:, i * n:(i + 1) * n] for i in range(6)]
    w_big = jnp.concatenate([q_c, _rot_cols(q_c, DIL_HEADS, HEAD_DIM), k_c, _rot_cols(k_c, DIL_HEADS, HEAD_DIM),
                             v_c, q_d, k_d, v_d], axis=1)
    return w_big.astype(BF16)


def _pre_ab_kernel(x_ref, g_ref, wbig_ref, qn_ref, wq_ref, kvn_ref, wkv_ref, ca_ref, sa_ref, cb_ref, sb_ref,
                   qm_ref, km_ref, vm_ref, qs_ref, ks_ref, vs_ref):
    h = _rms(x_ref[...], g_ref[...]).astype(BF16)
    proj = _dot(h, wbig_ref[...])
    ca, sa, cb, sb = ca_ref[...], sa_ref[...], cb_ref[...], sb_ref[...]
    o = 0
    c_q = proj[:, o:o + MLA_Q_RANK]
    o += MLA_Q_RANK
    c_kv = proj[:, o:o + MLA_KV_RANK]
    o += MLA_KV_RANK
    nq = SWA_HEADS * HEAD_DIM
    q_b, q_b_rot = proj[:, o:o + nq], proj[:, o + nq:o + 2 * nq]
    o += 2 * nq
    k_b, k_b_rot, v_b = proj[:, o:o + LANES], proj[:, o + LANES:o + 2 * LANES], proj[:, o + 2 * LANES:o + 3 * LANES]
    o += 3 * LANES
    k_r, k_r_rot = proj[:, o:o + LANES], proj[:, o + LANES:o + 2 * LANES]

    c_swa = HEAD_DIM ** -0.5 * LOG2E
    qs_ref[...] = jnp.concatenate(
        [(q_b[:, i * LANES:(i + 1) * LANES] * cb + q_b_rot[:, i * LANES:(i + 1) * LANES] * sb) * c_swa
         for i in range(nq // LANES)], axis=1).astype(BF16)
    ks_ref[...] = (k_b * cb + k_b_rot * sb).astype(BF16)
    vs_ref[...] = v_b.astype(BF16)

    q_all = _dot(_rms(c_q, qn_ref[...]).astype(BF16), wq_ref[...])
    nh = MLA_HEADS * LANES
    c_mla = (MLA_NOPE + MLA_ROPE) ** -0.5 * LOG2E
    qm_ref[...] = jnp.concatenate(
        [(q_all[:, i * LANES:(i + 1) * LANES] * ca + q_all[:, nh + i * LANES:nh + (i + 1) * LANES] * sa) * c_mla
         for i in range(MLA_HEADS)], axis=1).astype(BF16)
    kv = _dot(_rms(c_kv, kvn_ref[...]).astype(BF16), wkv_ref[...])
    k_rope = k_r * ca + k_r_rot * sa
    km_ref[...] = jnp.concatenate(
        [kv[:, i * LANES:(i + 1) * LANES] + k_rope for i in range(MLA_HEADS)], axis=1).astype(BF16)
    vm_ref[...] = kv[:, nh:].astype(BF16)


def _pre_ab(x, g, w_big, qn, w_q, kvn, w_kv, tabs, tm=512):
    t = x.shape[0]
    nsb = SEQ // tm
    full = lambda a: pl.BlockSpec(a.shape, lambda i: (0,) * a.ndim)
    tab = pl.BlockSpec((tm, LANES), lambda i: (i % nsb, 0))
    row = lambda n: pl.BlockSpec((tm, n), lambda i: (i, 0))
    outs = [(MLA_HEADS * LANES, BF16), (MLA_HEADS * LANES, BF16), (MLA_HEADS * MLA_V, BF16),
            (SWA_HEADS * HEAD_DIM, BF16), (LANES, BF16), (LANES, BF16)]
    return pl.pallas_call(
        _pre_ab_kernel,
        grid=(t // tm,),
        in_specs=[row(D_MODEL), full(g), full(w_big), full(qn), full(w_q), full(kvn), full(w_kv), tab, tab, tab, tab],
        out_specs=[row(n) for n, _ in outs],
        out_shape=[jax.ShapeDtypeStruct((t, n), d) for n, d in outs],
        compiler_params=_cparams("parallel"),
        name="pre_ab",
    )(x, g, w_big, qn, w_q, kvn, w_kv, *tabs)


def _pre_cd_kernel(x_ref, g_ref, wbig_ref, cb_ref, sb_ref, qc_ref, kc_ref, vc_ref, qd_ref, kd_ref, vd_ref):
    h = _rms(x_ref[...], g_ref[...]).astype(BF16)
    cb, sb = cb_ref[...], sb_ref[...]
    n = DIL_HEADS * HEAD_DIM
    w = wbig_ref

    c_att = HEAD_DIM ** -0.5 * LOG2E

    def roped(o, c):
        a = _dot(h, w[:, o:o + n])
        b = _dot(h, w[:, o + n:o + 2 * n])
        return jnp.concatenate([(a[:, i * LANES:(i + 1) * LANES] * cb + b[:, i * LANES:(i + 1) * LANES] * sb) * c
                                for i in range(n // LANES)], axis=1).astype(BF16)

    qc_ref[...] = roped(0, c_att)
    kc_ref[...] = roped(2 * n, 1.0)
    vc_ref[...] = _dot(h, w[:, 4 * n:5 * n]).astype(BF16)
    qd_ref[...] = (_dot(h, w[:, 5 * n:6 * n]) * c_att).astype(BF16)
    kd_ref[...] = _dot(h, w[:, 6 * n:7 * n]).astype(BF16)
    vd_ref[...] = _dot(h, w[:, 7 * n:8 * n]).astype(BF16)


def _pre_cd(x, g, w_big, tabs, tm=512):
    t = x.shape[0]
    nsb = SEQ // tm
    n = DIL_HEADS * HEAD_DIM
    full = lambda a: pl.BlockSpec(a.shape, lambda i: (0,) * a.ndim)
    tab = pl.BlockSpec((tm, LANES), lambda i: (i % nsb, 0))
    row = lambda m: pl.BlockSpec((tm, m), lambda i: (i, 0))
    return pl.pallas_call(
        _pre_cd_kernel,
        grid=(t // tm,),
        in_specs=[row(D_MODEL), full(g), full(w_big), tab, tab],
        out_specs=[row(n)] * 6,
        out_shape=[jax.ShapeDtypeStruct((t, n), BF16)] * 6,
        compiler_params=_cparams("parallel"),
        name="pre_cd",
    )(x, g, w_big, tabs[2], tabs[3])


def _low_half():
    return lax.broadcasted_iota(jnp.int32, (1, LANES), 1) < LANES // 2


def _split_heads(x2, low):
    zero = jnp.zeros_like(x2)
    return jnp.where(low, x2, zero), jnp.where(low, zero, x2)


def _softmax_pv(s, v2):
    m = jnp.max(s, axis=-1, keepdims=True)
    e = jnp.exp2(s - m)
    l = jnp.sum(e, axis=-1, keepdims=True)
    return _dot(e.astype(BF16), v2) / l


def _mla_kernel(q_ref, k_ref, v_ref, o_ref):
    low = _low_half()
    outs = []
    for p in range(MLA_HEADS // 2):
        v2 = v_ref[0, :, p * LANES:(p + 1) * LANES]
        halves = [_softmax_pv(_dot_nt(q_ref[0, :, h * LANES:(h + 1) * LANES], k_ref[0, :, h * LANES:(h + 1) * LANES]), v2)
                  for h in (2 * p, 2 * p + 1)]
        outs.append(jnp.where(low, halves[0], halves[1]))
    o_ref[0] = jnp.concatenate(outs, axis=1).astype(o_ref.dtype)


def _mla_attn(q, k, v, tq=512):
    b = q.shape[0]
    nq, nv = q.shape[-1], v.shape[-1]
    return pl.pallas_call(
        _mla_kernel,
        grid=(b, SEQ // tq),
        in_specs=[pl.BlockSpec((1, tq, nq), lambda i, j: (i, j, 0)),
                  pl.BlockSpec((1, SEQ, nq), lambda i, j: (i, 0, 0)),
                  pl.BlockSpec((1, SEQ, nv), lambda i, j: (i, 0, 0))],
        out_specs=pl.BlockSpec((1, tq, nv), lambda i, j: (i, j, 0)),
        out_shape=jax.ShapeDtypeStruct((b, SEQ, nv), BF16),
        compiler_params=_cparams("parallel", "parallel"),
        name="mla_attn",
    )(q, k, v)


def _swa_kernel(q_ref, kp_ref, kc_ref, kn_ref, vp_ref, vc_ref, vn_ref, sink_ref, o_ref, *, blk):
    i = pl.program_id(1)
    low = _low_half()
    k = jnp.concatenate([kp_ref[0], kc_ref[0], kn_ref[0]], axis=0)
    v = jnp.concatenate([vp_ref[0], vc_ref[0], vn_ref[0]], axis=0)
    qpos = i * blk + lax.broadcasted_iota(jnp.int32, (blk, 3 * blk), 0)
    kpos = (i - 1) * blk + lax.broadcasted_iota(jnp.int32, (blk, 3 * blk), 1)
    valid = (jnp.abs(qpos - kpos) <= SWA_HALF_WINDOW) & (kpos >= 0) & (kpos < SEQ)
    mask = jnp.where(valid, 0.0, NEG_INF)
    tiles = []
    for r in range(SWA_GROUP):
        q = jnp.concatenate(_split_heads(q_ref[0, :, r * LANES:(r + 1) * LANES], low), axis=0)
        s = (_dot_nt(q, k).reshape(2, blk, 3 * blk) + mask[None]).reshape(2 * blk, 3 * blk)
        m = jnp.max(s, axis=-1, keepdims=True)
        e = jnp.exp2(s - m)
        l = jnp.sum(e, axis=-1, keepdims=True)
        lse = m * LN2 + jnp.log(l)
        keep = jax.nn.sigmoid(lse - sink_ref[2 * r * blk:(2 * r + 2) * blk])
        o = _dot(e.astype(BF16), v) * (keep / l)
        tiles.append(jnp.where(low, o[:blk], o[blk:]))
    o_ref[0] = jnp.concatenate(tiles, axis=1).astype(o_ref.dtype)


def _swa_attn(q, k, v, sink_rows, blk=SWA_BLOCK):
    b = q.shape[0]
    nb = SEQ // blk
    kv = lambda f: pl.BlockSpec((1, blk, LANES), f)
    prev = lambda i, j: (i, jnp.maximum(j - 1, 0), 0)
    cur = lambda i, j: (i, j, 0)
    nxt = lambda i, j: (i, jnp.minimum(j + 1, nb - 1), 0)
    return pl.pallas_call(
        functools.partial(_swa_kernel, blk=blk),
        grid=(b, nb),
        in_specs=[pl.BlockSpec((1, blk, SWA_HEADS * HEAD_DIM), cur), kv(prev), kv(cur), kv(nxt),
                  kv(prev), kv(cur), kv(nxt), pl.BlockSpec(sink_rows.shape, lambda i, j: (0, 0))],
        out_specs=pl.BlockSpec((1, blk, SWA_HEADS * HEAD_DIM), cur),
        out_shape=jax.ShapeDtypeStruct((b, SEQ, SWA_HEADS * HEAD_DIM), BF16),
        compiler_params=_cparams("parallel", "parallel"),
        name="swa_attn",
    )(q, k, k, k, v, v, v, sink_rows)


DIL_BLK = 128
DIL_WIN = 256


def _banded_blocks(q_blocks, k_wins, v_wins, offsets, half, low):
    n = len(q_blocks)
    win = k_wins[0].shape[0]
    q3 = jnp.stack([jnp.concatenate(_split_heads(q, low), axis=0) for q in q_blocks])
    k3, v3 = jnp.stack(k_wins), jnp.stack(v_wins)
    rel = (lax.broadcasted_iota(jnp.int32, (DIL_BLK, win), 0) - lax.broadcasted_iota(jnp.int32, (DIL_BLK, win), 1))
    masks = {off: jnp.where(jnp.abs(rel - off) <= half, 0.0, NEG_INF) for off in set(offsets)}
    mask3 = jnp.stack([masks[off] for off in offsets])
    s = jnp.einsum("bqd,bkd->bqk", q3, k3, preferred_element_type=F32)
    s = (s.reshape(n, 2, DIL_BLK, win) + mask3[:, None]).reshape(n, 2 * DIL_BLK, win)
    m = jnp.max(s, axis=-1, keepdims=True)
    e = jnp.exp2(s - m)
    l = jnp.sum(e, axis=-1, keepdims=True)
    o = jnp.einsum("bqk,bkd->bqd", e.astype(BF16), v3, preferred_element_type=F32)
    pick = lambda t: jnp.where(low, t[:, :DIL_BLK], t[:, DIL_BLK:])
    return pick(o), pick(jnp.broadcast_to(m, o.shape)), pick(jnp.broadcast_to(l, o.shape))


def _dilated_kernel(q_ref, k_ref, v_ref, o_ref, qf, kf, vf, *stats):
    low = _low_half()
    qf[...] = q_ref[0].astype(F32)
    kf[...] = k_ref[0].astype(F32)
    vf[...] = v_ref[0].astype(F32)
    for n, (window, dil) in enumerate(DIL_BRANCHES):
        length = SEQ // dil
        half = window // (2 * dil)
        win = min(DIL_WIN, length)
        per_stream = length // DIL_BLK
        starts = [min(max(a * DIL_BLK - half, 0), length - win) for a in range(per_stream)]
        q_blocks, k_wins, v_wins, offsets = [], [], [], []
        for r in range(dil):
            stream = lambda ref: ref[pl.ds(r, length, stride=dil), :].astype(BF16)
            qs, ks, vs = stream(qf), stream(kf), stream(vf)
            for a, w0 in enumerate(starts):
                q_blocks.append(qs[a * DIL_BLK:(a + 1) * DIL_BLK])
                k_wins.append(ks[w0:w0 + win])
                v_wins.append(vs[w0:w0 + win])
                offsets.append(w0 - a * DIL_BLK)
        tiles = _banded_blocks(q_blocks, k_wins, v_wins, offsets, half, low)
        for tile, dst in zip(tiles, stats[3 * n:3 * n + 3]):
            for r in range(dil):
                dst[pl.ds(r, length, stride=dil), :] = tile[r * per_stream:(r + 1) * per_stream].reshape(length, LANES)
    accs, ms, ls = stats[0::3], stats[1::3], stats[2::3]
    m_all = functools.reduce(jnp.maximum, [m[...] for m in ms])
    ws = [jnp.exp2(m[...] - m_all) for m in ms]
    num = sum(w * a[...] for w, a in zip(ws, accs))
    den = sum(w * l[...] for w, l in zip(ws, ls))
    o_ref[0] = (num / den).astype(o_ref.dtype)


def _dilated_attn(q, k, v):
    b = q.shape[0]
    spec = pl.BlockSpec((1, SEQ, LANES), lambda i, p: (i, 0, p))
    return pl.pallas_call(
        _dilated_kernel,
        grid=(b, DIL_HEADS // 2),
        in_specs=[spec, spec, spec],
        out_specs=spec,
        out_shape=jax.ShapeDtypeStruct((b, SEQ, DIL_HEADS * HEAD_DIM), BF16),
        scratch_shapes=[pltpu.VMEM((SEQ, LANES), F32)] * (3 + 3 * len(DIL_BRANCHES)),
        compiler_params=_cparams("parallel", "parallel"),
        name="dilated_attn",
    )(q, k, v)


NA_Q_ROWS = 4
NA_Q = NA_Q_ROWS * GRID_W
NA_K_ROWS = 12
NA_K = NA_K_ROWS * GRID_W
NA_GROUPS = SEQ // NA_Q
NA_RPB_ROWS = 2 * NA_KH - 1
NA_RPB_COLS = 2 * NA_KW - 1
NA_WIN0 = (lambda a: 0, lambda a: a, lambda a: NA_Q_ROWS)
NA_RHO0 = (NA_KH - 1, NA_KH - 1 - NA_Q_ROWS, NA_KH - 1 - 2 * NA_Q_ROWS)


def _na_bias_kernel(rpb_ref, o_ref):
    pair = pl.program_id(0)
    c = lax.broadcasted_iota(jnp.int32, (GRID_W, LANES), 0)
    lane = lax.broadcasted_iota(jnp.int32, (GRID_W, LANES), 1)
    odd = lane >= GRID_W
    cp = lane & (GRID_W - 1)
    c0 = jnp.clip(c - NA_KW // 2, 0, GRID_W - NA_KW)
    col_ok = (cp >= c0) & (cp < c0 + NA_KW)
    d = cp - c + (NA_KW - 1)
    neg = jnp.full((GRID_W, LANES), NEG_INF, F32)

    for a in range(2):
        head = 2 * pair + a
        cache = {}

        def tile(rho_even, ok_even, ok_odd):
            key = (rho_even, ok_even, ok_odd)
            if key not in cache:
                acc = neg
                for kk in range(NA_RPB_COLS):
                    lo = rpb_ref[(head * NA_RPB_ROWS + rho_even) * NA_RPB_COLS + kk] if ok_even else NEG_INF
                    hi = rpb_ref[(head * NA_RPB_ROWS + rho_even + 1) * NA_RPB_COLS + kk] if ok_odd else NEG_INF
                    acc = jnp.where(d == kk, jnp.where(odd, hi, lo), acc)
                cache[key] = jnp.where(col_ok, acc * LOG2E, NEG_INF)
            return cache[key]

        for pat in range(3):
            for qa in range(NA_Q_ROWS):
                w0 = NA_WIN0[pat](qa)
                tiles = []
                for j in range(0, NA_K_ROWS, 2):
                    ok_even = w0 <= j < w0 + NA_KH
                    ok_odd = w0 <= j + 1 < w0 + NA_KH
                    rho = j - qa + NA_RHO0[pat]
                    tiles.append(tile(rho, ok_even, ok_odd) if (ok_even or ok_odd) else neg)
                o_ref[pat, 0, a, qa * GRID_W:(qa + 1) * GRID_W, :] = jnp.concatenate(tiles, axis=1)


def _na_bias(rpb):
    return pl.pallas_call(
        _na_bias_kernel,
        grid=(NA_HEADS // 2,),
        in_specs=[pl.BlockSpec(memory_space=pltpu.SMEM)],
        out_specs=pl.BlockSpec((3, 1, 2, NA_Q, NA_K), lambda p: (0, p, 0, 0, 0)),
        out_shape=jax.ShapeDtypeStruct((3, NA_HEADS // 2, 2, NA_Q, NA_K), F32),
        compiler_params=_cparams("parallel"),
        name="na_bias",
    )(rpb.reshape(-1))


NA_STEP_PAIRS = 4


def _na_kernel(q_ref, k0_ref, k1_ref, k2_ref, v0_ref, v1_ref, v2_ref, bias_ref, o_ref):
    low = _low_half()
    outs = []
    for p in range(NA_STEP_PAIRS):
        sl = slice(p * LANES, (p + 1) * LANES)
        k = jnp.concatenate([k0_ref[0, :, sl], k1_ref[0, :, sl], k2_ref[0, :, sl]], axis=0)
        v = jnp.concatenate([v0_ref[0, :, sl], v1_ref[0, :, sl], v2_ref[0, :, sl]], axis=0)
        halves = [_softmax_pv(_dot_nt(q, k) + bias_ref[0, p, a], v)
                  for a, q in enumerate(_split_heads(q_ref[0, :, sl], low))]
        outs.append(jnp.where(low, halves[0], halves[1]))
    o_ref[0] = jnp.concatenate(outs, axis=1).astype(o_ref.dtype)


def _na_attn(q, k, v, bias):
    b = q.shape[0]
    width = NA_STEP_PAIRS * LANES
    first_block = lambda g: jnp.clip(g - 1, 0, NA_GROUPS - NA_K // NA_Q)
    pattern = lambda g: jnp.where(g == 0, 0, jnp.where(g == NA_GROUPS - 1, 2, 1))
    cur = pl.BlockSpec((1, NA_Q, width), lambda p, g, i: (i, g, p))
    kv = lambda n: pl.BlockSpec((1, NA_Q, width), lambda p, g, i: (i, first_block(g) + n, p))
    return pl.pallas_call(
        _na_kernel,
        grid=(NA_HEADS // 2 // NA_STEP_PAIRS, NA_GROUPS, b),
        in_specs=[cur, kv(0), kv(1), kv(2), kv(0), kv(1), kv(2),
                  pl.BlockSpec((1, NA_STEP_PAIRS, 2, NA_Q, NA_K), lambda p, g, i: (pattern(g), p, 0, 0, 0))],
        out_specs=cur,
        out_shape=jax.ShapeDtypeStruct((b, SEQ, NA_HEADS * HEAD_DIM), BF16),
        compiler_params=_cparams("parallel", "parallel", "parallel"),
        name="na_attn",
    )(q, k, k, k, v, v, v, bias)


def _post_kernel(oa_ref, ob_ref, x_ref, w1_ref, w2_ref, g_ref, wr_ref, xh_ref, aff_ref):
    x = x_ref[...] + _dot(oa_ref[...], w1_ref[...]) + _dot(ob_ref[...], w2_ref[...])
    h = _rms(x, g_ref[...])
    xh_ref[:, :D_MODEL] = x
    xh_ref[:, D_MODEL:] = h
    logits = _dot_nt(wr_ref[...], h.astype(BF16))
    e = jnp.exp(logits - jnp.max(logits, axis=0, keepdims=True))
    aff_ref[...] = e / jnp.sum(e, axis=0, keepdims=True)


def _post(oa, ob, x, w_out, g, w_router, tm=512):
    t = x.shape[0]
    half = oa.shape[1]
    w1, w2 = w_out[:half].astype(BF16), w_out[half:].astype(BF16)
    wr = w_router.T.astype(BF16)
    full = lambda a: pl.BlockSpec(a.shape, lambda i: (0,) * a.ndim)
    row = lambda n: pl.BlockSpec((tm, n), lambda i: (i, 0))
    return pl.pallas_call(
        _post_kernel,
        grid=(t // tm,),
        in_specs=[row(half), row(half), row(D_MODEL), full(w1), full(w2), full(g), full(wr)],
        out_specs=[row(2 * D_MODEL), pl.BlockSpec((N_EXPERTS, tm), lambda i: (0, i))],
        out_shape=[jax.ShapeDtypeStruct((t, 2 * D_MODEL), F32), jax.ShapeDtypeStruct((N_EXPERTS, t), F32)],
        compiler_params=_cparams("parallel"),
        name="post_attn",
    )(oa, ob, x, w1, w2, g, wr)


def _select_kernel(aff_ref, idx_ref, gate_ref, incl_s, a_hi, a_mid, a_lo, start_s, end_s, cnt_s, *, cap, tc):
    s_id = pl.program_id(1)
    rows = aff_ref.shape[1]

    @pl.when(s_id == 0)
    def _():
        aff = aff_ref[0]
        bits = pltpu.bitcast(aff, jnp.int32)

        def search(i, prefix):
            cand = prefix | jnp.left_shift(jnp.int32(1), 30 - i)
            n = jnp.sum((bits >= cand).astype(F32), keepdims=True)
            return jnp.where(n >= cap, cand, prefix)

        thr = lax.fori_loop(0, 31, search, jnp.zeros((1, 1), jnp.int32))
        gt = bits > thr
        eq = bits == thr
        need = cap - jnp.sum(gt.astype(F32), keepdims=True)
        li = lax.broadcasted_iota(jnp.int32, (LANES, LANES), 0)
        lj = lax.broadcasted_iota(jnp.int32, (LANES, LANES), 1)
        tri = (li <= lj).astype(BF16)
        ones = jnp.ones((LANES, LANES), BF16)
        ri = lax.broadcasted_iota(jnp.int32, (rows, rows), 0)
        rj = lax.broadcasted_iota(jnp.int32, (rows, rows), 1)
        below = (rj < ri).astype(BF16)
        above = (ri < rj).astype(BF16)

        eqb = eq.astype(BF16)
        eq_rank = _dot(below, _dot(eqb, ones).astype(BF16)) + _dot(eqb, tri) - eqb.astype(F32)
        sel = (gt | (eq & (eq_rank < need))).astype(BF16)
        incl_s[...] = _dot(sel, tri).astype(BF16)
        cnt = _dot_nt(jnp.ones((8, LANES), BF16), sel)
        start = _dot(cnt.astype(BF16), above)
        cnt_s[...] = cnt
        start_s[...] = start
        end_s[...] = start + cnt
        hi = aff.astype(BF16)
        r1 = aff - hi.astype(F32)
        mid = r1.astype(BF16)
        a_hi[...] = hi
        a_mid[...] = mid
        a_lo[...] = (r1 - mid.astype(F32)).astype(BF16)

    c = (s_id * tc + lax.broadcasted_iota(jnp.int32, (tc, rows), 0)).astype(F32)
    start, end, cnt = start_s[0:1, :], end_s[0:1, :], cnt_s[0:1, :]
    before = c >= end
    row_id = jnp.sum(before.astype(F32), axis=1, keepdims=True)
    row_start = jnp.sum(jnp.where(before, cnt, 0.0), axis=1, keepdims=True)
    onehot = ((c >= start) & (c < end)).astype(BF16)
    within = _dot(onehot, incl_s[...])
    k = c[:, 0:1] - row_start
    lane = jnp.sum((within <= k).astype(F32), axis=1, keepdims=True)
    idx_ref[0] = (row_id * LANES + lane).astype(jnp.int32)
    vals = _dot(onehot, a_hi[...]) + _dot(onehot, a_mid[...]) + _dot(onehot, a_lo[...])
    lane_iota = lax.broadcasted_iota(jnp.int32, (tc, LANES), 1).astype(F32)
    gate = jnp.sum(jnp.where(lane_iota == lane, vals, 0.0), axis=1, keepdims=True)
    gate_ref[0] = jnp.broadcast_to(gate, (tc, LANES))


def _select(aff_t, cap, tc=1024):
    t = aff_t.shape[1]
    rows = t // LANES
    aff3 = aff_t.reshape(N_EXPERTS, rows, LANES)
    idx, gate = pl.pallas_call(
        functools.partial(_select_kernel, cap=cap, tc=tc),
        grid=(N_EXPERTS, cap // tc),
        in_specs=[pl.BlockSpec((1, rows, LANES), lambda e, s: (e, 0, 0))],
        out_specs=[pl.BlockSpec((1, tc, 1), lambda e, s: (e, s, 0)),
                   pl.BlockSpec((1, tc, LANES), lambda e, s: (e, s, 0))],
        out_shape=[jax.ShapeDtypeStruct((N_EXPERTS, cap, 1), jnp.int32),
                   jax.ShapeDtypeStruct((N_EXPERTS, cap, LANES), F32)],
        scratch_shapes=[pltpu.VMEM((rows, LANES), BF16)] * 4 + [pltpu.VMEM((8, rows), F32)] * 3,
        compiler_params=_cparams("parallel", "arbitrary"),
        name="ec_select",
    )(aff3)
    return idx.reshape(N_EXPERTS * cap), gate.reshape(N_EXPERTS * cap, LANES)


FFN_SUB = 4


def _ffn_kernel(idx_ref, gate_ref, wg_ref, wu_ref, wd_ref, xh_in, xh_hbm, *scratch, tc, n_steps):
    del xh_in
    bufs, (sem_g, sem_o) = scratch[:FFN_SUB], scratch[FFN_SUB:]
    e, s = pl.program_id(0), pl.program_id(1)
    step_base = (e * n_steps + s) * (FFN_SUB * tc)

    def start_gather(base, j):
        for i in range(tc):
            t = idx_ref[base + i]
            pltpu.make_async_copy(xh_hbm.at[pl.ds(t, 1)], bufs[j].at[pl.ds(i, 1)], sem_g.at[j]).start(priority=i % 2)

    def wait_gather(j):
        pltpu.make_async_copy(xh_hbm.at[pl.ds(0, tc)], bufs[j], sem_g.at[j]).wait()

    def start_scatter(base, j):
        for i in range(tc):
            t = idx_ref[base + i]
            pltpu.make_async_copy(bufs[j].at[pl.ds(i, 1), pl.ds(0, D_MODEL)],
                                  xh_hbm.at[pl.ds(t, 1), pl.ds(0, D_MODEL)], sem_o.at[j]).start(priority=i % 2)

    def wait_scatter(j):
        pltpu.make_async_copy(bufs[j].at[:, pl.ds(0, D_MODEL)],
                              xh_hbm.at[pl.ds(0, tc), pl.ds(0, D_MODEL)], sem_o.at[j]).wait()

    @pl.when(s == 0)
    def _():
        @pl.when(e > 0)
        def _():
            for j in range(FFN_SUB):
                wait_scatter(j)

        start_gather(step_base, 0)

    wait_gather(0)
    for j in range(FFN_SUB):
        if j + 1 < FFN_SUB:
            @pl.when(s > 0)
            def _():
                wait_scatter(j + 1)

            start_gather(step_base + (j + 1) * tc, j + 1)
        else:
            @pl.when(s + 1 < n_steps)
            def _():
                wait_scatter(0)
                start_gather(step_base + FFN_SUB * tc, 0)
        if j > 0:
            start_scatter(step_base + (j - 1) * tc, j - 1)

        h = bufs[j][:, D_MODEL:].astype(BF16)
        g = _dot(h, wg_ref[0])
        u = _dot(h, wu_ref[0])
        act = (g * jax.nn.sigmoid(g) * u).astype(BF16)
        y = _dot(act, wd_ref[0]) * gate_ref[j * tc:(j + 1) * tc, 0:1]
        bufs[j][:, :D_MODEL] = bufs[j][:, :D_MODEL] + y
        if j + 1 < FFN_SUB:
            wait_gather(j + 1)
    start_scatter(step_base + (FFN_SUB - 1) * tc, FFN_SUB - 1)

    @pl.when((e == N_EXPERTS - 1) & (s == n_steps - 1))
    def _():
        for j in range(FFN_SUB):
            wait_scatter(j)


def _ffn(idx, gate, xh, w_gate, w_up, w_down, cap, tc=256):
    n_steps = cap // (FFN_SUB * tc)
    assert n_steps * FFN_SUB * tc == cap
    wspec = pl.BlockSpec((1, D_MODEL, D_MODEL), lambda e, s, idx_ref: (e, 0, 0))
    any_spec = pl.BlockSpec(memory_space=pl.ANY)
    grid_spec = pltpu.PrefetchScalarGridSpec(
        num_scalar_prefetch=1,
        grid=(N_EXPERTS, n_steps),
        in_specs=[pl.BlockSpec((FFN_SUB * tc, LANES), lambda e, s, idx_ref: (e * n_steps + s, 0)),
                  wspec, wspec, wspec, any_spec],
        out_specs=any_spec,
        scratch_shapes=[pltpu.VMEM((tc, 2 * D_MODEL), F32)] * FFN_SUB
        + [pltpu.SemaphoreType.DMA((FFN_SUB,)), pltpu.SemaphoreType.DMA((FFN_SUB,))],
    )
    return pl.pallas_call(
        functools.partial(_ffn_kernel, tc=tc, n_steps=n_steps),
        grid_spec=grid_spec,
        out_shape=jax.ShapeDtypeStruct(xh.shape, F32),
        input_output_aliases={5: 0},
        compiler_params=_cparams("arbitrary", "arbitrary"),
        name="ec_ffn",
    )(idx, gate, w_gate, w_up, w_down, xh)


def _final_kernel(x_ref, g_ref, o_ref):
    o_ref[...] = _rms(x_ref[...], g_ref[...])


def _final_norm(x, g, tm=1024):
    t = x.shape[0]
    return pl.pallas_call(
        _final_kernel,
        grid=(t // tm,),
        in_specs=[pl.BlockSpec((tm, D_MODEL), lambda i: (i, 0)), pl.BlockSpec(g.shape, lambda i: (0, 0))],
        out_specs=pl.BlockSpec((tm, D_MODEL), lambda i: (i, 0)),
        out_shape=jax.ShapeDtypeStruct((t, D_MODEL), F32),
        compiler_params=_cparams("parallel"),
        name="final_norm",
    )(x, g)


def _moe(xh, aff_t, w_gate, w_up, w_down):
    cap = EC_CAPACITY_FACTOR * xh.shape[0] // N_EXPERTS
    idx, gate = _select(aff_t, cap)
    return _ffn(idx, gate, xh, w_gate, w_up, w_down, cap)


def _trunk(x3, p):
    b = x3.shape[0]
    t = b * SEQ
    x = x3.reshape(t, D_MODEL)
    seq = lambda a: a.reshape(b, SEQ, a.shape[-1])
    flat = lambda a: a.reshape(t, a.shape[-1])

    qm, km, vm, qs, ks, vs = _pre_ab(x, p["g_mix0"], p["w_big_ab"], p["q_norm"], p["w_q"], p["kv_norm"], p["w_kv"],
                                     p["tabs"])
    o_a = _mla_attn(seq(qm), seq(km), seq(vm))
    o_b = _swa_attn(seq(qs), seq(ks), seq(vs), p["sink"])
    xh, aff_t = _post(flat(o_a), flat(o_b), x, p["w_out_ab"], p["g_ffn0"], p["w_router0"])
    x = _moe(xh, aff_t, p["w_gate0"], p["w_up0"], p["w_down0"])

    qc, kc, vc, qd, kd, vd = _pre_cd(x, p["g_mix1"], p["w_big_cd"], p["tabs"])
    o_c = _dilated_attn(seq(qc), seq(kc), seq(vc))
    o_d = _na_attn(seq(qd), seq(kd), seq(vd), p["na_bias"])
    xh, aff_t = _post(flat(o_c), flat(o_d), x, p["w_out_cd"], p["g_ffn1"], p["w_router1"])
    x = _moe(xh, aff_t, p["w_gate1"], p["w_up1"], p["w_down1"])
    return _final_norm(x, p["g_final"]).reshape(b, SEQ, D_MODEL)


def kernel(x_prompt, x_sample, norm_mix, norm_ffn, norm_final, w_in_ab, mla_q_norm, mla_w_uq, mla_kv_norm,
           mla_w_ukv, swa_sink, w_out_ab, w_in_cd, na_rpb, w_out_cd, w_router, w_gate, w_up, w_down):
    w_big_ab, w_q, w_kv = _prep_ab(w_in_ab[0], mla_w_uq[0], mla_w_ukv[0])
    p = {
        "tabs": _rope_tables(),
        "g_mix0": norm_mix[0:1], "g_mix1": norm_mix[1:2], "g_ffn0": norm_ffn[0:1], "g_ffn1": norm_ffn[1:2],
        "g_final": norm_final.reshape(1, D_MODEL),
        "w_big_ab": w_big_ab, "w_q": w_q, "w_kv": w_kv,
        "q_norm": mla_q_norm[0:1], "kv_norm": mla_kv_norm[0:1],
        "sink": jnp.repeat(swa_sink[0][jnp.array(SWA_HEAD_ORDER)], SWA_BLOCK).reshape(SWA_HEADS * SWA_BLOCK, 1),
        "w_out_ab": _permute_swa_rows(w_out_ab[0]), "w_big_cd": _prep_cd(w_in_cd[0]), "w_out_cd": w_out_cd[0],
        "na_bias": _na_bias(na_rpb[0]),
        "w_router0": w_router[0], "w_router1": w_router[1],
    }
    for l in range(2):
        p[f"w_gate{l}"] = w_gate[l].astype(BF16)
        p[f"w_up{l}"] = w_up[l].astype(BF16)
        p[f"w_down{l}"] = w_down[l].astype(BF16)
    return _trunk(x_prompt, p), _trunk(x_sample, p)
```

```python
import functools

import jax
import jax.numpy as jnp
from jax import lax
from jax.experimental import pallas as pl
from jax.experimental.pallas import tpu as pltpu

D_MODEL = 1024
SEQ = 2048
HEAD_DIM = 64
ROPE_THETA = 10000.0
NEG_INF = -1e30
RMS_EPS = 1e-6

MLA_HEADS = 8
MLA_Q_RANK = 384
MLA_KV_RANK = 256
MLA_NOPE = 64
MLA_ROPE = 32
MLA_V = 64

SWA_HEADS = 8
SWA_KV_HEADS = 2
SWA_GROUP = SWA_HEADS // SWA_KV_HEADS
SWA_HALF_WINDOW = 128
SWA_BLOCK = 128

DIL_HEADS = 8
DIL_BRANCHES = ((128, 1), (512, 4), (2048, 16))

NA_HEADS = 8
GRID_W = 64
NA_KH = 8
NA_KW = 16

N_EXPERTS = 16
EC_CAPACITY_FACTOR = 2

SWA_HEAD_ORDER = tuple(g * SWA_GROUP + r for r in range(SWA_GROUP) for g in range(SWA_KV_HEADS))
LOG2E = 1.4426950408889634
LN2 = 0.6931471805599453

LANES = 128
VMEM_LIMIT = 56 * 1024 * 1024

BF16 = jnp.bfloat16
F32 = jnp.float32


def _cparams(*sem):
    return pltpu.CompilerParams(dimension_semantics=sem, vmem_limit_bytes=VMEM_LIMIT)


def _dot(a, b):
    return jnp.dot(a, b, preferred_element_type=F32)


def _dot_nt(a, b):
    return lax.dot_general(a, b, (((1,), (1,)), ((), ())), preferred_element_type=F32)


def _rms(x, g):
    return x * lax.rsqrt(jnp.mean(x * x, axis=-1, keepdims=True) + RMS_EPS) * g


def _rot_cols(w, n_heads, dim):
    k = w.shape[0]
    w = w.reshape(k, n_heads, dim)
    return jnp.concatenate([-w[..., dim // 2:], w[..., :dim // 2]], axis=-1).reshape(k, n_heads * dim)


def _rope_tables():
    def tables(dim):
        inv = 1.0 / (ROPE_THETA ** (jnp.arange(0, dim, 2, dtype=F32) / dim))
        ang = jnp.arange(SEQ, dtype=F32)[:, None] * inv[None, :]
        c, s = jnp.cos(ang), jnp.sin(ang)
        return jnp.concatenate([c, c], axis=-1), jnp.concatenate([s, s], axis=-1)

    c32, s32 = tables(MLA_ROPE)
    one = jnp.ones((SEQ, MLA_NOPE), F32)
    zero = jnp.zeros((SEQ, MLA_NOPE), F32)
    z32 = jnp.zeros((SEQ, LANES - MLA_NOPE - MLA_ROPE), F32)
    cos_a = jnp.concatenate([one, c32, z32], axis=-1)
    sin_a = jnp.concatenate([zero, s32, z32], axis=-1)
    c64, s64 = tables(HEAD_DIM)
    cos_b = jnp.concatenate([c64, c64], axis=-1)
    sin_b = jnp.concatenate([s64, s64], axis=-1)
    return cos_a, sin_a, cos_b, sin_b


def _pad_heads(w, n_heads, dim):
    k = w.shape[0]
    w = w.reshape(k, n_heads, dim)
    return jnp.pad(w, ((0, 0), (0, 0), (0, LANES - dim))).reshape(k, n_heads * LANES)


def _prep_ab(w_in, w_uq, w_ukv):
    cq = w_in[:, :MLA_Q_RANK]
    o = MLA_Q_RANK
    ckv = w_in[:, o:o + MLA_KV_RANK]
    o += MLA_KV_RANK
    kr = w_in[:, o:o + MLA_ROPE]
    o += MLA_ROPE
    qb = w_in[:, o:o + SWA_HEADS * HEAD_DIM]
    qb = qb.reshape(D_MODEL, SWA_HEADS, HEAD_DIM)[:, list(SWA_HEAD_ORDER)].reshape(D_MODEL, SWA_HEADS * HEAD_DIM)
    o += SWA_HEADS * HEAD_DIM
    kb = w_in[:, o:o + SWA_KV_HEADS * HEAD_DIM]
    o += SWA_KV_HEADS * HEAD_DIM
    vb = w_in[:, o:]
    kr_rot = _rot_cols(kr, 1, MLA_ROPE)
    lead = jnp.zeros((D_MODEL, MLA_NOPE), F32)
    tail = jnp.zeros((D_MODEL, LANES - MLA_NOPE - MLA_ROPE), F32)
    kr_pad = jnp.concatenate([lead, kr, tail], axis=1)
    kr_rot_pad = jnp.concatenate([lead, kr_rot, tail], axis=1)
    w_big = jnp.concatenate([cq, ckv, qb, _rot_cols(qb, SWA_HEADS, HEAD_DIM), kb,
                             _rot_cols(kb, SWA_KV_HEADS, HEAD_DIM), vb, kr_pad, kr_rot_pad], axis=1)
    dq = MLA_NOPE + MLA_ROPE
    wq = w_uq.reshape(MLA_Q_RANK, MLA_HEADS, dq)
    rope = wq[..., MLA_NOPE:]
    rope_rot = jnp.concatenate([-rope[..., MLA_ROPE // 2:], rope[..., :MLA_ROPE // 2]], axis=-1)
    wq_rot = jnp.concatenate([jnp.zeros_like(wq[..., :MLA_NOPE]), rope_rot], axis=-1)
    wq_pad = _pad_heads(wq.reshape(MLA_Q_RANK, -1), MLA_HEADS, dq)
    wq_rot_pad = _pad_heads(wq_rot.reshape(MLA_Q_RANK, -1), MLA_HEADS, dq)
    w_q = jnp.concatenate([wq_pad, wq_rot_pad], axis=1)
    wkv = w_ukv.reshape(MLA_KV_RANK, MLA_HEADS, MLA_NOPE + MLA_V)
    wk_pad = _pad_heads(wkv[..., :MLA_NOPE].reshape(MLA_KV_RANK, -1), MLA_HEADS, MLA_NOPE)
    wv = wkv[..., MLA_NOPE:].reshape(MLA_KV_RANK, MLA_HEADS * MLA_V)
    w_kv = jnp.concatenate([wk_pad, wv], axis=1)
    return w_big.astype(BF16), w_q.astype(BF16), w_kv.astype(BF16)


def _permute_swa_rows(w_out):
    n = MLA_HEADS * MLA_V
    wb = w_out[n:].reshape(SWA_HEADS, HEAD_DIM, D_MODEL)[jnp.array(SWA_HEAD_ORDER)]
    return jnp.concatenate([w_out[:n], wb.reshape(SWA_HEADS * HEAD_DIM, D_MODEL)], axis=0)


def _prep_cd(w_in):
    return w_in.astype(BF16)


def _pre_ab_kernel(x_ref, g_ref, wbig_ref, qn_ref, wq_ref, kvn_ref, wkv_ref, ca_ref, sa_ref, cb_ref, sb_ref,
                   qm_ref, km_ref, vm_ref, qs_ref, ks_ref, vs_ref):
    h = _rms(x_ref[...], g_ref[...]).astype(BF16)
    proj = _dot(h, wbig_ref[...])
    ca, sa, cb, sb = ca_ref[...], sa_ref[...], cb_ref[...], sb_ref[...]
    o = 0
    c_q = proj[:, o:o + MLA_Q_RANK]
    o += MLA_Q_RANK
    c_kv = proj[:, o:o + MLA_KV_RANK]
    o += MLA_KV_RANK
    nq = SWA_HEADS * HEAD_DIM
    q_b, q_b_rot = proj[:, o:o + nq], proj[:, o + nq:o + 2 * nq]
    o += 2 * nq
    k_b, k_b_rot, v_b = proj[:, o:o + LANES], proj[:, o + LANES:o + 2 * LANES], proj[:, o + 2 * LANES:o + 3 * LANES]
    o += 3 * LANES
    k_r, k_r_rot = proj[:, o:o + LANES], proj[:, o + LANES:o + 2 * LANES]

    c_swa = HEAD_DIM ** -0.5 * LOG2E
    qs_ref[...] = jnp.concatenate(
        [(q_b[:, i * LANES:(i + 1) * LANES] * cb + q_b_rot[:, i * LANES:(i + 1) * LANES] * sb) * c_swa
         for i in range(nq // LANES)], axis=1).astype(BF16)
    ks_ref[...] = (k_b * cb + k_b_rot * sb).astype(BF16)
    vs_ref[...] = v_b.astype(BF16)

    q_all = _dot(_rms(c_q, qn_ref[...]).astype(BF16), wq_ref[...])
    nh = MLA_HEADS * LANES
    c_mla = (MLA_NOPE + MLA_ROPE) ** -0.5 * LOG2E
    qm_ref[...] = jnp.concatenate(
        [(q_all[:, i * LANES:(i + 1) * LANES] * ca + q_all[:, nh + i * LANES:nh + (i + 1) * LANES] * sa) * c_mla
         for i in range(MLA_HEADS)], axis=1).astype(BF16)
    kv = _dot(_rms(c_kv, kvn_ref[...]).astype(BF16), wkv_ref[...])
    k_rope = k_r * ca + k_r_rot * sa
    km_ref[...] = jnp.concatenate(
        [kv[:, i * LANES:(i + 1) * LANES] + k_rope for i in range(MLA_HEADS)], axis=1).astype(BF16)
    vm_ref[...] = kv[:, nh:].astype(BF16)


def _pre_ab(x, g, w_big, qn, w_q, kvn, w_kv, tabs, tm=512):
    t = x.shape[0]
    nsb = SEQ // tm
    full = lambda a: pl.BlockSpec(a.shape, lambda i: (0,) * a.ndim)
    tab = pl.BlockSpec((tm, LANES), lambda i: (i % nsb, 0))
    row = lambda n: pl.BlockSpec((tm, n), lambda i: (i, 0))
    outs = [(MLA_HEADS * LANES, BF16), (MLA_HEADS * LANES, BF16), (MLA_HEADS * MLA_V, BF16),
            (SWA_HEADS * HEAD_DIM, BF16), (LANES, BF16), (LANES, BF16)]
    return pl.pallas_call(
        _pre_ab_kernel,
        grid=(t // tm,),
        in_specs=[row(D_MODEL), full(g), full(w_big), full(qn), full(w_q), full(kvn), full(w_kv), tab, tab, tab, tab],
        out_specs=[row(n) for n, _ in outs],
        out_shape=[jax.ShapeDtypeStruct((t, n), d) for n, d in outs],
        compiler_params=_cparams("parallel"),
        name="pre_ab",
    )(x, g, w_big, qn, w_q, kvn, w_kv, *tabs)


def _pre_cd_kernel(x_ref, g_ref, wbig_ref, cb_ref, sb_ref, qc_ref, kc_ref, vc_ref, qd_ref, kd_ref, vd_ref):
    h = _rms(x_ref[...], g_ref[...]).astype(BF16)
    cb, sb = cb_ref[...], sb_ref[...]
    n = DIL_HEADS * HEAD_DIM
    w = wbig_ref

    c_att = HEAD_DIM ** -0.5 * LOG2E

    lane = lax.broadcasted_iota(jnp.int32, (1, LANES), 1)
    first = (lane & (HEAD_DIM - 1)) < HEAD_DIM // 2
    sb_signed = jnp.where(first, -sb, sb)

    def roped(o, c):
        a = _dot(h, w[:, o:o + n])
        tiles = []
        for i in range(n // LANES):
            t = a[:, i * LANES:(i + 1) * LANES]
            rot = jnp.where(first, pltpu.roll(t, LANES - HEAD_DIM // 2, 1), pltpu.roll(t, HEAD_DIM // 2, 1))
            tiles.append((t * cb + rot * sb_signed) * c)
        return jnp.concatenate(tiles, axis=1).astype(BF16)

    qc_ref[...] = roped(0, c_att)
    kc_ref[...] = roped(n, 1.0)
    vc_ref[...] = _dot(h, w[:, 2 * n:3 * n]).astype(BF16)
    qd_ref[...] = (_dot(h, w[:, 3 * n:4 * n]) * c_att).astype(BF16)
    kd_ref[...] = _dot(h, w[:, 4 * n:5 * n]).astype(BF16)
    vd_ref[...] = _dot(h, w[:, 5 * n:6 * n]).astype(BF16)


def _pre_cd(x, g, w_big, tabs, tm=512):
    t = x.shape[0]
    nsb = SEQ // tm
    n = DIL_HEADS * HEAD_DIM
    full = lambda a: pl.BlockSpec(a.shape, lambda i: (0,) * a.ndim)
    tab = pl.BlockSpec((tm, LANES), lambda i: (i % nsb, 0))
    row = lambda m: pl.BlockSpec((tm, m), lambda i: (i, 0))
    return pl.pallas_call(
        _pre_cd_kernel,
        grid=(t // tm,),
        in_specs=[row(D_MODEL), full(g), full(w_big), tab, tab],
        out_specs=[row(n)] * 6,
        out_shape=[jax.ShapeDtypeStruct((t, n), BF16)] * 6,
        compiler_params=_cparams("parallel"),
        name="pre_cd",
    )(x, g, w_big, tabs[2], tabs[3])


def _low_half():
    return lax.broadcasted_iota(jnp.int32, (1, LANES), 1) < LANES // 2


def _split_heads(x2, low):
    zero = jnp.zeros_like(x2)
    return jnp.where(low, x2, zero), jnp.where(low, zero, x2)


def _softmax_pv(s, v2):
    m = jnp.max(s, axis=-1, keepdims=True)
    e = jnp.exp2(s - m)
    l = jnp.sum(e, axis=-1, keepdims=True)
    return _dot(e.astype(BF16), v2) / l


def _mla_kernel(q_ref, k_ref, v_ref, o_ref):
    low = _low_half()
    outs = []
    for p in range(MLA_HEADS // 2):
        v2 = v_ref[0, :, p * LANES:(p + 1) * LANES]
        halves = [_softmax_pv(_dot_nt(q_ref[0, :, h * LANES:(h + 1) * LANES], k_ref[0, :, h * LANES:(h + 1) * LANES]), v2)
                  for h in (2 * p, 2 * p + 1)]
        outs.append(jnp.where(low, halves[0], halves[1]))
    o_ref[0] = jnp.concatenate(outs, axis=1).astype(o_ref.dtype)


def _mla_attn(q, k, v, tq=512):
    b = q.shape[0]
    nq, nv = q.shape[-1], v.shape[-1]
    return pl.pallas_call(
        _mla_kernel,
        grid=(b, SEQ // tq),
        in_specs=[pl.BlockSpec((1, tq, nq), lambda i, j: (i, j, 0)),
                  pl.BlockSpec((1, SEQ, nq), lambda i, j: (i, 0, 0)),
                  pl.BlockSpec((1, SEQ, nv), lambda i, j: (i, 0, 0))],
        out_specs=pl.BlockSpec((1, tq, nv), lambda i, j: (i, j, 0)),
        out_shape=jax.ShapeDtypeStruct((b, SEQ, nv), BF16),
        compiler_params=_cparams("parallel", "parallel"),
        name="mla_attn",
    )(q, k, v)


def _swa_kernel(q_ref, kp_ref, kc_ref, kn_ref, vp_ref, vc_ref, vn_ref, sink_ref, o_ref, *, blk):
    i = pl.program_id(1)
    low = _low_half()
    k = jnp.concatenate([kp_ref[0], kc_ref[0], kn_ref[0]], axis=0)
    v = jnp.concatenate([vp_ref[0], vc_ref[0], vn_ref[0]], axis=0)
    qpos = i * blk + lax.broadcasted_iota(jnp.int32, (blk, 3 * blk), 0)
    kpos = (i - 1) * blk + lax.broadcasted_iota(jnp.int32, (blk, 3 * blk), 1)
    valid = (jnp.abs(qpos - kpos) <= SWA_HALF_WINDOW) & (kpos >= 0) & (kpos < SEQ)
    mask = jnp.where(valid, 0.0, NEG_INF)
    tiles = []
    for r in range(SWA_GROUP):
        q = jnp.concatenate(_split_heads(q_ref[0, :, r * LANES:(r + 1) * LANES], low), axis=0)
        s = (_dot_nt(q, k).reshape(2, blk, 3 * blk) + mask[None]).reshape(2 * blk, 3 * blk)
        m = jnp.max(s, axis=-1, keepdims=True)
        e = jnp.exp2(s - m)
        l = jnp.sum(e, axis=-1, keepdims=True)
        lse = m * LN2 + jnp.log(l)
        keep = jax.nn.sigmoid(lse - sink_ref[2 * r * blk:(2 * r + 2) * blk])
        o = _dot(e.astype(BF16), v) * (keep / l)
        tiles.append(jnp.where(low, o[:blk], o[blk:]))
    o_ref[0] = jnp.concatenate(tiles, axis=1).astype(o_ref.dtype)


def _swa_attn(q, k, v, sink_rows, blk=SWA_BLOCK):
    b = q.shape[0]
    nb = SEQ // blk
    kv = lambda f: pl.BlockSpec((1, blk, LANES), f)
    prev = lambda i, j: (i, jnp.maximum(j - 1, 0), 0)
    cur = lambda i, j: (i, j, 0)
    nxt = lambda i, j: (i, jnp.minimum(j + 1, nb - 1), 0)
    return pl.pallas_call(
        functools.partial(_swa_kernel, blk=blk),
        grid=(b, nb),
        in_specs=[pl.BlockSpec((1, blk, SWA_HEADS * HEAD_DIM), cur), kv(prev), kv(cur), kv(nxt),
                  kv(prev), kv(cur), kv(nxt), pl.BlockSpec(sink_rows.shape, lambda i, j: (0, 0))],
        out_specs=pl.BlockSpec((1, blk, SWA_HEADS * HEAD_DIM), cur),
        out_shape=jax.ShapeDtypeStruct((b, SEQ, SWA_HEADS * HEAD_DIM), BF16),
        compiler_params=_cparams("parallel", "parallel"),
        name="swa_attn",
    )(q, k, k, k, v, v, v, sink_rows)


DIL_BLK = 128
DIL_WIN = 256


def _banded_blocks(q_blocks, k_wins, v_wins, offsets, half, low):
    n = len(q_blocks)
    win = k_wins[0].shape[0]
    q3 = jnp.stack([jnp.concatenate(_split_heads(q, low), axis=0) for q in q_blocks])
    k3, v3 = jnp.stack(k_wins), jnp.stack(v_wins)
    rel = (lax.broadcasted_iota(jnp.int32, (DIL_BLK, win), 0) - lax.broadcasted_iota(jnp.int32, (DIL_BLK, win), 1))
    masks = {off: jnp.where(jnp.abs(rel - off) <= half, 0.0, NEG_INF) for off in set(offsets)}
    mask3 = jnp.stack([masks[off] for off in offsets])
    s = jnp.einsum("bqd,bkd->bqk", q3, k3, preferred_element_type=F32)
    s = (s.reshape(n, 2, DIL_BLK, win) + mask3[:, None]).reshape(n, 2 * DIL_BLK, win)
    m = jnp.max(s, axis=-1, keepdims=True)
    e = jnp.exp2(s - m)
    l = jnp.sum(e, axis=-1, keepdims=True)
    o = jnp.einsum("bqk,bkd->bqd", e.astype(BF16), v3, preferred_element_type=F32)
    pick = lambda t: jnp.where(low, t[:, :DIL_BLK], t[:, DIL_BLK:])
    return pick(o), pick(jnp.broadcast_to(m, o.shape)), pick(jnp.broadcast_to(l, o.shape))


def _dilated_kernel(q_ref, k_ref, v_ref, o_ref, qf, kf, vf, *stats):
    low = _low_half()
    qf[...] = q_ref[0].astype(F32)
    kf[...] = k_ref[0].astype(F32)
    vf[...] = v_ref[0].astype(F32)
    for n, (window, dil) in enumerate(DIL_BRANCHES):
        length = SEQ // dil
        half = window // (2 * dil)
        win = min(DIL_WIN, length)
        per_stream = length // DIL_BLK
        starts = [min(max(a * DIL_BLK - half, 0), length - win) for a in range(per_stream)]
        q_blocks, k_wins, v_wins, offsets = [], [], [], []
        for r in range(dil):
            stream = lambda ref: ref[pl.ds(r, length, stride=dil), :].astype(BF16)
            qs, ks, vs = stream(qf), stream(kf), stream(vf)
            for a, w0 in enumerate(starts):
                q_blocks.append(qs[a * DIL_BLK:(a + 1) * DIL_BLK])
                k_wins.append(ks[w0:w0 + win])
                v_wins.append(vs[w0:w0 + win])
                offsets.append(w0 - a * DIL_BLK)
        tiles = _banded_blocks(q_blocks, k_wins, v_wins, offsets, half, low)
        for tile, dst in zip(tiles, stats[3 * n:3 * n + 3]):
            for r in range(dil):
                dst[pl.ds(r, length, stride=dil), :] = tile[r * per_stream:(r + 1) * per_stream].reshape(length, LANES)
    accs, ms, ls = stats[0::3], stats[1::3], stats[2::3]
    m_all = functools.reduce(jnp.maximum, [m[...] for m in ms])
    ws = [jnp.exp2(m[...] - m_all) for m in ms]
    num = sum(w * a[...] for w, a in zip(ws, accs))
    den = sum(w * l[...] for w, l in zip(ws, ls))
    o_ref[0] = (num / den).astype(o_ref.dtype)


def _dilated_attn(q, k, v):
    b = q.shape[0]
    spec = pl.BlockSpec((1, SEQ, LANES), lambda i, p: (i, 0, p))
    return pl.pallas_call(
        _dilated_kernel,
        grid=(b, DIL_HEADS // 2),
        in_specs=[spec, spec, spec],
        out_specs=spec,
        out_shape=jax.ShapeDtypeStruct((b, SEQ, DIL_HEADS * HEAD_DIM), BF16),
        scratch_shapes=[pltpu.VMEM((SEQ, LANES), F32)] * (3 + 3 * len(DIL_BRANCHES)),
        compiler_params=_cparams("parallel", "parallel"),
        name="dilated_attn",
    )(q, k, v)


NA_Q_ROWS = 4
NA_Q = NA_Q_ROWS * GRID_W
NA_K_ROWS = 12
NA_K = NA_K_ROWS * GRID_W
NA_GROUPS = SEQ // NA_Q
NA_RPB_ROWS = 2 * NA_KH - 1
NA_RPB_COLS = 2 * NA_KW - 1
NA_WIN0 = (lambda a: 0, lambda a: a, lambda a: NA_Q_ROWS)
NA_RHO0 = (NA_KH - 1, NA_KH - 1 - NA_Q_ROWS, NA_KH - 1 - 2 * NA_Q_ROWS)


def _na_bias_kernel(rpb_ref, o_ref):
    pair = pl.program_id(0)
    c = lax.broadcasted_iota(jnp.int32, (GRID_W, LANES), 0)
    lane = lax.broadcasted_iota(jnp.int32, (GRID_W, LANES), 1)
    odd = lane >= GRID_W
    cp = lane & (GRID_W - 1)
    c0 = jnp.clip(c - NA_KW // 2, 0, GRID_W - NA_KW)
    col_ok = (cp >= c0) & (cp < c0 + NA_KW)
    d = cp - c + (NA_KW - 1)
    neg = jnp.full((GRID_W, LANES), NEG_INF, F32)

    for a in range(2):
        head = 2 * pair + a
        cache = {}

        def tile(rho_even, ok_even, ok_odd):
            key = (rho_even, ok_even, ok_odd)
            if key not in cache:
                acc = neg
                for kk in range(NA_RPB_COLS):
                    lo = rpb_ref[(head * NA_RPB_ROWS + rho_even) * NA_RPB_COLS + kk] if ok_even else NEG_INF
                    hi = rpb_ref[(head * NA_RPB_ROWS + rho_even + 1) * NA_RPB_COLS + kk] if ok_odd else NEG_INF
                    acc = jnp.where(d == kk, jnp.where(odd, hi, lo), acc)
                cache[key] = jnp.where(col_ok, acc * LOG2E, NEG_INF)
            return cache[key]

        for pat in range(3):
            for qa in range(NA_Q_ROWS):
                w0 = NA_WIN0[pat](qa)
                tiles = []
                for j in range(0, NA_K_ROWS, 2):
                    ok_even = w0 <= j < w0 + NA_KH
                    ok_odd = w0 <= j + 1 < w0 + NA_KH
                    rho = j - qa + NA_RHO0[pat]
                    tiles.append(tile(rho, ok_even, ok_odd) if (ok_even or ok_odd) else neg)
                o_ref[pat, 0, a, qa * GRID_W:(qa + 1) * GRID_W, :] = jnp.concatenate(tiles, axis=1)


def _na_bias(rpb):
    return pl.pallas_call(
        _na_bias_kernel,
        grid=(NA_HEADS // 2,),
        in_specs=[pl.BlockSpec(memory_space=pltpu.SMEM)],
        out_specs=pl.BlockSpec((3, 1, 2, NA_Q, NA_K), lambda p: (0, p, 0, 0, 0)),
        out_shape=jax.ShapeDtypeStruct((3, NA_HEADS // 2, 2, NA_Q, NA_K), F32),
        compiler_params=_cparams("parallel"),
        name="na_bias",
    )(rpb.reshape(-1))


NA_STEP_PAIRS = 4


def _na_kernel(q_ref, k0_ref, k1_ref, k2_ref, v0_ref, v1_ref, v2_ref, bias_ref, o_ref):
    low = _low_half()
    outs = []
    for p in range(NA_STEP_PAIRS):
        sl = slice(p * LANES, (p + 1) * LANES)
        k = jnp.concatenate([k0_ref[0, :, sl], k1_ref[0, :, sl], k2_ref[0, :, sl]], axis=0)
        v = jnp.concatenate([v0_ref[0, :, sl], v1_ref[0, :, sl], v2_ref[0, :, sl]], axis=0)
        halves = [_softmax_pv(_dot_nt(q, k) + bias_ref[0, p, a], v)
                  for a, q in enumerate(_split_heads(q_ref[0, :, sl], low))]
        outs.append(jnp.where(low, halves[0], halves[1]))
    o_ref[0] = jnp.concatenate(outs, axis=1).astype(o_ref.dtype)


def _na_attn(q, k, v, bias):
    b = q.shape[0]
    width = NA_STEP_PAIRS * LANES
    first_block = lambda g: jnp.clip(g - 1, 0, NA_GROUPS - NA_K // NA_Q)
    pattern = lambda g: jnp.where(g == 0, 0, jnp.where(g == NA_GROUPS - 1, 2, 1))
    cur = pl.BlockSpec((1, NA_Q, width), lambda p, g, i: (i, g, p))
    kv = lambda n: pl.BlockSpec((1, NA_Q, width), lambda p, g, i: (i, first_block(g) + n, p))
    return pl.pallas_call(
        _na_kernel,
        grid=(NA_HEADS // 2 // NA_STEP_PAIRS, NA_GROUPS, b),
        in_specs=[cur, kv(0), kv(1), kv(2), kv(0), kv(1), kv(2),
                  pl.BlockSpec((1, NA_STEP_PAIRS, 2, NA_Q, NA_K), lambda p, g, i: (pattern(g), p, 0, 0, 0))],
        out_specs=cur,
        out_shape=jax.ShapeDtypeStruct((b, SEQ, NA_HEADS * HEAD_DIM), BF16),
        compiler_params=_cparams("parallel", "parallel", "parallel"),
        name="na_attn",
    )(q, k, k, k, v, v, v, bias)


def _post_kernel(oa_ref, ob_ref, x_ref, w1_ref, w2_ref, g_ref, wr_ref, xh_ref, aff_ref):
    x = x_ref[...] + _dot(oa_ref[...], w1_ref[...]) + _dot(ob_ref[...], w2_ref[...])
    h = _rms(x, g_ref[...])
    xh_ref[:, :D_MODEL] = x
    xh_ref[:, D_MODEL:] = h
    logits = _dot_nt(wr_ref[...], h.astype(BF16))
    e = jnp.exp(logits - jnp.max(logits, axis=0, keepdims=True))
    aff_ref[...] = e / jnp.sum(e, axis=0, keepdims=True)


def _post(oa, ob, x, w_out, g, w_router, tm=512):
    t = x.shape[0]
    half = oa.shape[1]
    w1, w2 = w_out[:half].astype(BF16), w_out[half:].astype(BF16)
    wr = w_router.T.astype(BF16)
    full = lambda a: pl.BlockSpec(a.shape, lambda i: (0,) * a.ndim)
    row = lambda n: pl.BlockSpec((tm, n), lambda i: (i, 0))
    return pl.pallas_call(
        _post_kernel,
        grid=(t // tm,),
        in_specs=[row(half), row(half), row(D_MODEL), full(w1), full(w2), full(g), full(wr)],
        out_specs=[row(2 * D_MODEL), pl.BlockSpec((N_EXPERTS, tm), lambda i: (0, i))],
        out_shape=[jax.ShapeDtypeStruct((t, 2 * D_MODEL), F32), jax.ShapeDtypeStruct((N_EXPERTS, t), F32)],
        compiler_params=_cparams("parallel"),
        name="post_attn",
    )(oa, ob, x, w1, w2, g, wr)


def _select_kernel(aff_ref, idx_ref, gate_ref, incl_s, a_hi, a_mid, a_lo, start_s, end_s, cnt_s, *, cap, tc):
    s_id = pl.program_id(1)
    rows = aff_ref.shape[1]

    @pl.when(s_id == 0)
    def _():
        aff = aff_ref[0]
        bits = pltpu.bitcast(aff, jnp.int32)

        def search(i, prefix):
            cand = prefix | jnp.left_shift(jnp.int32(1), 30 - i)
            n = jnp.sum((bits >= cand).astype(F32), keepdims=True)
            return jnp.where(n >= cap, cand, prefix)

        thr = lax.fori_loop(0, 31, search, jnp.zeros((1, 1), jnp.int32))
        gt = bits > thr
        eq = bits == thr
        need = cap - jnp.sum(gt.astype(F32), keepdims=True)
        li = lax.broadcasted_iota(jnp.int32, (LANES, LANES), 0)
        lj = lax.broadcasted_iota(jnp.int32, (LANES, LANES), 1)
        tri = (li <= lj).astype(BF16)
        ones = jnp.ones((LANES, LANES), BF16)
        ri = lax.broadcasted_iota(jnp.int32, (rows, rows), 0)
        rj = lax.broadcasted_iota(jnp.int32, (rows, rows), 1)
        below = (rj < ri).astype(BF16)
        above = (ri < rj).astype(BF16)

        eqb = eq.astype(BF16)
        eq_rank = _dot(below, _dot(eqb, ones).astype(BF16)) + _dot(eqb, tri) - eqb.astype(F32)
        sel = (gt | (eq & (eq_rank < need))).astype(BF16)
        incl_s[...] = _dot(sel, tri).astype(BF16)
        cnt = _dot_nt(jnp.ones((8, LANES), BF16), sel)
        start = _dot(cnt.astype(BF16), above)
        cnt_s[...] = cnt
        start_s[...] = start
        end_s[...] = start + cnt
        hi = aff.astype(BF16)
        r1 = aff - hi.astype(F32)
        mid = r1.astype(BF16)
        a_hi[...] = hi
        a_mid[...] = mid
        a_lo[...] = (r1 - mid.astype(F32)).astype(BF16)

    c = (s_id * tc + lax.broadcasted_iota(jnp.int32, (tc, rows), 0)).astype(F32)
    start, end, cnt = start_s[0:1, :], end_s[0:1, :], cnt_s[0:1, :]
    before = c >= end
    row_id = jnp.sum(before.astype(F32), axis=1, keepdims=True)
    row_start = jnp.sum(jnp.where(before, cnt, 0.0), axis=1, keepdims=True)
    onehot = ((c >= start) & (c < end)).astype(BF16)
    within = _dot(onehot, incl_s[...])
    k = c[:, 0:1] - row_start
    lane = jnp.sum((within <= k).astype(F32), axis=1, keepdims=True)
    idx_ref[0] = (row_id * LANES + lane).astype(jnp.int32)
    vals = _dot(onehot, a_hi[...]) + _dot(onehot, a_mid[...]) + _dot(onehot, a_lo[...])
    lane_iota = lax.broadcasted_iota(jnp.int32, (tc, LANES), 1).astype(F32)
    gate = jnp.sum(jnp.where(lane_iota == lane, vals, 0.0), axis=1, keepdims=True)
    gate_ref[0] = jnp.broadcast_to(gate, (tc, LANES))


def _select(aff_t, cap, tc=1024):
    t = aff_t.shape[1]
    rows = t // LANES
    aff3 = aff_t.reshape(N_EXPERTS, rows, LANES)
    idx, gate = pl.pallas_call(
        functools.partial(_select_kernel, cap=cap, tc=tc),
        grid=(N_EXPERTS, cap // tc),
        in_specs=[pl.BlockSpec((1, rows, LANES), lambda e, s: (e, 0, 0))],
        out_specs=[pl.BlockSpec((1, tc, 1), lambda e, s: (e, s, 0)),
                   pl.BlockSpec((1, tc, LANES), lambda e, s: (e, s, 0))],
        out_shape=[jax.ShapeDtypeStruct((N_EXPERTS, cap, 1), jnp.int32),
                   jax.ShapeDtypeStruct((N_EXPERTS, cap, LANES), F32)],
        scratch_shapes=[pltpu.VMEM((rows, LANES), BF16)] * 4 + [pltpu.VMEM((8, rows), F32)] * 3,
        compiler_params=_cparams("parallel", "arbitrary"),
        name="ec_select",
    )(aff3)
    return idx.reshape(N_EXPERTS * cap), gate.reshape(N_EXPERTS * cap, LANES)


FFN_SUB = 4


def _ffn_kernel(idx_ref, gate_ref, wg_ref, wu_ref, wd_ref, xh_in, xh_hbm, *scratch, tc, n_steps):
    del xh_in
    bufs, (sem_g, sem_o) = scratch[:FFN_SUB], scratch[FFN_SUB:]
    e, s = pl.program_id(0), pl.program_id(1)
    step_base = (e * n_steps + s) * (FFN_SUB * tc)

    def start_gather(base, j):
        for i in range(tc):
            t = idx_ref[base + i]
            pltpu.make_async_copy(xh_hbm.at[pl.ds(t, 1)], bufs[j].at[pl.ds(i, 1)], sem_g.at[j]).start(priority=i % 2)

    def wait_gather(j):
        pltpu.make_async_copy(xh_hbm.at[pl.ds(0, tc)], bufs[j], sem_g.at[j]).wait()

    def start_scatter(base, j):
        for i in range(tc):
            t = idx_ref[base + i]
            pltpu.make_async_copy(bufs[j].at[pl.ds(i, 1), pl.ds(0, D_MODEL)],
                                  xh_hbm.at[pl.ds(t, 1), pl.ds(0, D_MODEL)], sem_o.at[j]).start(priority=i % 2)

    def wait_scatter(j):
        pltpu.make_async_copy(bufs[j].at[:, pl.ds(0, D_MODEL)],
                              xh_hbm.at[pl.ds(0, tc), pl.ds(0, D_MODEL)], sem_o.at[j]).wait()

    @pl.when(s == 0)
    def _():
        @pl.when(e > 0)
        def _():
            for j in range(FFN_SUB):
                wait_scatter(j)

        start_gather(step_base, 0)

    wait_gather(0)
    for j in range(FFN_SUB):
        if j + 1 < FFN_SUB:
            @pl.when(s > 0)
            def _():
                wait_scatter(j + 1)

            start_gather(step_base + (j + 1) * tc, j + 1)
        else:
            @pl.when(s + 1 < n_steps)
            def _():
                wait_scatter(0)
                start_gather(step_base + FFN_SUB * tc, 0)
        if j > 0:
            start_scatter(step_base + (j - 1) * tc, j - 1)

        h = bufs[j][:, D_MODEL:].astype(BF16)
        g = _dot(h, wg_ref[0])
        u = _dot(h, wu_ref[0])
        act = (g * jax.nn.sigmoid(g) * u).astype(BF16)
        y = _dot(act, wd_ref[0]) * gate_ref[j * tc:(j + 1) * tc, 0:1]
        bufs[j][:, :D_MODEL] = bufs[j][:, :D_MODEL] + y
        if j + 1 < FFN_SUB:
            wait_gather(j + 1)
    start_scatter(step_base + (FFN_SUB - 1) * tc, FFN_SUB - 1)

    @pl.when((e == N_EXPERTS - 1) & (s == n_steps - 1))
    def _():
        for j in range(FFN_SUB):
            wait_scatter(j)


def _ffn(idx, gate, xh, w_gate, w_up, w_down, cap, tc=256):
    n_steps = cap // (FFN_SUB * tc)
    assert n_steps * FFN_SUB * tc == cap
    wspec = pl.BlockSpec((1, D_MODEL, D_MODEL), lambda e, s, idx_ref: (e, 0, 0))
    any_spec = pl.BlockSpec(memory_space=pl.ANY)
    grid_spec = pltpu.PrefetchScalarGridSpec(
        num_scalar_prefetch=1,
        grid=(N_EXPERTS, n_steps),
        in_specs=[pl.BlockSpec((FFN_SUB * tc, LANES), lambda e, s, idx_ref: (e * n_steps + s, 0)),
                  wspec, wspec, wspec, any_spec],
        out_specs=any_spec,
        scratch_shapes=[pltpu.VMEM((tc, 2 * D_MODEL), F32)] * FFN_SUB
        + [pltpu.SemaphoreType.DMA((FFN_SUB,)), pltpu.SemaphoreType.DMA((FFN_SUB,))],
    )
    return pl.pallas_call(
        functools.partial(_ffn_kernel, tc=tc, n_steps=n_steps),
        grid_spec=grid_spec,
        out_shape=jax.ShapeDtypeStruct(xh.shape, F32),
        input_output_aliases={5: 0},
        compiler_params=_cparams("arbitrary", "arbitrary"),
        name="ec_ffn",
    )(idx, gate, w_gate, w_up, w_down, xh)


def _final_kernel(x_ref, g_ref, o_ref):
    o_ref[...] = _rms(x_ref[...], g_ref[...])


def _final_norm(x, g, tm=1024):
    t = x.shape[0]
    return pl.pallas_call(
        _final_kernel,
        grid=(t // tm,),
        in_specs=[pl.BlockSpec((tm, D_MODEL), lambda i: (i, 0)), pl.BlockSpec(g.shape, lambda i: (0, 0))],
        out_specs=pl.BlockSpec((tm, D_MODEL), lambda i: (i, 0)),
        out_shape=jax.ShapeDtypeStruct((t, D_MODEL), F32),
        compiler_params=_cparams("parallel"),
        name="final_norm",
    )(x, g)


def _moe(xh, aff_t, w_gate, w_up, w_down):
    cap = EC_CAPACITY_FACTOR * xh.shape[0] // N_EXPERTS
    idx, gate = _select(aff_t, cap)
    return _ffn(idx, gate, xh, w_gate, w_up, w_down, cap)


def _trunk(x3, p):
    b = x3.shape[0]
    t = b * SEQ
    x = x3.reshape(t, D_MODEL)
    seq = lambda a: a.reshape(b, SEQ, a.shape[-1])
    flat = lambda a: a.reshape(t, a.shape[-1])

    qm, km, vm, qs, ks, vs = _pre_ab(x, p["g_mix0"], p["w_big_ab"], p["q_norm"], p["w_q"], p["kv_norm"], p["w_kv"],
                                     p["tabs"])
    o_a = _mla_attn(seq(qm), seq(km), seq(vm))
    o_b = _swa_attn(seq(qs), seq(ks), seq(vs), p["sink"])
    xh, aff_t = _post(flat(o_a), flat(o_b), x, p["w_out_ab"], p["g_ffn0"], p["w_router0"])
    x = _moe(xh, aff_t, p["w_gate0"], p["w_up0"], p["w_down0"])

    qc, kc, vc, qd, kd, vd = _pre_cd(x, p["g_mix1"], p["w_big_cd"], p["tabs"])
    o_c = _dilated_attn(seq(qc), seq(kc), seq(vc))
    o_d = _na_attn(seq(qd), seq(kd), seq(vd), p["na_bias"])
    xh, aff_t = _post(flat(o_c), flat(o_d), x, p["w_out_cd"], p["g_ffn1"], p["w_router1"])
    x = _moe(xh, aff_t, p["w_gate1"], p["w_up1"], p["w_down1"])
    return _final_norm(x, p["g_final"]).reshape(b, SEQ, D_MODEL)


def kernel(x_prompt, x_sample, norm_mix, norm_ffn, norm_final, w_in_ab, mla_q_norm, mla_w_uq, mla_kv_norm,
           mla_w_ukv, swa_sink, w_out_ab, w_in_cd, na_rpb, w_out_cd, w_router, w_gate, w_up, w_down):
    w_big_ab, w_q, w_kv = _prep_ab(w_in_ab[0], mla_w_uq[0], mla_w_ukv[0])
    p = {
        "tabs": _rope_tables(),
        "g_mix0": norm_mix[0:1], "g_mix1": norm_mix[1:2], "g_ffn0": norm_ffn[0:1], "g_ffn1": norm_ffn[1:2],
        "g_final": norm_final.reshape(1, D_MODEL),
        "w_big_ab": w_big_ab, "w_q": w_q, "w_kv": w_kv,
        "q_norm": mla_q_norm[0:1], "kv_norm": mla_kv_norm[0:1],
        "sink": jnp.repeat(swa_sink[0][jnp.array(SWA_HEAD_ORDER)], SWA_BLOCK).reshape(SWA_HEADS * SWA_BLOCK, 1),
        "w_out_ab": _permute_swa_rows(w_out_ab[0]), "w_big_cd": _prep_cd(w_in_cd[0]), "w_out_cd": w_out_cd[0],
        "na_bias": _na_bias(na_rpb[0]),
        "w_router0": w_router[0], "w_router1": w_router[1],
    }
    for l in range(2):
        p[f"w_gate{l}"] = w_gate[l].astype(BF16)
        p[f"w_up{l}"] = w_up[l].astype(BF16)
        p[f"w_down{l}"] = w_down[l].astype(BF16)
    return _trunk(x_prompt, p), _trunk(x_sample, p)
```
